```python
import jax, jax.numpy as jnp
from jax import lax
import numpy as np

D_MODEL = 1024
BATCH = 8
SEQ = 8192
DEPTH = 2
DEC_BATCH = 32
DEC_SEQ = 64
PAST_LEN = 4096

CHUNK = 64
Q_BLOCK = 128
EPS = 1e-6
D_FF = 2816
SSD_HEADS = 8
SSD_HEAD_DIM = 64
SSD_WIDTH = SSD_HEADS * SSD_HEAD_DIM
SSD_GROUPS = 2
SSD_STATE = 128
SSD_CONV = 4
SSD_CONV_CH = SSD_WIDTH + 2 * SSD_GROUPS * SSD_STATE
GLA_HEADS = 4
GLA_DK = 32
GLA_DV = 64
GLA_WIDTH = GLA_HEADS * GLA_DV
GLA_RANK = 16
GLA_TAU = 16.0
FOX_HEADS = 4
FOX_HEAD_DIM = 64
FOX_WIDTH = FOX_HEADS * FOX_HEAD_DIM
D_MIX = SSD_WIDTH + GLA_WIDTH + FOX_WIDTH
IN_SPLITS = (SSD_WIDTH, SSD_CONV_CH, SSD_HEADS,
             GLA_HEADS * GLA_DK, GLA_HEADS * GLA_DK, GLA_WIDTH, GLA_WIDTH, GLA_RANK,
             FOX_WIDTH, FOX_WIDTH, FOX_WIDTH, FOX_HEADS)
IN_COLS = sum(IN_SPLITS)

kernel_name = 'hybrid_ssd_gla_fox_macaron_step'


def split_cols(a, sizes):
    out, off = [], 0
    for s in sizes:
        out.append(a[..., off:off + s])
        off += s
    return out


def rmsnorm(x, w):
    xf = x.astype(jnp.float32)
    y = xf * lax.rsqrt(jnp.mean(xf * xf, axis=-1, keepdims=True) + EPS)
    return (y * w.astype(jnp.float32)).astype(x.dtype)


def swiglu(h, w_gate, w_up, w_down):
    return (jax.nn.silu(h @ w_gate) * (h @ w_up)) @ w_down


def causal_conv(u, prev, w, b):
    T = u.shape[1]
    up = jnp.concatenate([prev.astype(u.dtype), u], axis=1)
    y = up[:, 0:T] * w[0]
    for j in range(1, SSD_CONV):
        y = y + up[:, j:j + T] * w[j]
    return jax.nn.silu(y + b), up[:, -(SSD_CONV - 1):]


def ssd_scan(x, dt, A, Bm, Cm, h0):
    Bsz, T, H, P = x.shape
    G, N = Bm.shape[2], Bm.shape[3]
    L = min(CHUNK, T)
    nc = T // L
    x = x.reshape(Bsz, nc, L, H, P)
    dt = dt.reshape(Bsz, nc, L, H)
    Bm = Bm.reshape(Bsz, nc, L, G, N)
    Cm = Cm.reshape(Bsz, nc, L, G, N)
    hg = jnp.arange(H) // (H // G)
    acum = jnp.cumsum(dt * A, axis=2)
    cb = jnp.einsum('bclgn,bcsgn->bclsg', Cm, Bm)[..., hg]
    causal = jnp.tril(jnp.ones((L, L), dtype=bool))[None, None, :, :, None]
    seg = acum[:, :, :, None, :] - acum[:, :, None, :, :]
    decay = jnp.where(causal, jnp.exp(jnp.where(causal, seg, 0.0)), 0.0)
    y_intra = jnp.einsum('bclsh,bcsh,bcshp->bclhp', cb * decay, dt, x)
    Bh = Bm[..., hg, :]
    Ch = Cm[..., hg, :]
    decay_end = jnp.exp(acum[:, :, -1:, :] - acum)
    states = jnp.einsum('bclh,bclhn,bclhp->bchpn', decay_end * dt, Bh, x)
    chunk_decay = jnp.exp(acum[:, :, -1, :])

    def step(h, inp):
        st, cd = inp
        return h * cd[:, :, None, None] + st, h

    h_last, h_prev = lax.scan(step, h0, (jnp.moveaxis(states, 1, 0), jnp.moveaxis(chunk_decay, 1, 0)))
    h_prev = jnp.moveaxis(h_prev, 0, 1)
    y_inter = jnp.einsum('bclhn,bchpn,bclh->bclhp', Ch, h_prev, jnp.exp(acum))
    return (y_intra + y_inter).reshape(Bsz, T, H, P), h_last


def gla_scan(q, k, v, la, s0):
    Bsz, T, H, K = q.shape
    V = v.shape[-1]
    L = min(CHUNK, T)
    nc = T // L
    q = q.reshape(Bsz, nc, L, H, K)
    k = k.reshape(Bsz, nc, L, H, K)
    v = v.reshape(Bsz, nc, L, H, V)
    bcum = jnp.cumsum(la.reshape(Bsz, nc, L, H, K), axis=2)
    qd = q * jnp.exp(bcum)
    kd = k * jnp.exp(-bcum)
    causal = jnp.tril(jnp.ones((L, L), dtype=bool))
    att = jnp.where(causal, jnp.einsum('bclhk,bcshk->bchls', qd, kd), 0.0)
    o_intra = jnp.einsum('bchls,bcshv->bclhv', att, v)
    blast = bcum[:, :, -1]
    kend = k * jnp.exp(blast[:, :, None] - bcum)
    states = jnp.einsum('bclhk,bclhv->bchkv', kend, v)

    def step(S, inp):
        st, bl = inp
        return S * jnp.exp(bl)[..., None] + st, S

    s_last, s_prev = lax.scan(step, s0, (jnp.moveaxis(states, 1, 0), jnp.moveaxis(blast, 1, 0)))
    s_prev = jnp.moveaxis(s_prev, 0, 1)
    o_inter = jnp.einsum('bclhk,bchkv->bclhv', qd, s_prev)
    return (o_intra + o_inter).reshape(Bsz, T, H, V), s_last


def fox_attend(q, k, v, cq, ck, qpos, kpos):
    f32 = jnp.float32
    s = jnp.einsum('bqhd,bkhd->bhqk', q.astype(f32), k.astype(f32)) * (FOX_HEAD_DIM ** -0.5)
    s = s + (jnp.swapaxes(cq, 1, 2)[..., :, None] - jnp.swapaxes(ck, 1, 2)[..., None, :])
    s = jnp.where(kpos[None, :] <= qpos[:, None], s, -jnp.inf)
    p = jax.nn.softmax(s, axis=-1)
    return jnp.einsum('bhqk,bkhd->bqhd', p, v.astype(f32))


def fox_prompt(q, k, v, c):
    B, T, H, D = q.shape
    nb = T // Q_BLOCK
    kpos = jnp.arange(T)
    qb = jnp.swapaxes(q.reshape(B, nb, Q_BLOCK, H, D), 0, 1)
    cb = jnp.swapaxes(c.reshape(B, nb, Q_BLOCK, H), 0, 1)

    def blk(args):
        i, qi, ci = args
        qpos = i * Q_BLOCK + jnp.arange(Q_BLOCK)
        return fox_attend(qi, k, v, ci, c, qpos, kpos)

    o = lax.map(blk, (jnp.arange(nb), qb, cb))
    return jnp.swapaxes(o, 0, 1).reshape(B, T, H, D)


def token_mix(h, p, past):
    conv_prev, ssd_h0, gla_s0, k_past, v_past, lf_past = past
    B, T, _ = h.shape
    f32 = jnp.float32
    u = h @ p['w_in']
    (z, xbc, dt_raw, gq, gk, gv, gg, gr, fq, fk, fv, ff) = split_cols(u, IN_SPLITS)

    xbc_act, conv_new = causal_conv(xbc, conv_prev, p['ssd_conv_w'], p['ssd_conv_b'])
    xs, bs, cs = split_cols(xbc_act.astype(f32), (SSD_WIDTH, SSD_GROUPS * SSD_STATE, SSD_GROUPS * SSD_STATE))
    xs = xs.reshape(B, T, SSD_HEADS, SSD_HEAD_DIM)
    bs = bs.reshape(B, T, SSD_GROUPS, SSD_STATE)
    cs = cs.reshape(B, T, SSD_GROUPS, SSD_STATE)
    dt = jax.nn.softplus(dt_raw.astype(f32) + p['ssd_dt_bias'].astype(f32))
    A = -jnp.exp(p['ssd_a_log'].astype(f32))
    y_ssd, ssd_new = ssd_scan(xs, dt, A, bs, cs, ssd_h0.astype(f32))
    y_ssd = y_ssd + p['ssd_d'].astype(f32)[:, None] * xs
    y_ssd = y_ssd.reshape(B, T, SSD_WIDTH) * jax.nn.silu(z.astype(f32))
    y_ssd = rmsnorm(y_ssd.reshape(B, T, SSD_GROUPS, SSD_WIDTH // SSD_GROUPS),
                    p['ssd_norm'].reshape(SSD_GROUPS, SSD_WIDTH // SSD_GROUPS)).reshape(B, T, SSD_WIDTH)

    q = gq.astype(f32).reshape(B, T, GLA_HEADS, GLA_DK) * (GLA_DK ** -0.5)
    k = gk.astype(f32).reshape(B, T, GLA_HEADS, GLA_DK)
    v = gv.astype(f32).reshape(B, T, GLA_HEADS, GLA_DV)
    la = jax.nn.log_sigmoid((gr @ p['gla_w_gate'] + p['gla_b_gate']).astype(f32)) / GLA_TAU
    o_gla, gla_new = gla_scan(q, k, v, la.reshape(B, T, GLA_HEADS, GLA_DK), gla_s0.astype(f32))
    o_gla = rmsnorm(o_gla, p['gla_norm']) * jax.nn.silu(gg.astype(f32).reshape(B, T, GLA_HEADS, GLA_DV))
    o_gla = o_gla.reshape(B, T, GLA_WIDTH)

    fq_n = rmsnorm(fq.reshape(B, T, FOX_HEADS, FOX_HEAD_DIM), p['fox_q_norm'])
    fk_n = rmsnorm(fk.reshape(B, T, FOX_HEADS, FOX_HEAD_DIM), p['fox_k_norm'])
    fv_h = fv.reshape(B, T, FOX_HEADS, FOX_HEAD_DIM)
    lf = jax.nn.log_sigmoid(ff.astype(f32) + p['fox_f_bias'].astype(f32))
    if k_past is None:
        o_fox = fox_prompt(fq_n, fk_n, fv_h, jnp.cumsum(lf, axis=1))
    else:
        P = k_past.shape[1]
        k_all = jnp.concatenate([k_past.astype(fk_n.dtype), fk_n], axis=1)
        v_all = jnp.concatenate([v_past.astype(fv_h.dtype), fv_h], axis=1)
        c_all = jnp.cumsum(jnp.concatenate([lf_past.astype(f32), lf], axis=1), axis=1)
        o_fox = fox_attend(fq_n, k_all, v_all, c_all[:, P:], c_all, P + jnp.arange(T), jnp.arange(P + T))
    o_fox = o_fox.reshape(B, T, FOX_WIDTH)

    mixed = jnp.concatenate([y_ssd.astype(f32), o_gla, o_fox], axis=-1).astype(h.dtype)
    out = mixed @ p['w_out']
    dt_out = h.dtype
    new_state = (conv_new.astype(dt_out), ssd_new.astype(dt_out), gla_new.astype(dt_out),
                 fk_n.astype(dt_out), fv_h.astype(dt_out), lf.astype(dt_out))
    return out, new_state


def trunk_layer(x, p, past):
    x = x + (0.5 * swiglu(rmsnorm(x, p['norm_ffn1']), p['w1_gate'], p['w1_up'], p['w1_down'])).astype(x.dtype)
    y, new_state = token_mix(rmsnorm(x, p['norm_mix']), p, past)
    x = x + y.astype(x.dtype)
    x = x + (0.5 * swiglu(rmsnorm(x, p['norm_ffn2']), p['w2_gate'], p['w2_up'], p['w2_down'])).astype(x.dtype)
    return x, new_state


def setup_inputs(seed: int = 0) -> dict:
    key = jax.random.key(seed)
    ks = iter(jax.random.split(key, 48))
    f32 = jnp.float32

    def nrm(shape, scale=1.0):
        return scale * jax.random.normal(next(ks), shape, f32)

    def gain(shape):
        return 1.0 + 0.1 * jax.random.normal(next(ks), shape, f32)

    dt0 = jnp.exp(jax.random.uniform(next(ks), (DEPTH, SSD_HEADS), f32, np.log(1e-3), np.log(1e-1)))
    d = {}
    d['x_prompt'] = nrm((BATCH, SEQ, D_MODEL))
    d['x_sample'] = nrm((DEC_BATCH, DEC_SEQ, D_MODEL))
    d['state_ssd_conv'] = nrm((DEPTH, DEC_BATCH, SSD_CONV - 1, SSD_CONV_CH))
    d['state_ssd'] = nrm((DEPTH, DEC_BATCH, SSD_HEADS, SSD_HEAD_DIM, SSD_STATE), 0.3)
    d['state_gla'] = nrm((DEPTH, DEC_BATCH, GLA_HEADS, GLA_DK, GLA_DV), 0.3)
    d['cache_fox_k'] = nrm((DEPTH, DEC_BATCH, PAST_LEN, FOX_HEADS, FOX_HEAD_DIM))
    d['cache_fox_v'] = nrm((DEPTH, DEC_BATCH, PAST_LEN, FOX_HEADS, FOX_HEAD_DIM))
    d['cache_fox_logf'] = jax.nn.log_sigmoid(2.0 + nrm((DEPTH, DEC_BATCH, PAST_LEN, FOX_HEADS)))
    d['norm_ffn1'] = gain((DEPTH, D_MODEL))
    d['w1_gate'] = nrm((DEPTH, D_MODEL, D_FF), D_MODEL ** -0.5)
    d['w1_up'] = nrm((DEPTH, D_MODEL, D_FF), D_MODEL ** -0.5)
    d['w1_down'] = nrm((DEPTH, D_FF, D_MODEL), D_FF ** -0.5)
    d['norm_mix'] = gain((DEPTH, D_MODEL))
    d['w_in'] = nrm((DEPTH, D_MODEL, IN_COLS), D_MODEL ** -0.5)
    d['ssd_conv_w'] = nrm((DEPTH, SSD_CONV, SSD_CONV_CH), SSD_CONV ** -0.5)
    d['ssd_conv_b'] = nrm((DEPTH, SSD_CONV_CH), 0.05)
    d['ssd_dt_bias'] = dt0 + jnp.log(-jnp.expm1(-dt0))
    d['ssd_a_log'] = jnp.log(jax.random.uniform(next(ks), (DEPTH, SSD_HEADS), f32, 1.0, 16.0))
    d['ssd_d'] = gain((DEPTH, SSD_HEADS))
    d['ssd_norm'] = gain((DEPTH, SSD_WIDTH))
    d['gla_w_gate'] = nrm((DEPTH, GLA_RANK, GLA_HEADS * GLA_DK), GLA_RANK ** -0.5)
    d['gla_b_gate'] = nrm((DEPTH, GLA_HEADS * GLA_DK), 0.05)
    d['gla_norm'] = gain((DEPTH, GLA_DV))
    d['fox_q_norm'] = gain((DEPTH, FOX_HEAD_DIM))
    d['fox_k_norm'] = gain((DEPTH, FOX_HEAD_DIM))
    d['fox_f_bias'] = 2.0 + nrm((DEPTH, FOX_HEADS), 0.5)
    d['w_out'] = nrm((DEPTH, D_MIX, D_MODEL), D_MIX ** -0.5)
    d['norm_ffn2'] = gain((DEPTH, D_MODEL))
    d['w2_gate'] = nrm((DEPTH, D_MODEL, D_FF), D_MODEL ** -0.5)
    d['w2_up'] = nrm((DEPTH, D_MODEL, D_FF), D_MODEL ** -0.5)
    d['w2_down'] = nrm((DEPTH, D_FF, D_MODEL), D_FF ** -0.5)
    return d


def reference(x_prompt, x_sample, state_ssd_conv, state_ssd, state_gla, cache_fox_k, cache_fox_v,
              cache_fox_logf, norm_ffn1, w1_gate, w1_up, w1_down, norm_mix, w_in, ssd_conv_w,
              ssd_conv_b, ssd_dt_bias, ssd_a_log, ssd_d, ssd_norm, gla_w_gate, gla_b_gate, gla_norm,
              fox_q_norm, fox_k_norm, fox_f_bias, w_out, norm_ffn2, w2_gate, w2_up, w2_down):
    xp, xs = x_prompt, x_sample
    bp = xp.shape[0]
    p_new = [[] for _ in range(6)]
    s_new = [[] for _ in range(6)]
    for l in range(DEPTH):
        prm = dict(norm_ffn1=norm_ffn1[l], w1_gate=w1_gate[l], w1_up=w1_up[l], w1_down=w1_down[l],
                   norm_mix=norm_mix[l], w_in=w_in[l], ssd_conv_w=ssd_conv_w[l], ssd_conv_b=ssd_conv_b[l],
                   ssd_dt_bias=ssd_dt_bias[l], ssd_a_log=ssd_a_log[l], ssd_d=ssd_d[l], ssd_norm=ssd_norm[l],
                   gla_w_gate=gla_w_gate[l], gla_b_gate=gla_b_gate[l], gla_norm=gla_norm[l],
                   fox_q_norm=fox_q_norm[l], fox_k_norm=fox_k_norm[l], fox_f_bias=fox_f_bias[l],
                   w_out=w_out[l], norm_ffn2=norm_ffn2[l], w2_gate=w2_gate[l], w2_up=w2_up[l],
                   w2_down=w2_down[l])
        prompt_past = (jnp.zeros((bp, SSD_CONV - 1, SSD_CONV_CH), xp.dtype),
                       jnp.zeros((bp, SSD_HEADS, SSD_HEAD_DIM, SSD_STATE), jnp.float32),
                       jnp.zeros((bp, GLA_HEADS, GLA_DK, GLA_DV), jnp.float32),
                       None, None, None)
        sample_past = (state_ssd_conv[l], state_ssd[l], state_gla[l],
                       cache_fox_k[l], cache_fox_v[l], cache_fox_logf[l])
        xp, st_p = trunk_layer(xp, prm, prompt_past)
        xs, st_s = trunk_layer(xs, prm, sample_past)
        for i in range(6):
            p_new[i].append(st_p[i])
            s_new[i].append(st_s[i])
    p_ssd_conv, p_ssd, p_gla, p_fox_k, p_fox_v, p_fox_logf = [jnp.stack(a) for a in p_new]
    s_ssd_conv, s_ssd, s_gla, s_fox_k, s_fox_v, s_fox_logf = [jnp.stack(a) for a in s_new]
    return (xp, xs, p_ssd_conv, p_ssd, p_gla, p_fox_k, p_fox_v, p_fox_logf,
            s_ssd_conv, s_ssd, s_gla, s_fox_k, s_fox_v, s_fox_logf)
```

```python
import functools

import numpy as np
import jax
import jax.numpy as jnp
from jax import lax
from jax.experimental import pallas as pl
from jax.experimental.pallas import tpu as pltpu

F32 = jnp.float32
BF16 = jnp.bfloat16

EPS = 1e-6
D_MODEL = 1024
D_FF = 2816
SSD_HEADS = 8
SSD_HEAD_DIM = 64
SSD_WIDTH = 512
SSD_GROUPS = 2
SSD_STATE = 128
SSD_CONV = 4
SSD_CONV_CH = 1024
GLA_HEADS = 4
GLA_DK = 32
GLA_DV = 64
GLA_WIDTH = 256
GLA_RANK = 16
GLA_TAU = 16.0
GLA_CHUNK = 64
FOX_HEADS = 4
FOX_HEAD_DIM = 64
FOX_WIDTH = 256
LANES = 128
SUBLANES = 8
MIB = 1024 * 1024

_OFF_Z, _OFF_XBC, _OFF_DT = 0, 512, 1536
_OFF_GQ, _OFF_GR = 1544, 2312
_OFF_FQ, _OFF_FF = 2328, 3096
SSD_COLS = SSD_WIDTH + SSD_CONV_CH + LANES
GLA_COLS = 2 * GLA_HEADS * GLA_DK + 2 * GLA_WIDTH + LANES
FOX_COLS = 3 * FOX_WIDTH + LANES
AUX_PARTS = 3


def _dot(a, b):
    return jnp.dot(a, b, preferred_element_type=F32)


def _dot_nt(a, b):
    return lax.dot_general(a, b, (((1,), (1,)), ((), ())), preferred_element_type=F32)


def _dot_tn(a, b):
    return lax.dot_general(a, b, (((0,), (0,)), ((), ())), preferred_element_type=F32)


def _split_bf16(x, parts):
    out = []
    r = x
    for i in range(parts):
        p = r.astype(BF16)
        out.append(p)
        if i + 1 < parts:
            r = r - p.astype(F32)
    return out


def _dot_sel_lhs(sel, x, parts=3):
    acc = None
    for p in _split_bf16(x, parts):
        t = _dot(sel, p)
        acc = t if acc is None else acc + t
    return acc


def _dot_sel_rhs(x, sel, parts=3):
    acc = None
    for p in _split_bf16(x, parts):
        t = _dot(p, sel)
        acc = t if acc is None else acc + t
    return acc


def _rms(x, w):
    ms = jnp.mean(x * x, axis=-1, keepdims=True)
    return x * lax.rsqrt(ms + EPS) * w


def _silu(x):
    return x * jax.nn.sigmoid(x)


def _softplus(x):
    return jnp.maximum(x, 0.0) + jnp.log1p(jnp.exp(-jnp.abs(x)))


def _log_sigmoid(x):
    return -_softplus(-x)


def _tri_mask(n, m=None):
    m = n if m is None else m
    r = lax.broadcasted_iota(jnp.int32, (n, m), 0)
    c = lax.broadcasted_iota(jnp.int32, (n, m), 1)
    return r, c


def _lane_pair_select(a_even, a_odd):
    lane = lax.broadcasted_iota(jnp.int32, a_even.shape, 1)
    return jnp.where(lane < 64, a_even, a_odd)


def _expand_heads(v, heads, rows):
    pieces = []
    for i in range(0, len(heads), 2):
        a = jnp.broadcast_to(v[:, heads[i]:heads[i] + 1], (rows, LANES))
        b = jnp.broadcast_to(v[:, heads[i + 1]:heads[i + 1] + 1], (rows, LANES))
        pieces.append(_lane_pair_select(a, b))
    return pieces[0] if len(pieces) == 1 else jnp.concatenate(pieces, axis=1)


def _group_mean_matrix(width, group):
    r, c = _tri_mask(width)
    return jnp.where((r // group) == (c // group), 1.0 / group, 0.0).astype(BF16)


def _params(sem, vmem_mib):
    return pltpu.CompilerParams(dimension_semantics=sem, vmem_limit_bytes=vmem_mib * MIB)


def _const_spec(shape, single=False):
    nd = len(shape)
    if single:
        return pl.BlockSpec(shape, lambda *_: (0,) * nd, pipeline_mode=pl.Buffered(1))
    return pl.BlockSpec(shape, lambda *_: (0,) * nd)


def _swiglu_half(x, g_ref, wg_ref, wu_ref, wd_ref):
    h = _rms(x, g_ref[...]).astype(BF16)
    gate = _dot(h, wg_ref[...])
    up = _dot(h, wu_ref[...])
    a = (_silu(gate) * up).astype(BF16)
    return x + 0.5 * _dot(a, wd_ref[...])


def _ffn_body(x_ref, g_ref, wg_ref, wu_ref, wd_ref, o_ref):
    o_ref[...] = _swiglu_half(x_ref[...], g_ref, wg_ref, wu_ref, wd_ref)


def _out_ffn_body(x_ref, ys_ref, og_ref, of_ref, wo_ref, g_ref, wg_ref, wu_ref, wd_ref, o_ref):
    x = x_ref[...]
    x = x + (_dot(ys_ref[...], wo_ref[0:SSD_WIDTH, :])
             + _dot(og_ref[...], wo_ref[SSD_WIDTH:SSD_WIDTH + GLA_WIDTH, :])
             + _dot(of_ref[...], wo_ref[SSD_WIDTH + GLA_WIDTH:, :]))
    o_ref[...] = _swiglu_half(x, g_ref, wg_ref, wu_ref, wd_ref)


def _ffn_weight_specs():
    return [_const_spec((1, D_MODEL)),
            _const_spec((D_MODEL, D_FF), single=True),
            _const_spec((D_MODEL, D_FF), single=True),
            _const_spec((D_FF, D_MODEL), single=True)]


def _row_tile(n, want):
    t = min(want, n)
    while n % t:
        t //= 2
    return t


def _ffn_call(x, g, wg, wu, wd, tm):
    n = x.shape[0]
    tm = _row_tile(n, tm)
    row = pl.BlockSpec((tm, D_MODEL), lambda i: (i, 0))
    return pl.pallas_call(
        _ffn_body, grid=(n // tm,),
        in_specs=[row] + _ffn_weight_specs(),
        out_specs=row,
        out_shape=jax.ShapeDtypeStruct((n, D_MODEL), F32),
        compiler_params=_params(("parallel",), 56),
        name="ffn",
    )(x, g, wg, wu, wd)


def _out_ffn_call(x, ys, og, of, wo, g, wg, wu, wd, tm):
    n = x.shape[0]
    tm = _row_tile(n, tm)
    row = lambda w: pl.BlockSpec((tm, w), lambda i: (i, 0))
    return pl.pallas_call(
        _out_ffn_body, grid=(n // tm,),
        in_specs=[row(D_MODEL), row(SSD_WIDTH), row(GLA_WIDTH), row(FOX_WIDTH),
                  _const_spec((D_MODEL, D_MODEL), single=True)] + _ffn_weight_specs(),
        out_specs=row(D_MODEL),
        out_shape=jax.ShapeDtypeStruct((n, D_MODEL), F32),
        compiler_params=_params(("parallel",), 56),
        name="out_ffn",
    )(x, ys, og, of, wo, g, wg, wu, wd)


def _ssd_body(x_ref, nm_ref, w_ref, cw_ref, cb_ref, dtb_ref, alog_ref, dexp_ref, nw_ref,
              cprev_ref, h0_ref, y_ref, cnew_ref, hnew_ref, xbuf, hst, *, tt):
    t = pl.program_id(1)

    @pl.when(t == 0)
    def _():
        xbuf[0:SUBLANES, :] = cprev_ref[...]
        hst[...] = h0_ref[...]

    h = _rms(x_ref[...], nm_ref[...]).astype(BF16)
    u = _dot(h, w_ref[...])
    z = u[:, 0:SSD_WIDTH]
    xbc = u[:, SSD_WIDTH:SSD_WIDTH + SSD_CONV_CH]
    dt_raw = u[:, SSD_WIDTH + SSD_CONV_CH:]

    xbuf[SUBLANES:SUBLANES + tt, :] = xbc
    conv = xbuf[SUBLANES - 3:SUBLANES - 3 + tt, :] * cw_ref[0:1, :]
    conv = conv + xbuf[SUBLANES - 2:SUBLANES - 2 + tt, :] * cw_ref[1:2, :]
    conv = conv + xbuf[SUBLANES - 1:SUBLANES - 1 + tt, :] * cw_ref[2:3, :]
    conv = conv + xbc * cw_ref[3:4, :]
    xa = _silu(conv + cb_ref[...])
    xbuf[0:SUBLANES, :] = xbc[tt - SUBLANES:, :]

    xs = xa[:, 0:SSD_WIDTH]
    bm = xa[:, SSD_WIDTH:SSD_WIDTH + SSD_GROUPS * SSD_STATE]
    cm = xa[:, SSD_WIDTH + SSD_GROUPS * SSD_STATE:]
    dt = _softplus(dt_raw + dtb_ref[...])
    a = dt * (-jnp.exp(alog_ref[...]))

    r, c = _tri_mask(tt)
    causal = r >= c
    acum = _dot_sel_lhs(causal.astype(BF16), a)
    acum_t = acum.T
    alast = acum[tt - 1:tt, :]
    ea = jnp.exp(acum)
    wend = jnp.exp(alast - acum) * dt

    heads_per_group = SSD_HEADS // SSD_GROUPS
    gw = heads_per_group * SSD_HEAD_DIM
    y_parts = []
    for g in range(SSD_GROUPS):
        heads = list(range(g * heads_per_group, (g + 1) * heads_per_group))
        bg = bm[:, g * SSD_STATE:(g + 1) * SSD_STATE].astype(BF16)
        cg = cm[:, g * SSD_STATE:(g + 1) * SSD_STATE].astype(BF16)
        cb = _dot_nt(cg, bg)
        xg = xs[:, g * gw:(g + 1) * gw]
        xdt = (xg * _expand_heads(dt, heads, tt)).astype(BF16)
        intra = []
        for pi in range(heads_per_group // 2):
            ys = []
            for e in range(2):
                hh = heads[2 * pi + e]
                seg = acum[:, hh:hh + 1] - acum_t[hh:hh + 1, :]
                m = (cb * jnp.where(causal, jnp.exp(seg), 0.0)).astype(BF16)
                ys.append(_dot(m, xdt[:, pi * LANES:(pi + 1) * LANES]))
            intra.append(_lane_pair_select(ys[0], ys[1]))
        y_intra = jnp.concatenate(intra, axis=1)

        hg = hst[g * gw:(g + 1) * gw, :]
        y_inter = _dot_nt(cg, hg.astype(BF16)) * _expand_heads(ea, heads, tt)
        y_parts.append(y_intra + y_inter)

        xw = (xg * _expand_heads(wend, heads, tt)).astype(BF16)
        upd = _dot_tn(xw, bg)
        cd = jnp.concatenate(
            [jnp.broadcast_to(jnp.exp(alast[:, hh:hh + 1]), (SSD_HEAD_DIM, SSD_STATE)) for hh in heads],
            axis=0)
        hst[g * gw:(g + 1) * gw, :] = hg * cd + upd

    y = jnp.concatenate(y_parts, axis=1) + dexp_ref[...] * xs
    y = y * _silu(z)
    nw = nw_ref[...]
    outs = []
    for g in range(SSD_GROUPS):
        outs.append(_rms(y[:, g * gw:(g + 1) * gw], nw[:, g * gw:(g + 1) * gw]))
    y_ref[...] = jnp.concatenate(outs, axis=1).astype(y_ref.dtype)

    @pl.when(t == pl.num_programs(1) - 1)
    def _():
        cnew_ref[...] = xbuf[0:SUBLANES, :]
        hnew_ref[...] = hst[...]


def _ssd_call(x, nm, w, cw, cb, dtb, alog, dexp, nw, cprev, h0, tt):
    bsz, t, _ = x.shape
    tt = _row_tile(t, tt)
    bt = lambda w_: pl.BlockSpec((None, tt, w_), lambda b, i: (b, i, 0))
    per_b = lambda r_, w_: pl.BlockSpec((None, r_, w_), lambda b, i: (b, 0, 0))
    return pl.pallas_call(
        functools.partial(_ssd_body, tt=tt), grid=(bsz, t // tt),
        in_specs=[bt(D_MODEL), _const_spec((1, D_MODEL)), _const_spec((D_MODEL, SSD_COLS), single=True),
                  _const_spec((SSD_CONV, SSD_CONV_CH)), _const_spec((1, SSD_CONV_CH)),
                  _const_spec((1, LANES)), _const_spec((1, LANES)),
                  _const_spec((1, SSD_WIDTH)), _const_spec((1, SSD_WIDTH)),
                  per_b(SUBLANES, SSD_CONV_CH), per_b(SSD_WIDTH, SSD_STATE)],
        out_specs=[bt(SSD_WIDTH), per_b(SUBLANES, SSD_CONV_CH), per_b(SSD_WIDTH, SSD_STATE)],
        out_shape=[jax.ShapeDtypeStruct((bsz, t, SSD_WIDTH), BF16),
                   jax.ShapeDtypeStruct((bsz, SUBLANES, SSD_CONV_CH), F32),
                   jax.ShapeDtypeStruct((bsz, SSD_WIDTH, SSD_STATE), F32)],
        scratch_shapes=[pltpu.VMEM((tt + SUBLANES, SSD_CONV_CH), F32),
                        pltpu.VMEM((SSD_WIDTH, SSD_STATE), F32)],
        compiler_params=_params(("parallel", "arbitrary"), 40),
        name="ssd",
    )(x, nm, w, cw, cb, dtb, alog, dexp, nw, cprev, h0)


def _gla_body(x_ref, nm_ref, w_ref, wgate_ref, bgate_ref, gn_ref, s0_ref, o_ref, snew_ref, st, *, tt):
    t = pl.program_id(1)

    @pl.when(t == 0)
    def _():
        st[...] = s0_ref[...]

    hk = GLA_HEADS * GLA_DK
    h = _rms(x_ref[...], nm_ref[...]).astype(BF16)
    u = _dot(h, w_ref[...])
    q = u[:, 0:hk] * (GLA_DK ** -0.5)
    k = u[:, hk:2 * hk]
    v = u[:, 2 * hk:2 * hk + GLA_WIDTH].astype(BF16)
    gg = u[:, 2 * hk + GLA_WIDTH:2 * hk + 2 * GLA_WIDTH]
    gr = u[:, 2 * hk + 2 * GLA_WIDTH:].astype(BF16)
    la = _log_sigmoid(_dot(gr, wgate_ref[...]) + bgate_ref[...]) / GLA_TAU

    r, c = _tri_mask(tt)
    same = (r // GLA_CHUNK) == (c // GLA_CHUNK)
    bcum = _dot_sel_lhs((same & (r >= c)).astype(BF16), la)
    blast = _dot_sel_lhs(same.astype(BF16), la)
    qd = (q * jnp.exp(bcum)).astype(BF16)
    kd = (k * jnp.exp(-bcum)).astype(BF16)
    kend = (k * jnp.exp(blast - bcum)).astype(BF16)
    eblast = jnp.exp(blast)

    lane_k = lax.broadcasted_iota(jnp.int32, (GLA_CHUNK, hk), 1) // GLA_DK
    lane_v = lax.broadcasted_iota(jnp.int32, (GLA_CHUNK, GLA_WIDTH), 1) // GLA_DV
    ar, ac = _tri_mask(GLA_HEADS * GLA_CHUNK, GLA_CHUNK)
    att_causal = (ar % GLA_CHUNK) >= ac
    sr, sc = _tri_mask(GLA_WIDTH, hk)
    diag = (sr // GLA_DV) == (sc // GLA_DK)

    outs = []
    for ci in range(tt // GLA_CHUNK):
        sl = slice(ci * GLA_CHUNK, (ci + 1) * GLA_CHUNK)
        qd_c, kd_c, kend_c, v_c = qd[sl], kd[sl], kend[sl], v[sl]
        lhs = jnp.concatenate([jnp.where(lane_k == hh, qd_c, jnp.zeros_like(qd_c))
                               for hh in range(GLA_HEADS)], axis=0)
        att = jnp.where(att_causal, _dot_nt(lhs, kd_c), 0.0).astype(BF16)
        res = _dot(att, v_c)
        o_intra = jnp.zeros((GLA_CHUNK, GLA_WIDTH), F32)
        for hh in range(GLA_HEADS):
            o_intra = jnp.where(lane_v == hh, res[hh * GLA_CHUNK:(hh + 1) * GLA_CHUNK], o_intra)
        s_prev = st[...]
        o_inter = _dot_nt(qd_c, s_prev.astype(BF16))
        outs.append(o_intra + o_inter)
        upd = _dot_tn(v_c, kend_c)
        st[...] = s_prev * eblast[ci * GLA_CHUNK:ci * GLA_CHUNK + 1, :] + jnp.where(diag, upd, 0.0)

    o = jnp.concatenate(outs, axis=0) if len(outs) > 1 else outs[0]
    ms = _dot_sel_rhs(o * o, _group_mean_matrix(GLA_WIDTH, GLA_DV), parts=2)
    o = o * lax.rsqrt(ms + EPS) * gn_ref[...]
    o_ref[...] = (o * _silu(gg)).astype(o_ref.dtype)

    @pl.when(t == pl.num_programs(1) - 1)
    def _():
        snew_ref[...] = st[...]


def _gla_call(x, nm, w, wgate, bgate, gn, s0, tt):
    bsz, t, _ = x.shape
    tt = _row_tile(t, tt)
    hk = GLA_HEADS * GLA_DK
    bt = lambda w_: pl.BlockSpec((None, tt, w_), lambda b, i: (b, i, 0))
    per_b = pl.BlockSpec((None, GLA_WIDTH, hk), lambda b, i: (b, 0, 0))
    return pl.pallas_call(
        functools.partial(_gla_body, tt=tt), grid=(bsz, t // tt),
        in_specs=[bt(D_MODEL), _const_spec((1, D_MODEL)), _const_spec((D_MODEL, GLA_COLS), single=True),
                  _const_spec((LANES, hk)), _const_spec((1, hk)), _const_spec((1, GLA_WIDTH)), per_b],
        out_specs=[bt(GLA_WIDTH), per_b],
        out_shape=[jax.ShapeDtypeStruct((bsz, t, GLA_WIDTH), BF16),
                   jax.ShapeDtypeStruct((bsz, GLA_WIDTH, hk), F32)],
        scratch_shapes=[pltpu.VMEM((GLA_WIDTH, hk), F32)],
        compiler_params=_params(("parallel", "arbitrary"), 32),
        name="gla",
    )(x, nm, w, wgate, bgate, gn, s0)


def _fox_proj_body(x_ref, nm_ref, w_ref, qn_ref, kn_ref, fb_ref, pq_ref, pk_ref, arow_ref,
                   k_ref, v_ref, lf_ref, *rest, tt, attn_layout):
    if attn_layout:
        qh_ref, kp_ref, vp_ref, carry = rest
    else:
        qs_ref, carry = rest
    t = pl.program_id(1)

    @pl.when(t == 0)
    def _():
        carry[...] = jnp.zeros_like(carry)

    h = _rms(x_ref[...], nm_ref[...]).astype(BF16)
    u = _dot(h, w_ref[...])
    fq = u[:, 0:FOX_WIDTH]
    fk = u[:, FOX_WIDTH:2 * FOX_WIDTH]
    fv = u[:, 2 * FOX_WIDTH:3 * FOX_WIDTH]
    ff = u[:, 3 * FOX_WIDTH:]

    gmat = _group_mean_matrix(FOX_WIDTH, FOX_HEAD_DIM)
    qn = fq * lax.rsqrt(_dot_sel_rhs(fq * fq, gmat, parts=2) + EPS) * qn_ref[...]
    kn = fk * lax.rsqrt(_dot_sel_rhs(fk * fk, gmat, parts=2) + EPS) * kn_ref[...]
    lf = _log_sigmoid(ff + fb_ref[...])
    k_ref[...] = kn
    v_ref[...] = fv
    lf_ref[...] = lf
    qs = (qn * (FOX_HEAD_DIM ** -0.5)).astype(BF16)

    if not attn_layout:
        qs_ref[...] = qs
        return

    r, c = _tri_mask(tt)
    cum = carry[...] + _dot_sel_lhs((r >= c).astype(BF16), lf)
    carry[...] = cum[tt - 1:tt, :]
    c3 = jnp.concatenate(_split_bf16(cum, AUX_PARTS), axis=1)
    aux_k = (_dot(c3, pk_ref[...]) + arow_ref[0:1, :]).astype(BF16)
    knb = kn.astype(BF16)
    fvb = fv.astype(BF16)
    lane = lax.broadcasted_iota(jnp.int32, (tt, LANES), 1)
    for p in range(FOX_HEADS // 2):
        sl = slice(p * LANES, (p + 1) * LANES)
        kp_ref[p, :, 0:LANES] = knb[:, sl]
        kp_ref[p, :, LANES:] = aux_k
        vp_ref[p] = fvb[:, sl]
        for e in range(2):
            hh = 2 * p + e
            qmask = jnp.where((lane // FOX_HEAD_DIM) == e, qs[:, sl], jnp.zeros_like(qs[:, sl]))
            aux_q = (_dot(c3, pq_ref[hh]) + arow_ref[1 + hh:2 + hh, :]).astype(BF16)
            qh_ref[hh, :, 0:LANES] = qmask
            qh_ref[hh, :, LANES:] = aux_q


def _fox_aux_constants():
    nslot = FOX_HEADS * AUX_PARTS
    pq = np.zeros((FOX_HEADS, AUX_PARTS * LANES, LANES), np.float32)
    pk = np.zeros((AUX_PARTS * LANES, LANES), np.float32)
    arow = np.zeros((SUBLANES, LANES), np.float32)
    for hh in range(FOX_HEADS):
        for j in range(AUX_PARTS):
            pq[hh, j * LANES + hh, AUX_PARTS * hh + j] = 1.0
            pk[j * LANES + hh, nslot + AUX_PARTS * hh + j] = -1.0
            arow[0, AUX_PARTS * hh + j] = 1.0
            arow[1 + hh, nslot + AUX_PARTS * hh + j] = 1.0
    return jnp.asarray(pq, BF16), jnp.asarray(pk, BF16), jnp.asarray(arow, F32)


def _fox_proj_call(x, nm, w, qn, kn, fb, tt, attn_layout):
    bsz, t, _ = x.shape
    tt = _row_tile(t, tt)
    pq, pk, arow = _fox_aux_constants()
    bt = lambda w_: pl.BlockSpec((None, tt, w_), lambda b, i: (b, i, 0))
    bht = lambda n_, w_: pl.BlockSpec((None, n_, tt, w_), lambda b, i: (b, 0, i, 0))
    out_specs = [bt(FOX_WIDTH), bt(FOX_WIDTH), bt(LANES)]
    out_shape = [jax.ShapeDtypeStruct((bsz, t, FOX_WIDTH), F32),
                 jax.ShapeDtypeStruct((bsz, t, FOX_WIDTH), F32),
                 jax.ShapeDtypeStruct((bsz, t, LANES), F32)]
    if attn_layout:
        out_specs += [bht(FOX_HEADS, 2 * LANES), bht(FOX_HEADS // 2, 2 * LANES), bht(FOX_HEADS // 2, LANES)]
        out_shape += [jax.ShapeDtypeStruct((bsz, FOX_HEADS, t, 2 * LANES), BF16),
                      jax.ShapeDtypeStruct((bsz, FOX_HEADS // 2, t, 2 * LANES), BF16),
                      jax.ShapeDtypeStruct((bsz, FOX_HEADS // 2, t, LANES), BF16)]
    else:
        out_specs += [bt(FOX_WIDTH)]
        out_shape += [jax.ShapeDtypeStruct((bsz, t, FOX_WIDTH), BF16)]
    return pl.pallas_call(
        functools.partial(_fox_proj_body, tt=tt, attn_layout=attn_layout), grid=(bsz, t // tt),
        in_specs=[bt(D_MODEL), _const_spec((1, D_MODEL)), _const_spec((D_MODEL, FOX_COLS), single=True),
                  _const_spec((1, FOX_WIDTH)), _const_spec((1, FOX_WIDTH)), _const_spec((1, LANES)),
                  _const_spec(pq.shape), _const_spec(pk.shape), _const_spec(arow.shape)],
        out_specs=out_specs, out_shape=out_shape,
        scratch_shapes=[pltpu.VMEM((1, LANES), F32)],
        compiler_params=_params(("parallel", "arbitrary"), 32),
        name="fox_proj",
    )(x, nm, w, qn, kn, fb, pq, pk, arow)


def _fox_prompt_body(q_ref, k_ref, v_ref, o_ref, m_s, l_s, acc_s, *, tq):
    qi = pl.program_id(2)
    m_s[...] = jnp.full_like(m_s, -jnp.inf)
    l_s[...] = jnp.zeros_like(l_s)
    acc_s[...] = jnp.zeros_like(acc_s)

    def tile(kj, masked):
        start = pl.multiple_of(kj * tq, tq)
        kt = k_ref[pl.ds(start, tq), :]
        vt = v_ref[pl.ds(start, tq), :]
        for e in range(2):
            s = _dot_nt(q_ref[e], kt)
            if masked:
                r, c = _tri_mask(tq)
                s = jnp.where(r >= c, s, -jnp.inf)
            m_old = m_s[e]
            m_new = jnp.maximum(m_old, jnp.max(s, axis=-1, keepdims=True))
            alpha = jnp.exp(m_old - m_new)
            p = jnp.exp(s - m_new)
            l_s[e] = alpha * l_s[e] + jnp.sum(p, axis=-1, keepdims=True)
            acc_s[e] = alpha * acc_s[e] + _dot(p.astype(BF16), vt)
            m_s[e] = m_new

    def full_tile(kj, carry):
        tile(kj, False)
        return carry

    lax.fori_loop(0, qi, full_tile, 0)
    tile(qi, True)
    o_ref[...] = _lane_pair_select(acc_s[0] / l_s[0], acc_s[1] / l_s[1]).astype(o_ref.dtype)


def _fox_prompt_call(qh, kp, vp, tq):
    bsz, _, t, _ = qh.shape
    tq = _row_tile(t, tq)
    npair = FOX_HEADS // 2
    return pl.pallas_call(
        functools.partial(_fox_prompt_body, tq=tq), grid=(bsz, npair, t // tq),
        in_specs=[pl.BlockSpec((None, 2, tq, 2 * LANES), lambda b, p, i: (b, p, i, 0)),
                  pl.BlockSpec((None, None, t, 2 * LANES), lambda b, p, i: (b, p, 0, 0)),
                  pl.BlockSpec((None, None, t, LANES), lambda b, p, i: (b, p, 0, 0))],
        out_specs=pl.BlockSpec((None, tq, LANES), lambda b, p, i: (b, i, p)),
        out_shape=jax.ShapeDtypeStruct((bsz, t, FOX_WIDTH), BF16),
        scratch_shapes=[pltpu.VMEM((2, tq, 1), F32), pltpu.VMEM((2, tq, 1), F32),
                        pltpu.VMEM((2, tq, LANES), F32)],
        compiler_params=_params(("parallel", "parallel", "arbitrary"), 40),
        name="fox_attn_prompt",
    )(qh, kp, vp)


def _fox_sample_body(q_ref, kn_ref, vn_ref, lfn_ref, kc_ref, vc_ref, lfp_ref, o_ref, *, tn, past, seg):
    r, c = _tri_mask(seg)
    upper = (r <= c).astype(BF16)
    carry = jnp.zeros((SUBLANES, 1), F32)
    cps = []
    for j in range(past // seg):
        cs = carry + _dot_sel_rhs(lfp_ref[:, j * seg:(j + 1) * seg], upper)
        cps.append(cs)
        carry = cs[:, seg - 1:seg]
    cp = jnp.concatenate(cps, axis=1) if len(cps) > 1 else cps[0]
    lfn = lfn_ref[...]
    if tn < LANES:
        lfn = jnp.concatenate([lfn, jnp.zeros((LANES - tn, LANES), F32)], axis=0)
    r, c = _tri_mask(LANES)
    cn = _dot_sel_lhs((r >= c).astype(BF16), lfn)
    cn_t = cn.T

    q = q_ref[...]
    kc = kc_ref[...].astype(BF16)
    vc = vc_ref[...].astype(BF16)
    kn = kn_ref[...].astype(BF16)
    vn = vn_ref[...].astype(BF16)
    lane = lax.broadcasted_iota(jnp.int32, (tn, FOX_WIDTH), 1) // FOX_HEAD_DIM
    r, c = _tri_mask(tn)
    out = jnp.zeros((tn, FOX_WIDTH), F32)
    for hh in range(FOX_HEADS):
        qh = jnp.where(lane == hh, q, jnp.zeros_like(q))
        cq = cn[0:tn, hh:hh + 1]
        s_past = _dot_nt(qh, kc) + ((carry[hh:hh + 1, :] + cq) - cp[hh:hh + 1, :])
        s_new = _dot_nt(qh, kn) + (cq - cn_t[hh:hh + 1, 0:tn])
        s_new = jnp.where(r >= c, s_new, -jnp.inf)
        m = jnp.maximum(jnp.max(s_past, axis=-1, keepdims=True), jnp.max(s_new, axis=-1, keepdims=True))
        p_past = jnp.exp(s_past - m)
        p_new = jnp.exp(s_new - m)
        denom = jnp.sum(p_past, axis=-1, keepdims=True) + jnp.sum(p_new, axis=-1, keepdims=True)
        o = (_dot(p_past.astype(BF16), vc) + _dot(p_new.astype(BF16), vn)) / denom
        out = jnp.where(lane == hh, o, out)
    o_ref[...] = out.astype(o_ref.dtype)


def _fox_sample_call(q, kn, vn, lfn, cache_k, cache_v, lfp_t, layer):
    bsz, tn, _ = q.shape
    past = cache_k.shape[2]
    seg = _row_tile(past, 512)
    bt = lambda w_: pl.BlockSpec((None, tn, w_), lambda b: (b, 0, 0))
    cache = pl.BlockSpec((None, None, past, FOX_WIDTH), lambda b: (layer, b, 0, 0))
    return pl.pallas_call(
        functools.partial(_fox_sample_body, tn=tn, past=past, seg=seg), grid=(bsz,),
        in_specs=[bt(FOX_WIDTH), bt(FOX_WIDTH), bt(FOX_WIDTH), bt(LANES), cache, cache,
                  pl.BlockSpec((None, None, SUBLANES, past), lambda b: (layer, b, 0, 0))],
        out_specs=bt(FOX_WIDTH),
        out_shape=jax.ShapeDtypeStruct((bsz, tn, FOX_WIDTH), BF16),
        compiler_params=_params(("parallel",), 48),
        name="fox_attn_sample",
    )(q, kn, vn, lfn, cache_k, cache_v, lfp_t)


def _pad_cols(a, width):
    return jnp.pad(a, ((0, 0), (0, width - a.shape[1])))


def _layer_params(l, norm_ffn1, w1_gate, w1_up, w1_down, norm_mix, w_in, ssd_conv_w, ssd_conv_b,
                  ssd_dt_bias, ssd_a_log, ssd_d, ssd_norm, gla_w_gate, gla_b_gate, gla_norm,
                  fox_q_norm, fox_k_norm, fox_f_bias, w_out, norm_ffn2, w2_gate, w2_up, w2_down):
    wi = w_in[l]
    row = lambda a: a.reshape(1, -1).astype(F32)
    hk = GLA_HEADS * GLA_DK
    p = dict(
        ffn1=(row(norm_ffn1[l]), w1_gate[l].astype(BF16), w1_up[l].astype(BF16), w1_down[l].astype(BF16)),
        ffn2=(row(norm_ffn2[l]), w2_gate[l].astype(BF16), w2_up[l].astype(BF16), w2_down[l].astype(BF16)),
        norm_mix=row(norm_mix[l]),
        w_out=w_out[l].astype(BF16),
        w_ssd=jnp.concatenate([wi[:, _OFF_Z:_OFF_DT], _pad_cols(wi[:, _OFF_DT:_OFF_GQ], LANES)], axis=1).astype(BF16),
        conv_w=ssd_conv_w[l].astype(F32),
        conv_b=row(ssd_conv_b[l]),
        dt_bias=_pad_cols(row(ssd_dt_bias[l]), LANES),
        a_log=_pad_cols(row(ssd_a_log[l]), LANES),
        d_exp=row(jnp.repeat(ssd_d[l], SSD_HEAD_DIM)),
        ssd_norm=row(ssd_norm[l]),
        w_gla=jnp.concatenate([wi[:, _OFF_GQ:_OFF_GR], _pad_cols(wi[:, _OFF_GR:_OFF_FQ], LANES)], axis=1).astype(BF16),
        w_gate=jnp.pad(gla_w_gate[l], ((0, LANES - GLA_RANK), (0, 0))).astype(BF16),
        b_gate=row(gla_b_gate[l]),
        gla_norm=row(jnp.tile(gla_norm[l], GLA_HEADS)),
        w_fox=jnp.concatenate([wi[:, _OFF_FQ:_OFF_FF], _pad_cols(wi[:, _OFF_FF:], LANES)], axis=1).astype(BF16),
        q_norm=row(jnp.tile(fox_q_norm[l], FOX_HEADS)),
        k_norm=row(jnp.tile(fox_k_norm[l], FOX_HEADS)),
        f_bias=_pad_cols(row(fox_f_bias[l]), LANES),
    )
    return p


def _gla_state_in(s):
    bsz = s.shape[0]
    eye = jnp.eye(GLA_HEADS, dtype=s.dtype)
    full = jnp.einsum('bhkv,hg->bhvgk', s, eye)
    return full.reshape(bsz, GLA_HEADS * GLA_DV, GLA_HEADS * GLA_DK)


def _gla_state_out(st):
    bsz = st.shape[0]
    full = st.reshape(bsz, GLA_HEADS, GLA_DV, GLA_HEADS, GLA_DK)
    idx = jnp.arange(GLA_HEADS)
    diag = full[:, idx, :, idx, :]
    return jnp.transpose(diag, (1, 0, 3, 2))


def _mix(x, p, conv_prev, ssd_h0, gla_s0, fox_cache, layer, tiles):
    bsz, t, _ = x.shape
    cprev = jnp.pad(conv_prev, ((0, 0), (SUBLANES - (SSD_CONV - 1), 0), (0, 0)))
    y_ssd, cnew, hnew = _ssd_call(x, p['norm_mix'], p['w_ssd'], p['conv_w'], p['conv_b'], p['dt_bias'],
                                  p['a_log'], p['d_exp'], p['ssd_norm'], cprev,
                                  ssd_h0.reshape(bsz, SSD_WIDTH, SSD_STATE), tiles['ssd'])
    o_gla, snew = _gla_call(x, p['norm_mix'], p['w_gla'], p['w_gate'], p['b_gate'], p['gla_norm'],
                            _gla_state_in(gla_s0), tiles['gla'])
    if fox_cache is None:
        k, v, lf, qh, kp, vp = _fox_proj_call(x, p['norm_mix'], p['w_fox'], p['q_norm'], p['k_norm'],
                                              p['f_bias'], tiles['fox_proj'], True)
        o_fox = _fox_prompt_call(qh, kp, vp, tiles['fox_q'])
    else:
        k, v, lf, qs = _fox_proj_call(x, p['norm_mix'], p['w_fox'], p['q_norm'], p['k_norm'],
                                      p['f_bias'], tiles['fox_proj'], False)
        cache_k, cache_v, lfp_t = fox_cache
        o_fox = _fox_sample_call(qs, k, v, lf, cache_k, cache_v, lfp_t, layer)
    state = (cnew[:, SUBLANES - (SSD_CONV - 1):, :],
             hnew.reshape(bsz, SSD_HEADS, SSD_HEAD_DIM, SSD_STATE),
             _gla_state_out(snew),
             k.reshape(bsz, t, FOX_HEADS, FOX_HEAD_DIM),
             v.reshape(bsz, t, FOX_HEADS, FOX_HEAD_DIM),
             lf[:, :, 0:FOX_HEADS])
    return (y_ssd, o_gla, o_fox), state


def _trunk_layer(x, p, conv_prev, ssd_h0, gla_s0, fox_cache, layer, tiles):
    bsz, t, d = x.shape
    x1 = _ffn_call(x.reshape(bsz * t, d), *p['ffn1'], tiles['ffn']).reshape(bsz, t, d)
    (y_ssd, o_gla, o_fox), state = _mix(x1, p, conv_prev, ssd_h0, gla_s0, fox_cache, layer, tiles)
    flat = lambda a: a.reshape(bsz * t, a.shape[-1])
    x3 = _out_ffn_call(flat(x1), flat(y_ssd), flat(o_gla), flat(o_fox), p['w_out'], *p['ffn2'], tiles['ffn'])
    return x3.reshape(bsz, t, d), state


PROMPT_TILES = dict(ffn=512, ssd=256, gla=256, fox_proj=512, fox_q=512)
SAMPLE_TILES = dict(ffn=512, ssd=64, gla=64, fox_proj=64, fox_q=64)


def kernel(x_prompt, x_sample, state_ssd_conv, state_ssd, state_gla, cache_fox_k, cache_fox_v, cache_fox_logf, norm_ffn1, w1_gate, w1_up, w1_down, norm_mix, w_in, ssd_conv_w, ssd_conv_b, ssd_dt_bias, ssd_a_log, ssd_d, ssd_norm, gla_w_gate, gla_b_gate, gla_norm, fox_q_norm, fox_k_norm, fox_f_bias, w_out, norm_ffn2, w2_gate, w2_up, w2_down):
    depth = w_in.shape[0]
    bp = x_prompt.shape[0]
    bs, past = cache_fox_logf.shape[1:3]
    weights = (norm_ffn1, w1_gate, w1_up, w1_down, norm_mix, w_in, ssd_conv_w, ssd_conv_b, ssd_dt_bias,
               ssd_a_log, ssd_d, ssd_norm, gla_w_gate, gla_b_gate, gla_norm, fox_q_norm, fox_k_norm,
               fox_f_bias, w_out, norm_ffn2, w2_gate, w2_up, w2_down)
    cache_k = cache_fox_k.reshape(depth, bs, past, FOX_WIDTH)
    cache_v = cache_fox_v.reshape(depth, bs, past, FOX_WIDTH)
    lfp_t = jnp.pad(jnp.swapaxes(cache_fox_logf, 2, 3), ((0, 0), (0, 0), (0, SUBLANES - FOX_HEADS), (0, 0)))
    zeros_conv = jnp.zeros((bp, SSD_CONV - 1, SSD_CONV_CH), F32)
    zeros_ssd = jnp.zeros((bp, SSD_HEADS, SSD_HEAD_DIM, SSD_STATE), F32)
    zeros_gla = jnp.zeros((bp, GLA_HEADS, GLA_DK, GLA_DV), F32)

    xp, xs = x_prompt, x_sample
    p_new = [[] for _ in range(6)]
    s_new = [[] for _ in range(6)]
    for l in range(depth):
        p = _layer_params(l, *weights)
        xp, st_p = _trunk_layer(xp, p, zeros_conv, zeros_ssd, zeros_gla, None, l, PROMPT_TILES)
        xs, st_s = _trunk_layer(xs, p, state_ssd_conv[l], state_ssd[l], state_gla[l],
                                (cache_k, cache_v, lfp_t), l, SAMPLE_TILES)
        for i in range(6):
            p_new[i].append(st_p[i])
            s_new[i].append(st_s[i])
    outs_p = [jnp.stack(a) for a in p_new]
    outs_s = [jnp.stack(a) for a in s_new]
    return (xp, xs, *outs_p, *outs_s)
```

```python
import functools

import numpy as np
import jax
import jax.numpy as jnp
from jax import lax
from jax.experimental import pallas as pl
from jax.experimental.pallas import tpu as pltpu

F32 = jnp.float32
BF16 = jnp.bfloat16

EPS = 1e-6
D_MODEL = 1024
D_FF = 2816
SSD_HEADS = 8
SSD_HEAD_DIM = 64
SSD_WIDTH = 512
SSD_GROUPS = 2
SSD_STATE = 128
SSD_CONV = 4
SSD_CONV_CH = 1024
GLA_HEADS = 4
GLA_DK = 32
GLA_DV = 64
GLA_WIDTH = 256
GLA_RANK = 16
GLA_TAU = 16.0
GLA_CHUNK = 64
FOX_HEADS = 4
FOX_HEAD_DIM = 64
FOX_WIDTH = 256
LANES = 128
SUBLANES = 8
MIB = 1024 * 1024

_OFF_Z, _OFF_XBC, _OFF_DT = 0, 512, 1536
_OFF_GQ, _OFF_GR = 1544, 2312
_OFF_FQ, _OFF_FF = 2328, 3096
SSD_COLS = SSD_WIDTH + SSD_CONV_CH + LANES
GLA_COLS = 2 * GLA_HEADS * GLA_DK + 2 * GLA_WIDTH + LANES
FOX_COLS = 3 * FOX_WIDTH + LANES
AUX_PARTS = 3
FOX_QB = 256
FOX_KB = 256
FOX_SWEEP = 512
FOX_VT_ROWS = FOX_HEAD_DIM + 16


def _dot(a, b):
    return jnp.dot(a, b, preferred_element_type=F32)


def _dot_nt(a, b):
    return lax.dot_general(a, b, (((1,), (1,)), ((), ())), preferred_element_type=F32)


def _dot_tn(a, b):
    return lax.dot_general(a, b, (((0,), (0,)), ((), ())), preferred_element_type=F32)


def _split_bf16(x, parts):
    out = []
    r = x
    for i in range(parts):
        p = r.astype(BF16)
        out.append(p)
        if i + 1 < parts:
            r = r - p.astype(F32)
    return out


def _dot_sel_lhs(sel, x, parts=3):
    acc = None
    for p in _split_bf16(x, parts):
        t = _dot(sel, p)
        acc = t if acc is None else acc + t
    return acc


def _dot_sel_rhs(x, sel, parts=3):
    acc = None
    for p in _split_bf16(x, parts):
        t = _dot(p, sel)
        acc = t if acc is None else acc + t
    return acc


def _rms(x, w):
    ms = jnp.mean(x * x, axis=-1, keepdims=True)
    return x * lax.rsqrt(ms + EPS) * w


def _silu(x):
    return x * jax.nn.sigmoid(x)


def _softplus(x):
    return jnp.maximum(x, 0.0) + jnp.log1p(jnp.exp(-jnp.abs(x)))


def _log_sigmoid(x):
    return -_softplus(-x)


def _tri_mask(n, m=None):
    m = n if m is None else m
    r = lax.broadcasted_iota(jnp.int32, (n, m), 0)
    c = lax.broadcasted_iota(jnp.int32, (n, m), 1)
    return r, c


def _lane_pair_select(a_even, a_odd):
    lane = lax.broadcasted_iota(jnp.int32, a_even.shape, 1)
    return jnp.where(lane < 64, a_even, a_odd)


def _expand_heads(v, heads, rows):
    pieces = []
    for i in range(0, len(heads), 2):
        a = jnp.broadcast_to(v[:, heads[i]:heads[i] + 1], (rows, LANES))
        b = jnp.broadcast_to(v[:, heads[i + 1]:heads[i + 1] + 1], (rows, LANES))
        pieces.append(_lane_pair_select(a, b))
    return pieces[0] if len(pieces) == 1 else jnp.concatenate(pieces, axis=1)


def _group_mean_matrix(width, group):
    r, c = _tri_mask(width)
    return jnp.where((r // group) == (c // group), 1.0 / group, 0.0).astype(BF16)


def _params(sem, vmem_mib):
    return pltpu.CompilerParams(dimension_semantics=sem, vmem_limit_bytes=vmem_mib * MIB)


def _const_spec(shape, single=False):
    nd = len(shape)
    if single:
        return pl.BlockSpec(shape, lambda *_: (0,) * nd, pipeline_mode=pl.Buffered(1))
    return pl.BlockSpec(shape, lambda *_: (0,) * nd)


def _swiglu_half(x, g_ref, wg_ref, wu_ref, wd_ref):
    h = _rms(x, g_ref[...]).astype(BF16)
    gate = _dot(h, wg_ref[...])
    up = _dot(h, wu_ref[...])
    a = (_silu(gate) * up).astype(BF16)
    return x + 0.5 * _dot(a, wd_ref[...])


def _ffn_body(x_ref, g_ref, wg_ref, wu_ref, wd_ref, o_ref):
    o_ref[...] = _swiglu_half(x_ref[...], g_ref, wg_ref, wu_ref, wd_ref)


def _out_ffn_body(x_ref, ys_ref, og_ref, of_ref, wo_ref, g_ref, wg_ref, wu_ref, wd_ref, o_ref):
    x = x_ref[...]
    x = x + (_dot(ys_ref[...], wo_ref[0:SSD_WIDTH, :])
             + _dot(og_ref[...], wo_ref[SSD_WIDTH:SSD_WIDTH + GLA_WIDTH, :])
             + _dot(of_ref[...], wo_ref[SSD_WIDTH + GLA_WIDTH:, :]))
    o_ref[...] = _swiglu_half(x, g_ref, wg_ref, wu_ref, wd_ref)


def _ffn_weight_specs():
    return [_const_spec((1, D_MODEL)),
            _const_spec((D_MODEL, D_FF), single=True),
            _const_spec((D_MODEL, D_FF), single=True),
            _const_spec((D_FF, D_MODEL), single=True)]


def _row_tile(n, want):
    t = min(want, n)
    while n % t:
        t //= 2
    return t


def _ffn_call(x, g, wg, wu, wd, tm):
    n = x.shape[0]
    tm = _row_tile(n, tm)
    row = pl.BlockSpec((tm, D_MODEL), lambda i: (i, 0))
    return pl.pallas_call(
        _ffn_body, grid=(n // tm,),
        in_specs=[row] + _ffn_weight_specs(),
        out_specs=row,
        out_shape=jax.ShapeDtypeStruct((n, D_MODEL), F32),
        compiler_params=_params(("parallel",), 56),
        name="ffn",
    )(x, g, wg, wu, wd)


def _out_ffn_call(x, ys, og, of, wo, g, wg, wu, wd, tm):
    n = x.shape[0]
    tm = _row_tile(n, tm)
    row = lambda w: pl.BlockSpec((tm, w), lambda i: (i, 0))
    return pl.pallas_call(
        _out_ffn_body, grid=(n // tm,),
        in_specs=[row(D_MODEL), row(SSD_WIDTH), row(GLA_WIDTH), row(FOX_WIDTH),
                  _const_spec((D_MODEL, D_MODEL), single=True)] + _ffn_weight_specs(),
        out_specs=row(D_MODEL),
        out_shape=jax.ShapeDtypeStruct((n, D_MODEL), F32),
        compiler_params=_params(("parallel",), 56),
        name="out_ffn",
    )(x, ys, og, of, wo, g, wg, wu, wd)


def _ssd_body(x_ref, nm_ref, w_ref, cw_ref, cb_ref, dtb_ref, alog_ref, dexp_ref, nw_ref,
              cprev_ref, h0_ref, y_ref, cnew_ref, hnew_ref, xbuf, hst, *, tt):
    t = pl.program_id(1)

    @pl.when(t == 0)
    def _():
        xbuf[0:SUBLANES, :] = cprev_ref[...]
        hst[...] = h0_ref[...]

    h = _rms(x_ref[...], nm_ref[...]).astype(BF16)
    u = _dot(h, w_ref[...])
    z = u[:, 0:SSD_WIDTH]
    xbc = u[:, SSD_WIDTH:SSD_WIDTH + SSD_CONV_CH]
    dt_raw = u[:, SSD_WIDTH + SSD_CONV_CH:]

    xbuf[SUBLANES:SUBLANES + tt, :] = xbc
    conv = xbuf[SUBLANES - 3:SUBLANES - 3 + tt, :] * cw_ref[0:1, :]
    conv = conv + xbuf[SUBLANES - 2:SUBLANES - 2 + tt, :] * cw_ref[1:2, :]
    conv = conv + xbuf[SUBLANES - 1:SUBLANES - 1 + tt, :] * cw_ref[2:3, :]
    conv = conv + xbc * cw_ref[3:4, :]
    xa = _silu(conv + cb_ref[...])
    xbuf[0:SUBLANES, :] = xbc[tt - SUBLANES:, :]

    xs = xa[:, 0:SSD_WIDTH]
    bm = xa[:, SSD_WIDTH:SSD_WIDTH + SSD_GROUPS * SSD_STATE]
    cm = xa[:, SSD_WIDTH + SSD_GROUPS * SSD_STATE:]
    dt = _softplus(dt_raw + dtb_ref[...])
    a = dt * (-jnp.exp(alog_ref[...]))

    r, c = _tri_mask(tt)
    causal = r >= c
    acum = _dot_sel_lhs(causal.astype(BF16), a)
    acum_t = acum.T
    alast = acum[tt - 1:tt, :]
    ea = jnp.exp(acum)
    wend = jnp.exp(alast - acum) * dt

    heads_per_group = SSD_HEADS // SSD_GROUPS
    gw = heads_per_group * SSD_HEAD_DIM
    y_parts = []
    for g in range(SSD_GROUPS):
        heads = list(range(g * heads_per_group, (g + 1) * heads_per_group))
        bg = bm[:, g * SSD_STATE:(g + 1) * SSD_STATE].astype(BF16)
        cg = cm[:, g * SSD_STATE:(g + 1) * SSD_STATE].astype(BF16)
        cb = _dot_nt(cg, bg)
        xg = xs[:, g * gw:(g + 1) * gw]
        xdt = (xg * _expand_heads(dt, heads, tt)).astype(BF16)
        intra = []
        for pi in range(heads_per_group // 2):
            ys = []
            for e in range(2):
                hh = heads[2 * pi + e]
                seg = acum[:, hh:hh + 1] - acum_t[hh:hh + 1, :]
                m = (cb * jnp.where(causal, jnp.exp(seg), 0.0)).astype(BF16)
                ys.append(_dot(m, xdt[:, pi * LANES:(pi + 1) * LANES]))
            intra.append(_lane_pair_select(ys[0], ys[1]))
        y_intra = jnp.concatenate(intra, axis=1)

        hg = hst[g * gw:(g + 1) * gw, :]
        y_inter = _dot_nt(cg, hg.astype(BF16)) * _expand_heads(ea, heads, tt)
        y_parts.append(y_intra + y_inter)

        xw = (xg * _expand_heads(wend, heads, tt)).astype(BF16)
        upd = _dot_tn(xw, bg)
        cd = jnp.concatenate(
            [jnp.broadcast_to(jnp.exp(alast[:, hh:hh + 1]), (SSD_HEAD_DIM, SSD_STATE)) for hh in heads],
            axis=0)
        hst[g * gw:(g + 1) * gw, :] = hg * cd + upd

    y = jnp.concatenate(y_parts, axis=1) + dexp_ref[...] * xs
    y = y * _silu(z)
    nw = nw_ref[...]
    outs = []
    for g in range(SSD_GROUPS):
        outs.append(_rms(y[:, g * gw:(g + 1) * gw], nw[:, g * gw:(g + 1) * gw]))
    y_ref[...] = jnp.concatenate(outs, axis=1).astype(y_ref.dtype)

    @pl.when(t == pl.num_programs(1) - 1)
    def _():
        cnew_ref[...] = xbuf[0:SUBLANES, :]
        hnew_ref[...] = hst[...]


def _ssd_call(x, nm, w, cw, cb, dtb, alog, dexp, nw, cprev, h0, tt):
    bsz, t, _ = x.shape
    tt = _row_tile(t, tt)
    bt = lambda w_: pl.BlockSpec((None, tt, w_), lambda b, i: (b, i, 0))
    per_b = lambda r_, w_: pl.BlockSpec((None, r_, w_), lambda b, i: (b, 0, 0))
    return pl.pallas_call(
        functools.partial(_ssd_body, tt=tt), grid=(bsz, t // tt),
        in_specs=[bt(D_MODEL), _const_spec((1, D_MODEL)), _const_spec((D_MODEL, SSD_COLS), single=True),
                  _const_spec((SSD_CONV, SSD_CONV_CH)), _const_spec((1, SSD_CONV_CH)),
                  _const_spec((1, LANES)), _const_spec((1, LANES)),
                  _const_spec((1, SSD_WIDTH)), _const_spec((1, SSD_WIDTH)),
                  per_b(SUBLANES, SSD_CONV_CH), per_b(SSD_WIDTH, SSD_STATE)],
        out_specs=[bt(SSD_WIDTH), per_b(SUBLANES, SSD_CONV_CH), per_b(SSD_WIDTH, SSD_STATE)],
        out_shape=[jax.ShapeDtypeStruct((bsz, t, SSD_WIDTH), BF16),
                   jax.ShapeDtypeStruct((bsz, SUBLANES, SSD_CONV_CH), F32),
                   jax.ShapeDtypeStruct((bsz, SSD_WIDTH, SSD_STATE), F32)],
        scratch_shapes=[pltpu.VMEM((tt + SUBLANES, SSD_CONV_CH), F32),
                        pltpu.VMEM((SSD_WIDTH, SSD_STATE), F32)],
        compiler_params=_params(("parallel", "arbitrary"), 40),
        name="ssd",
    )(x, nm, w, cw, cb, dtb, alog, dexp, nw, cprev, h0)


def _gla_body(x_ref, nm_ref, w_ref, wgate_ref, bgate_ref, gn_ref, s0_ref, o_ref, snew_ref, st, *, tt):
    t = pl.program_id(1)

    @pl.when(t == 0)
    def _():
        st[...] = s0_ref[...]

    hk = GLA_HEADS * GLA_DK
    h = _rms(x_ref[...], nm_ref[...]).astype(BF16)
    u = _dot(h, w_ref[...])
    q = u[:, 0:hk] * (GLA_DK ** -0.5)
    k = u[:, hk:2 * hk]
    v = u[:, 2 * hk:2 * hk + GLA_WIDTH].astype(BF16)
    gg = u[:, 2 * hk + GLA_WIDTH:2 * hk + 2 * GLA_WIDTH]
    gr = u[:, 2 * hk + 2 * GLA_WIDTH:].astype(BF16)
    la = _log_sigmoid(_dot(gr, wgate_ref[...]) + bgate_ref[...]) / GLA_TAU

    r, c = _tri_mask(tt)
    same = (r // GLA_CHUNK) == (c // GLA_CHUNK)
    bcum = _dot_sel_lhs((same & (r >= c)).astype(BF16), la)
    blast = _dot_sel_lhs(same.astype(BF16), la)
    qd = (q * jnp.exp(bcum)).astype(BF16)
    kd = (k * jnp.exp(-bcum)).astype(BF16)
    kend = (k * jnp.exp(blast - bcum)).astype(BF16)
    eblast = jnp.exp(blast)

    lane_k = lax.broadcasted_iota(jnp.int32, (GLA_CHUNK, hk), 1) // GLA_DK
    lane_v = lax.broadcasted_iota(jnp.int32, (GLA_CHUNK, GLA_WIDTH), 1) // GLA_DV
    ar, ac = _tri_mask(GLA_HEADS * GLA_CHUNK, GLA_CHUNK)
    att_causal = (ar % GLA_CHUNK) >= ac
    sr, sc = _tri_mask(GLA_WIDTH, hk)
    diag = (sr // GLA_DV) == (sc // GLA_DK)

    outs = []
    for ci in range(tt // GLA_CHUNK):
        sl = slice(ci * GLA_CHUNK, (ci + 1) * GLA_CHUNK)
        qd_c, kd_c, kend_c, v_c = qd[sl], kd[sl], kend[sl], v[sl]
        lhs = jnp.concatenate([jnp.where(lane_k == hh, qd_c, jnp.zeros_like(qd_c))
                               for hh in range(GLA_HEADS)], axis=0)
        att = jnp.where(att_causal, _dot_nt(lhs, kd_c), 0.0).astype(BF16)
        res = _dot(att, v_c)
        o_intra = jnp.zeros((GLA_CHUNK, GLA_WIDTH), F32)
        for hh in range(GLA_HEADS):
            o_intra = jnp.where(lane_v == hh, res[hh * GLA_CHUNK:(hh + 1) * GLA_CHUNK], o_intra)
        s_prev = st[...]
        o_inter = _dot_nt(qd_c, s_prev.astype(BF16))
        outs.append(o_intra + o_inter)
        upd = _dot_tn(v_c, kend_c)
        st[...] = s_prev * eblast[ci * GLA_CHUNK:ci * GLA_CHUNK + 1, :] + jnp.where(diag, upd, 0.0)

    o = jnp.concatenate(outs, axis=0) if len(outs) > 1 else outs[0]
    ms = _dot_sel_rhs(o * o, _group_mean_matrix(GLA_WIDTH, GLA_DV), parts=2)
    o = o * lax.rsqrt(ms + EPS) * gn_ref[...]
    o_ref[...] = (o * _silu(gg)).astype(o_ref.dtype)

    @pl.when(t == pl.num_programs(1) - 1)
    def _():
        snew_ref[...] = st[...]


def _gla_call(x, nm, w, wgate, bgate, gn, s0, tt):
    bsz, t, _ = x.shape
    tt = _row_tile(t, tt)
    hk = GLA_HEADS * GLA_DK
    bt = lambda w_: pl.BlockSpec((None, tt, w_), lambda b, i: (b, i, 0))
    per_b = pl.BlockSpec((None, GLA_WIDTH, hk), lambda b, i: (b, 0, 0))
    return pl.pallas_call(
        functools.partial(_gla_body, tt=tt), grid=(bsz, t // tt),
        in_specs=[bt(D_MODEL), _const_spec((1, D_MODEL)), _const_spec((D_MODEL, GLA_COLS), single=True),
                  _const_spec((LANES, hk)), _const_spec((1, hk)), _const_spec((1, GLA_WIDTH)), per_b],
        out_specs=[bt(GLA_WIDTH), per_b],
        out_shape=[jax.ShapeDtypeStruct((bsz, t, GLA_WIDTH), BF16),
                   jax.ShapeDtypeStruct((bsz, GLA_WIDTH, hk), F32)],
        scratch_shapes=[pltpu.VMEM((GLA_WIDTH, hk), F32)],
        compiler_params=_params(("parallel", "arbitrary"), 32),
        name="gla",
    )(x, nm, w, wgate, bgate, gn, s0)


def _fox_proj_body(x_ref, nm_ref, w_ref, qn_ref, kn_ref, fb_ref, pq_ref, pk_ref, arow_ref,
                   k_ref, v_ref, lf_ref, *rest, tt, attn_layout):
    if attn_layout:
        qt_ref, kp_ref, vt_ref, carry = rest
    else:
        qs_ref, carry = rest
    t = pl.program_id(1)

    @pl.when(t == 0)
    def _():
        carry[...] = jnp.zeros_like(carry)

    h = _rms(x_ref[...], nm_ref[...]).astype(BF16)
    u = _dot(h, w_ref[...])
    fq = u[:, 0:FOX_WIDTH]
    fk = u[:, FOX_WIDTH:2 * FOX_WIDTH]
    fv = u[:, 2 * FOX_WIDTH:3 * FOX_WIDTH]
    ff = u[:, 3 * FOX_WIDTH:]

    gmat = _group_mean_matrix(FOX_WIDTH, FOX_HEAD_DIM)
    qn = fq * lax.rsqrt(_dot_sel_rhs(fq * fq, gmat, parts=2) + EPS) * qn_ref[...]
    kn = fk * lax.rsqrt(_dot_sel_rhs(fk * fk, gmat, parts=2) + EPS) * kn_ref[...]
    lf = _log_sigmoid(ff + fb_ref[...])
    k_ref[...] = kn
    v_ref[...] = fv
    lf_ref[...] = lf
    qs = (qn * (FOX_HEAD_DIM ** -0.5)).astype(BF16)

    if not attn_layout:
        qs_ref[...] = qs
        return

    r, c = _tri_mask(tt)
    cum = carry[...] + _dot_sel_lhs((r >= c).astype(BF16), lf)
    carry[...] = cum[tt - 1:tt, :]
    c3 = jnp.concatenate(_split_bf16(cum, AUX_PARTS), axis=1)
    aux_k = (_dot(c3, pk_ref[...]) + arow_ref[0:1, :]).astype(BF16)
    knb = kn.astype(BF16)
    fv_t = fv.T
    ones = jnp.ones((FOX_VT_ROWS - FOX_HEAD_DIM, tt), F32)
    lane = lax.broadcasted_iota(jnp.int32, (tt, LANES), 1)
    for p in range(FOX_HEADS // 2):
        sl = slice(p * LANES, (p + 1) * LANES)
        kp_ref[p, :, 0:LANES] = knb[:, sl]
        kp_ref[p, :, LANES:] = aux_k
        for e in range(2):
            hh = 2 * p + e
            vt = jnp.concatenate([fv_t[hh * FOX_HEAD_DIM:(hh + 1) * FOX_HEAD_DIM, :], ones], axis=0).astype(BF16)
            for j in range(tt // FOX_KB):
                vt_ref[hh, j] = vt[:, j * FOX_KB:(j + 1) * FOX_KB]
            qmask = jnp.where((lane // FOX_HEAD_DIM) == e, qs[:, sl], jnp.zeros_like(qs[:, sl])).astype(F32)
            aux_q = _dot(c3, pq_ref[hh]) + arow_ref[1 + hh:2 + hh, :]
            q_t = jnp.concatenate([qmask.T, aux_q.T], axis=0).astype(BF16)
            for j in range(tt // FOX_QB):
                qt_ref[hh, j] = q_t[:, j * FOX_QB:(j + 1) * FOX_QB]


def _fox_aux_constants():
    nslot = FOX_HEADS * AUX_PARTS
    pq = np.zeros((FOX_HEADS, AUX_PARTS * LANES, LANES), np.float32)
    pk = np.zeros((AUX_PARTS * LANES, LANES), np.float32)
    arow = np.zeros((SUBLANES, LANES), np.float32)
    for hh in range(FOX_HEADS):
        for j in range(AUX_PARTS):
            pq[hh, j * LANES + hh, AUX_PARTS * hh + j] = 1.0
            pk[j * LANES + hh, nslot + AUX_PARTS * hh + j] = -1.0
            arow[0, AUX_PARTS * hh + j] = 1.0
            arow[1 + hh, nslot + AUX_PARTS * hh + j] = 1.0
    return jnp.asarray(pq, BF16), jnp.asarray(pk, BF16), jnp.asarray(arow, F32)


def _fox_proj_call(x, nm, w, qn, kn, fb, tt, attn_layout):
    bsz, t, _ = x.shape
    tt = _row_tile(t, tt)
    pq, pk, arow = _fox_aux_constants()
    bt = lambda w_: pl.BlockSpec((None, tt, w_), lambda b, i: (b, i, 0))
    bht = lambda n_, w_: pl.BlockSpec((None, n_, tt, w_), lambda b, i: (b, 0, i, 0))
    out_specs = [bt(FOX_WIDTH), bt(FOX_WIDTH), bt(LANES)]
    out_shape = [jax.ShapeDtypeStruct((bsz, t, FOX_WIDTH), F32),
                 jax.ShapeDtypeStruct((bsz, t, FOX_WIDTH), F32),
                 jax.ShapeDtypeStruct((bsz, t, LANES), F32)]
    if attn_layout:
        tiled = lambda n_, r_, w_: pl.BlockSpec((None, FOX_HEADS, tt // n_, r_, w_), lambda b, i: (b, 0, i, 0, 0))
        out_specs += [tiled(FOX_QB, 2 * LANES, FOX_QB), bht(FOX_HEADS // 2, 2 * LANES),
                      tiled(FOX_KB, FOX_VT_ROWS, FOX_KB)]
        out_shape += [jax.ShapeDtypeStruct((bsz, FOX_HEADS, t // FOX_QB, 2 * LANES, FOX_QB), BF16),
                      jax.ShapeDtypeStruct((bsz, FOX_HEADS // 2, t, 2 * LANES), BF16),
                      jax.ShapeDtypeStruct((bsz, FOX_HEADS, t // FOX_KB, FOX_VT_ROWS, FOX_KB), BF16)]
    else:
        out_specs += [bt(FOX_WIDTH)]
        out_shape += [jax.ShapeDtypeStruct((bsz, t, FOX_WIDTH), BF16)]
    return pl.pallas_call(
        functools.partial(_fox_proj_body, tt=tt, attn_layout=attn_layout), grid=(bsz, t // tt),
        in_specs=[bt(D_MODEL), _const_spec((1, D_MODEL)), _const_spec((D_MODEL, FOX_COLS), single=True),
                  _const_spec((1, FOX_WIDTH)), _const_spec((1, FOX_WIDTH)), _const_spec((1, LANES)),
                  _const_spec(pq.shape), _const_spec(pk.shape), _const_spec(arow.shape)],
        out_specs=out_specs, out_shape=out_shape,
        scratch_shapes=[pltpu.VMEM((1, LANES), F32)],
        compiler_params=_params(("parallel", "arbitrary"), 32),
        name="fox_proj",
    )(x, nm, w, qn, kn, fb, pq, pk, arow)


def _fox_prompt_body(qt_ref, k_ref, vt_ref, o_ref, m_s, acc_s, s_s, *, nsub):
    qi = pl.program_id(2)
    qtile = nsub * FOX_QB
    chains = [(si, e) for si in range(nsub) for e in range(2)]
    m_s[...] = jnp.full_like(m_s, -jnp.inf)
    acc_s[...] = jnp.zeros_like(acc_s)

    def sweep(k0, vt0, nks, masked):
        for c, (si, e) in enumerate(chains):
            nk = nks[c]
            s = _dot(k_ref[pl.ds(k0, nk), :], qt_ref[e, si])
            if masked:
                r, col = _tri_mask(FOX_QB)
                tail = jnp.where(r <= col, s[nk - FOX_QB:], -jnp.inf)
                s = tail if nk == FOX_QB else jnp.concatenate([s[:nk - FOX_QB], tail], axis=0)
            s_s[c, 0:nk, :] = s
        for c, (si, e) in enumerate(chains):
            nk = nks[c]
            s = s_s[c, 0:nk, :]
            m_old = m_s[c]
            m_new = jnp.maximum(m_old, jnp.max(s, axis=0, keepdims=True))
            alpha = jnp.exp(m_old - m_new)
            p = jnp.exp(s - m_new).astype(BF16)
            vts = [vt_ref[e, vt0 + j] for j in range(nk // FOX_KB)]
            vt = vts[0] if len(vts) == 1 else jnp.concatenate(vts, axis=1)
            acc_s[c] = alpha * acc_s[c] + _dot(vt, p)
            m_s[c] = m_new

    def common(i, carry):
        k0 = pl.multiple_of(i * FOX_SWEEP, FOX_SWEEP)
        sweep(k0, i * (FOX_SWEEP // FOX_KB), [FOX_SWEEP] * len(chains), False)
        return carry

    lax.fori_loop(0, qi * (qtile // FOX_SWEEP), common, 0)
    sweep(pl.multiple_of(qi * qtile, qtile), qi * (qtile // FOX_KB),
          [(si + 1) * FOX_QB for si, _ in chains], True)

    for si in range(nsub):
        outs = []
        for e in range(2):
            acc = acc_s[chains.index((si, e))]
            outs.append(acc[0:FOX_HEAD_DIM] / acc[FOX_HEAD_DIM:FOX_HEAD_DIM + 1])
        o = jnp.concatenate(outs, axis=0).T
        o_ref[si * FOX_QB:(si + 1) * FOX_QB, :] = o.astype(o_ref.dtype)


def _fox_prompt_call(qt, kp, vt, tq):
    bsz, _, nqb, _, _ = qt.shape
    t = kp.shape[2]
    nsub = _row_tile(nqb, max(FOX_SWEEP // FOX_QB, tq // FOX_QB))
    npair = FOX_HEADS // 2
    nch = 2 * nsub
    return pl.pallas_call(
        functools.partial(_fox_prompt_body, nsub=nsub), grid=(bsz, npair, nqb // nsub),
        in_specs=[pl.BlockSpec((None, 2, nsub, 2 * LANES, FOX_QB), lambda b, p, i: (b, p, i, 0, 0)),
                  pl.BlockSpec((None, None, t, 2 * LANES), lambda b, p, i: (b, p, 0, 0)),
                  pl.BlockSpec((None, 2, t // FOX_KB, FOX_VT_ROWS, FOX_KB), lambda b, p, i: (b, p, 0, 0, 0))],
        out_specs=pl.BlockSpec((None, nsub * FOX_QB, LANES), lambda b, p, i: (b, i, p)),
        out_shape=jax.ShapeDtypeStruct((bsz, t, FOX_WIDTH), BF16),
        scratch_shapes=[pltpu.VMEM((nch, 1, FOX_QB), F32),
                        pltpu.VMEM((nch, FOX_VT_ROWS, FOX_QB), F32),
                        pltpu.VMEM((nch, nsub * FOX_QB, FOX_QB), F32)],
        compiler_params=_params(("parallel", "parallel", "arbitrary"), 40),
        name="fox_attn_prompt",
    )(qt, kp, vt)


def _fox_sample_body(q_ref, kn_ref, vn_ref, lfn_ref, kc_ref, vc_ref, lfp_ref, o_ref, *, tn, past, seg):
    r, c = _tri_mask(seg)
    upper = (r <= c).astype(BF16)
    carry = jnp.zeros((SUBLANES, 1), F32)
    cps = []
    for j in range(past // seg):
        cs = carry + _dot_sel_rhs(lfp_ref[:, j * seg:(j + 1) * seg], upper)
        cps.append(cs)
        carry = cs[:, seg - 1:seg]
    cp = jnp.concatenate(cps, axis=1) if len(cps) > 1 else cps[0]
    lfn = lfn_ref[...]
    if tn < LANES:
        lfn = jnp.concatenate([lfn, jnp.zeros((LANES - tn, LANES), F32)], axis=0)
    r, c = _tri_mask(LANES)
    cn = _dot_sel_lhs((r >= c).astype(BF16), lfn)
    cn_t = cn.T

    q = q_ref[...]
    kc = kc_ref[...].astype(BF16)
    vc = vc_ref[...].astype(BF16)
    kn = kn_ref[...].astype(BF16)
    vn = vn_ref[...].astype(BF16)
    lane = lax.broadcasted_iota(jnp.int32, (tn, FOX_WIDTH), 1) // FOX_HEAD_DIM
    r, c = _tri_mask(tn)
    out = jnp.zeros((tn, FOX_WIDTH), F32)
    for hh in range(FOX_HEADS):
        qh = jnp.where(lane == hh, q, jnp.zeros_like(q))
        cq = cn[0:tn, hh:hh + 1]
        s_past = _dot_nt(qh, kc) + ((carry[hh:hh + 1, :] + cq) - cp[hh:hh + 1, :])
        s_new = _dot_nt(qh, kn) + (cq - cn_t[hh:hh + 1, 0:tn])
        s_new = jnp.where(r >= c, s_new, -jnp.inf)
        m = jnp.maximum(jnp.max(s_past, axis=-1, keepdims=True), jnp.max(s_new, axis=-1, keepdims=True))
        p_past = jnp.exp(s_past - m)
        p_new = jnp.exp(s_new - m)
        denom = jnp.sum(p_past, axis=-1, keepdims=True) + jnp.sum(p_new, axis=-1, keepdims=True)
        o = (_dot(p_past.astype(BF16), vc) + _dot(p_new.astype(BF16), vn)) / denom
        out = jnp.where(lane == hh, o, out)
    o_ref[...] = out.astype(o_ref.dtype)


def _fox_sample_call(q, kn, vn, lfn, cache_k, cache_v, lfp_t, layer):
    bsz, tn, _ = q.shape
    past = cache_k.shape[2]
    seg = _row_tile(past, 512)
    bt = lambda w_: pl.BlockSpec((None, tn, w_), lambda b: (b, 0, 0))
    cache = pl.BlockSpec((None, None, past, FOX_WIDTH), lambda b: (layer, b, 0, 0))
    return pl.pallas_call(
        functools.partial(_fox_sample_body, tn=tn, past=past, seg=seg), grid=(bsz,),
        in_specs=[bt(FOX_WIDTH), bt(FOX_WIDTH), bt(FOX_WIDTH), bt(LANES), cache, cache,
                  pl.BlockSpec((None, None, SUBLANES, past), lambda b: (layer, b, 0, 0))],
        out_specs=bt(FOX_WIDTH),
        out_shape=jax.ShapeDtypeStruct((bsz, tn, FOX_WIDTH), BF16),
        compiler_params=_params(("parallel",), 48),
        name="fox_attn_sample",
    )(q, kn, vn, lfn, cache_k, cache_v, lfp_t)


def _pad_cols(a, width):
    return jnp.pad(a, ((0, 0), (0, width - a.shape[1])))


def _layer_params(l, norm_ffn1, w1_gate, w1_up, w1_down, norm_mix, w_in, ssd_conv_w, ssd_conv_b,
                  ssd_dt_bias, ssd_a_log, ssd_d, ssd_norm, gla_w_gate, gla_b_gate, gla_norm,
                  fox_q_norm, fox_k_norm, fox_f_bias, w_out, norm_ffn2, w2_gate, w2_up, w2_down):
    wi = w_in[l]
    row = lambda a: a.reshape(1, -1).astype(F32)
    hk = GLA_HEADS * GLA_DK
    p = dict(
        ffn1=(row(norm_ffn1[l]), w1_gate[l].astype(BF16), w1_up[l].astype(BF16), w1_down[l].astype(BF16)),
        ffn2=(row(norm_ffn2[l]), w2_gate[l].astype(BF16), w2_up[l].astype(BF16), w2_down[l].astype(BF16)),
        norm_mix=row(norm_mix[l]),
        w_out=w_out[l].astype(BF16),
        w_ssd=jnp.concatenate([wi[:, _OFF_Z:_OFF_DT], _pad_cols(wi[:, _OFF_DT:_OFF_GQ], LANES)], axis=1).astype(BF16),
        conv_w=ssd_conv_w[l].astype(F32),
        conv_b=row(ssd_conv_b[l]),
        dt_bias=_pad_cols(row(ssd_dt_bias[l]), LANES),
        a_log=_pad_cols(row(ssd_a_log[l]), LANES),
        d_exp=row(jnp.repeat(ssd_d[l], SSD_HEAD_DIM)),
        ssd_norm=row(ssd_norm[l]),
        w_gla=jnp.concatenate([wi[:, _OFF_GQ:_OFF_GR], _pad_cols(wi[:, _OFF_GR:_OFF_FQ], LANES)], axis=1).astype(BF16),
        w_gate=jnp.pad(gla_w_gate[l], ((0, LANES - GLA_RANK), (0, 0))).astype(BF16),
        b_gate=row(gla_b_gate[l]),
        gla_norm=row(jnp.tile(gla_norm[l], GLA_HEADS)),
        w_fox=jnp.concatenate([wi[:, _OFF_FQ:_OFF_FF], _pad_cols(wi[:, _OFF_FF:], LANES)], axis=1).astype(BF16),
        q_norm=row(jnp.tile(fox_q_norm[l], FOX_HEADS)),
        k_norm=row(jnp.tile(fox_k_norm[l], FOX_HEADS)),
        f_bias=_pad_cols(row(fox_f_bias[l]), LANES),
    )
    return p


def _gla_state_in(s):
    bsz = s.shape[0]
    eye = jnp.eye(GLA_HEADS, dtype=s.dtype)
    full = jnp.einsum('bhkv,hg->bhvgk', s, eye)
    return full.reshape(bsz, GLA_HEADS * GLA_DV, GLA_HEADS * GLA_DK)


def _gla_state_out(st):
    bsz = st.shape[0]
    full = st.reshape(bsz, GLA_HEADS, GLA_DV, GLA_HEADS, GLA_DK)
    idx = jnp.arange(GLA_HEADS)
    diag = full[:, idx, :, idx, :]
    return jnp.transpose(diag, (1, 0, 3, 2))


def _mix(x, p, conv_prev, ssd_h0, gla_s0, fox_cache, layer, tiles):
    bsz, t, _ = x.shape
    cprev = jnp.pad(conv_prev, ((0, 0), (SUBLANES - (SSD_CONV - 1), 0), (0, 0)))
    y_ssd, cnew, hnew = _ssd_call(x, p['norm_mix'], p['w_ssd'], p['conv_w'], p['conv_b'], p['dt_bias'],
                                  p['a_log'], p['d_exp'], p['ssd_norm'], cprev,
                                  ssd_h0.reshape(bsz, SSD_WIDTH, SSD_STATE), tiles['ssd'])
    o_gla, snew = _gla_call(x, p['norm_mix'], p['w_gla'], p['w_gate'], p['b_gate'], p['gla_norm'],
                            _gla_state_in(gla_s0), tiles['gla'])
    if fox_cache is None:
        k, v, lf, qt, kp, vt = _fox_proj_call(x, p['norm_mix'], p['w_fox'], p['q_norm'], p['k_norm'],
                                              p['f_bias'], tiles['fox_proj'], True)
        o_fox = _fox_prompt_call(qt, kp, vt, tiles['fox_q'])
    else:
        k, v, lf, qs = _fox_proj_call(x, p['norm_mix'], p['w_fox'], p['q_norm'], p['k_norm'],
                                      p['f_bias'], tiles['fox_proj'], False)
        cache_k, cache_v, lfp_t = fox_cache
        o_fox = _fox_sample_call(qs, k, v, lf, cache_k, cache_v, lfp_t, layer)
    state = (cnew[:, SUBLANES - (SSD_CONV - 1):, :],
             hnew.reshape(bsz, SSD_HEADS, SSD_HEAD_DIM, SSD_STATE),
             _gla_state_out(snew),
             k.reshape(bsz, t, FOX_HEADS, FOX_HEAD_DIM),
             v.reshape(bsz, t, FOX_HEADS, FOX_HEAD_DIM),
             lf[:, :, 0:FOX_HEADS])
    return (y_ssd, o_gla, o_fox), state


def _trunk_layer(x, p, conv_prev, ssd_h0, gla_s0, fox_cache, layer, tiles):
    bsz, t, d = x.shape
    x1 = _ffn_call(x.reshape(bsz * t, d), *p['ffn1'], tiles['ffn']).reshape(bsz, t, d)
    (y_ssd, o_gla, o_fox), state = _mix(x1, p, conv_prev, ssd_h0, gla_s0, fox_cache, layer, tiles)
    flat = lambda a: a.reshape(bsz * t, a.shape[-1])
    x3 = _out_ffn_call(flat(x1), flat(y_ssd), flat(o_gla), flat(o_fox), p['w_out'], *p['ffn2'], tiles['ffn'])
    return x3.reshape(bsz, t, d), state


PROMPT_TILES = dict(ffn=512, ssd=256, gla=256, fox_proj=512, fox_q=1024)
SAMPLE_TILES = dict(ffn=512, ssd=64, gla=64, fox_proj=64, fox_q=64)


def kernel(x_prompt, x_sample, state_ssd_conv, state_ssd, state_gla, cache_fox_k, cache_fox_v, cache_fox_logf, norm_ffn1, w1_gate, w1_up, w1_down, norm_mix, w_in, ssd_conv_w, ssd_conv_b, ssd_dt_bias, ssd_a_log, ssd_d, ssd_norm, gla_w_gate, gla_b_gate, gla_norm, fox_q_norm, fox_k_norm, fox_f_bias, w_out, norm_ffn2, w2_gate, w2_up, w2_down):
    depth = w_in.shape[0]
    bp = x_prompt.shape[0]
    bs, past = cache_fox_logf.shape[1:3]
    weights = (norm_ffn1, w1_gate, w1_up, w1_down, norm_mix, w_in, ssd_conv_w, ssd_conv_b, ssd_dt_bias,
               ssd_a_log, ssd_d, ssd_norm, gla_w_gate, gla_b_gate, gla_norm, fox_q_norm, fox_k_norm,
               fox_f_bias, w_out, norm_ffn2, w2_gate, w2_up, w2_down)
    cache_k = cache_fox_k.reshape(depth, bs, past, FOX_WIDTH)
    cache_v = cache_fox_v.reshape(depth, bs, past, FOX_WIDTH)
    lfp_t = jnp.pad(jnp.swapaxes(cache_fox_logf, 2, 3), ((0, 0), (0, 0), (0, SUBLANES - FOX_HEADS), (0, 0)))
    zeros_conv = jnp.zeros((bp, SSD_CONV - 1, SSD_CONV_CH), F32)
    zeros_ssd = jnp.zeros((bp, SSD_HEADS, SSD_HEAD_DIM, SSD_STATE), F32)
    zeros_gla = jnp.zeros((bp, GLA_HEADS, GLA_DK, GLA_DV), F32)

    xp, xs = x_prompt, x_sample
    p_new = [[] for _ in range(6)]
    s_new = [[] for _ in range(6)]
    for l in range(depth):
        p = _layer_params(l, *weights)
        xp, st_p = _trunk_layer(xp, p, zeros_conv, zeros_ssd, zeros_gla, None, l, PROMPT_TILES)
        xs, st_s = _trunk_layer(xs, p, state_ssd_conv[l], state_ssd[l], state_gla[l],
                                (cache_k, cache_v, lfp_t), l, SAMPLE_TILES)
        for i in range(6):
            p_new[i].append(st_p[i])
            s_new[i].append(st_s[i])
    outs_p = [jnp.stack(a) for a in p_new]
    outs_s = [jnp.stack(a) for a in s_new]
    return (xp, xs, *outs_p, *outs_s)
```

```python
import functools

import numpy as np
import jax
import jax.numpy as jnp
from jax import lax
from jax.experimental import pallas as pl
from jax.experimental.pallas import tpu as pltpu

F32 = jnp.float32
BF16 = jnp.bfloat16

EPS = 1e-6
D_MODEL = 1024
D_FF = 2816
SSD_HEADS = 8
SSD_HEAD_DIM = 64
SSD_WIDTH = 512
SSD_GROUPS = 2
SSD_STATE = 128
SSD_CONV = 4
SSD_CONV_CH = 1024
GLA_HEADS = 4
GLA_DK = 32
GLA_DV = 64
GLA_WIDTH = 256
GLA_RANK = 16
GLA_TAU = 16.0
GLA_CHUNK = 64
FOX_HEADS = 4
FOX_HEAD_DIM = 64
FOX_WIDTH = 256
LANES = 128
SUBLANES = 8
MIB = 1024 * 1024

_OFF_Z, _OFF_XBC, _OFF_DT = 0, 512, 1536
_OFF_GQ, _OFF_GR = 1544, 2312
_OFF_FQ, _OFF_FF = 2328, 3096
SSD_COLS = SSD_WIDTH + SSD_CONV_CH + LANES
GLA_COLS = 2 * GLA_HEADS * GLA_DK + 2 * GLA_WIDTH + LANES
FOX_COLS = 3 * FOX_WIDTH + LANES
AUX_PARTS = 3
FOX_QB = 256
FOX_KB = 256
FOX_SWEEP = 512
FOX_VT_ROWS = FOX_HEAD_DIM + 16


def _dot(a, b):
    return jnp.dot(a, b, preferred_element_type=F32)


def _dot_nt(a, b):
    return lax.dot_general(a, b, (((1,), (1,)), ((), ())), preferred_element_type=F32)


def _dot_tn(a, b):
    return lax.dot_general(a, b, (((0,), (0,)), ((), ())), preferred_element_type=F32)


def _split_bf16(x, parts):
    out = []
    r = x
    for i in range(parts):
        p = r.astype(BF16)
        out.append(p)
        if i + 1 < parts:
            r = r - p.astype(F32)
    return out


def _dot_sel_lhs(sel, x, parts=3):
    acc = None
    for p in _split_bf16(x, parts):
        t = _dot(sel, p)
        acc = t if acc is None else acc + t
    return acc


def _dot_sel_rhs(x, sel, parts=3):
    acc = None
    for p in _split_bf16(x, parts):
        t = _dot(p, sel)
        acc = t if acc is None else acc + t
    return acc


def _rms(x, w):
    ms = jnp.mean(x * x, axis=-1, keepdims=True)
    return x * lax.rsqrt(ms + EPS) * w


def _silu(x):
    return x * jax.nn.sigmoid(x)


def _softplus(x):
    return jnp.maximum(x, 0.0) + jnp.log1p(jnp.exp(-jnp.abs(x)))


def _log_sigmoid(x):
    return -_softplus(-x)


def _tri_mask(n, m=None):
    m = n if m is None else m
    r = lax.broadcasted_iota(jnp.int32, (n, m), 0)
    c = lax.broadcasted_iota(jnp.int32, (n, m), 1)
    return r, c


def _lane_pair_select(a_even, a_odd):
    lane = lax.broadcasted_iota(jnp.int32, a_even.shape, 1)
    return jnp.where(lane < 64, a_even, a_odd)


def _expand_heads(v, heads, rows):
    pieces = []
    for i in range(0, len(heads), 2):
        a = jnp.broadcast_to(v[:, heads[i]:heads[i] + 1], (rows, LANES))
        b = jnp.broadcast_to(v[:, heads[i + 1]:heads[i + 1] + 1], (rows, LANES))
        pieces.append(_lane_pair_select(a, b))
    return pieces[0] if len(pieces) == 1 else jnp.concatenate(pieces, axis=1)


def _group_mean_matrix(width, group):
    r, c = _tri_mask(width)
    return jnp.where((r // group) == (c // group), 1.0 / group, 0.0).astype(BF16)


def _params(sem, vmem_mib):
    return pltpu.CompilerParams(dimension_semantics=sem, vmem_limit_bytes=vmem_mib * MIB)


def _const_spec(shape, single=False):
    nd = len(shape)
    if single:
        return pl.BlockSpec(shape, lambda *_: (0,) * nd, pipeline_mode=pl.Buffered(1))
    return pl.BlockSpec(shape, lambda *_: (0,) * nd)


def _swiglu_half(x, g_ref, wg_ref, wu_ref, wd_ref):
    h = _rms(x, g_ref[...]).astype(BF16)
    gate = _dot(h, wg_ref[...])
    up = _dot(h, wu_ref[...])
    a = (_silu(gate) * up).astype(BF16)
    return x + 0.5 * _dot(a, wd_ref[...])


def _ffn_body(x_ref, g_ref, wg_ref, wu_ref, wd_ref, o_ref):
    o_ref[...] = _swiglu_half(x_ref[...], g_ref, wg_ref, wu_ref, wd_ref)


def _out_ffn_body(x_ref, ys_ref, og_ref, of_ref, wo_ref, g_ref, wg_ref, wu_ref, wd_ref, o_ref):
    x = x_ref[...]
    x = x + (_dot(ys_ref[...], wo_ref[0:SSD_WIDTH, :])
             + _dot(og_ref[...], wo_ref[SSD_WIDTH:SSD_WIDTH + GLA_WIDTH, :])
             + _dot(of_ref[...], wo_ref[SSD_WIDTH + GLA_WIDTH:, :]))
    o_ref[...] = _swiglu_half(x, g_ref, wg_ref, wu_ref, wd_ref)


def _ffn_weight_specs():
    return [_const_spec((1, D_MODEL)),
            _const_spec((D_MODEL, D_FF), single=True),
            _const_spec((D_MODEL, D_FF), single=True),
            _const_spec((D_FF, D_MODEL), single=True)]


def _row_tile(n, want):
    t = min(want, n)
    while n % t:
        t //= 2
    return t


def _ffn_call(x, g, wg, wu, wd, tm):
    n = x.shape[0]
    tm = _row_tile(n, tm)
    row = pl.BlockSpec((tm, D_MODEL), lambda i: (i, 0))
    return pl.pallas_call(
        _ffn_body, grid=(n // tm,),
        in_specs=[row] + _ffn_weight_specs(),
        out_specs=row,
        out_shape=jax.ShapeDtypeStruct((n, D_MODEL), F32),
        compiler_params=_params(("parallel",), 56),
        name="ffn",
    )(x, g, wg, wu, wd)


def _out_ffn_call(x, ys, og, of, wo, g, wg, wu, wd, tm):
    n = x.shape[0]
    tm = _row_tile(n, tm)
    row = lambda w: pl.BlockSpec((tm, w), lambda i: (i, 0))
    return pl.pallas_call(
        _out_ffn_body, grid=(n // tm,),
        in_specs=[row(D_MODEL), row(SSD_WIDTH), row(GLA_WIDTH), row(FOX_WIDTH),
                  _const_spec((D_MODEL, D_MODEL), single=True)] + _ffn_weight_specs(),
        out_specs=row(D_MODEL),
        out_shape=jax.ShapeDtypeStruct((n, D_MODEL), F32),
        compiler_params=_params(("parallel",), 56),
        name="out_ffn",
    )(x, ys, og, of, wo, g, wg, wu, wd)


def _ssd_body(x_ref, nm_ref, w_ref, cw_ref, cb_ref, dtb_ref, alog_ref, dexp_ref, nw_ref,
              cprev_ref, h0_ref, y_ref, cnew_ref, hnew_ref, xbuf, hst, *, tt):
    t = pl.program_id(1)

    @pl.when(t == 0)
    def _():
        xbuf[0:SUBLANES, :] = cprev_ref[...]
        hst[...] = h0_ref[...]

    h = _rms(x_ref[...], nm_ref[...]).astype(BF16)
    u = _dot(h, w_ref[...])
    z = u[:, 0:SSD_WIDTH]
    xbc = u[:, SSD_WIDTH:SSD_WIDTH + SSD_CONV_CH]
    dt_raw = u[:, SSD_WIDTH + SSD_CONV_CH:]

    xbuf[SUBLANES:SUBLANES + tt, :] = xbc
    conv = xbuf[SUBLANES - 3:SUBLANES - 3 + tt, :] * cw_ref[0:1, :]
    conv = conv + xbuf[SUBLANES - 2:SUBLANES - 2 + tt, :] * cw_ref[1:2, :]
    conv = conv + xbuf[SUBLANES - 1:SUBLANES - 1 + tt, :] * cw_ref[2:3, :]
    conv = conv + xbc * cw_ref[3:4, :]
    xa = _silu(conv + cb_ref[...])
    xbuf[0:SUBLANES, :] = xbc[tt - SUBLANES:, :]

    xs = xa[:, 0:SSD_WIDTH]
    bm = xa[:, SSD_WIDTH:SSD_WIDTH + SSD_GROUPS * SSD_STATE]
    cm = xa[:, SSD_WIDTH + SSD_GROUPS * SSD_STATE:]
    dt = _softplus(dt_raw + dtb_ref[...])
    a = dt * (-jnp.exp(alog_ref[...]))

    r, c = _tri_mask(tt)
    causal = r >= c
    acum = _dot_sel_lhs(causal.astype(BF16), a)
    acum_t = acum.T
    alast = acum[tt - 1:tt, :]
    ea = jnp.exp(acum)
    wend = jnp.exp(alast - acum) * dt

    heads_per_group = SSD_HEADS // SSD_GROUPS
    gw = heads_per_group * SSD_HEAD_DIM
    y_parts = []
    for g in range(SSD_GROUPS):
        heads = list(range(g * heads_per_group, (g + 1) * heads_per_group))
        bg = bm[:, g * SSD_STATE:(g + 1) * SSD_STATE].astype(BF16)
        cg = cm[:, g * SSD_STATE:(g + 1) * SSD_STATE].astype(BF16)
        cb = _dot_nt(cg, bg)
        xg = xs[:, g * gw:(g + 1) * gw]
        xdt = (xg * _expand_heads(dt, heads, tt)).astype(BF16)
        intra = []
        for pi in range(heads_per_group // 2):
            ys = []
            for e in range(2):
                hh = heads[2 * pi + e]
                seg = acum[:, hh:hh + 1] - acum_t[hh:hh + 1, :]
                m = (cb * jnp.where(causal, jnp.exp(seg), 0.0)).astype(BF16)
                ys.append(_dot(m, xdt[:, pi * LANES:(pi + 1) * LANES]))
            intra.append(_lane_pair_select(ys[0], ys[1]))
        y_intra = jnp.concatenate(intra, axis=1)

        hg = hst[g * gw:(g + 1) * gw, :]
        y_inter = _dot_nt(cg, hg.astype(BF16)) * _expand_heads(ea, heads, tt)
        y_parts.append(y_intra + y_inter)

        xw = (xg * _expand_heads(wend, heads, tt)).astype(BF16)
        upd = _dot_tn(xw, bg)
        cd = jnp.concatenate(
            [jnp.broadcast_to(jnp.exp(alast[:, hh:hh + 1]), (SSD_HEAD_DIM, SSD_STATE)) for hh in heads],
            axis=0)
        hst[g * gw:(g + 1) * gw, :] = hg * cd + upd

    y = jnp.concatenate(y_parts, axis=1) + dexp_ref[...] * xs
    y = y * _silu(z)
    nw = nw_ref[...]
    outs = []
    for g in range(SSD_GROUPS):
        outs.append(_rms(y[:, g * gw:(g + 1) * gw], nw[:, g * gw:(g + 1) * gw]))
    y_ref[...] = jnp.concatenate(outs, axis=1).astype(y_ref.dtype)

    @pl.when(t == pl.num_programs(1) - 1)
    def _():
        cnew_ref[...] = xbuf[0:SUBLANES, :]
        hnew_ref[...] = hst[...]


def _ssd_call(x, nm, w, cw, cb, dtb, alog, dexp, nw, cprev, h0, tt):
    bsz, t, _ = x.shape
    tt = _row_tile(t, tt)
    bt = lambda w_: pl.BlockSpec((None, tt, w_), lambda b, i: (b, i, 0))
    per_b = lambda r_, w_: pl.BlockSpec((None, r_, w_), lambda b, i: (b, 0, 0))
    return pl.pallas_call(
        functools.partial(_ssd_body, tt=tt), grid=(bsz, t // tt),
        in_specs=[bt(D_MODEL), _const_spec((1, D_MODEL)), _const_spec((D_MODEL, SSD_COLS), single=True),
                  _const_spec((SSD_CONV, SSD_CONV_CH)), _const_spec((1, SSD_CONV_CH)),
                  _const_spec((1, LANES)), _const_spec((1, LANES)),
                  _const_spec((1, SSD_WIDTH)), _const_spec((1, SSD_WIDTH)),
                  per_b(SUBLANES, SSD_CONV_CH), per_b(SSD_WIDTH, SSD_STATE)],
        out_specs=[bt(SSD_WIDTH), per_b(SUBLANES, SSD_CONV_CH), per_b(SSD_WIDTH, SSD_STATE)],
        out_shape=[jax.ShapeDtypeStruct((bsz, t, SSD_WIDTH), BF16),
                   jax.ShapeDtypeStruct((bsz, SUBLANES, SSD_CONV_CH), F32),
                   jax.ShapeDtypeStruct((bsz, SSD_WIDTH, SSD_STATE), F32)],
        scratch_shapes=[pltpu.VMEM((tt + SUBLANES, SSD_CONV_CH), F32),
                        pltpu.VMEM((SSD_WIDTH, SSD_STATE), F32)],
        compiler_params=_params(("parallel", "arbitrary"), 40),
        name="ssd",
    )(x, nm, w, cw, cb, dtb, alog, dexp, nw, cprev, h0)


def _gla_body(x_ref, nm_ref, w_ref, wgate_ref, bgate_ref, gn_ref, s0_ref, o_ref, snew_ref, st, *, tt):
    t = pl.program_id(1)

    @pl.when(t == 0)
    def _():
        st[...] = s0_ref[...]

    hk = GLA_HEADS * GLA_DK
    h = _rms(x_ref[...], nm_ref[...]).astype(BF16)
    u = _dot(h, w_ref[...])
    q = u[:, 0:hk] * (GLA_DK ** -0.5)
    k = u[:, hk:2 * hk]
    v = u[:, 2 * hk:2 * hk + GLA_WIDTH].astype(BF16)
    gg = u[:, 2 * hk + GLA_WIDTH:2 * hk + 2 * GLA_WIDTH]
    gr = u[:, 2 * hk + 2 * GLA_WIDTH:].astype(BF16)
    la = _log_sigmoid(_dot(gr, wgate_ref[...]) + bgate_ref[...]) / GLA_TAU

    r, c = _tri_mask(tt)
    same = (r // GLA_CHUNK) == (c // GLA_CHUNK)
    bcum = _dot_sel_lhs((same & (r >= c)).astype(BF16), la)
    blast = jnp.concatenate(
        [jnp.broadcast_to(bcum[(ci + 1) * GLA_CHUNK - 1:(ci + 1) * GLA_CHUNK, :], (GLA_CHUNK, hk))
         for ci in range(tt // GLA_CHUNK)], axis=0)
    qd = (q * jnp.exp(bcum)).astype(BF16)
    kd = (k * jnp.exp(-bcum)).astype(BF16)
    kend = (k * jnp.exp(blast - bcum)).astype(BF16)
    eblast = jnp.exp(blast)

    lane_k = lax.broadcasted_iota(jnp.int32, (GLA_CHUNK, hk), 1) // GLA_DK
    lane_v = lax.broadcasted_iota(jnp.int32, (GLA_CHUNK, GLA_WIDTH), 1) // GLA_DV
    ar, ac = _tri_mask(GLA_HEADS * GLA_CHUNK, GLA_CHUNK)
    att_causal = (ar % GLA_CHUNK) >= ac
    sr, sc = _tri_mask(GLA_WIDTH, hk)
    diag = (sr // GLA_DV) == (sc // GLA_DK)

    outs = []
    for ci in range(tt // GLA_CHUNK):
        sl = slice(ci * GLA_CHUNK, (ci + 1) * GLA_CHUNK)
        qd_c, kd_c, kend_c, v_c = qd[sl], kd[sl], kend[sl], v[sl]
        lhs = jnp.concatenate([jnp.where(lane_k == hh, qd_c, jnp.zeros_like(qd_c))
                               for hh in range(GLA_HEADS)], axis=0)
        att = jnp.where(att_causal, _dot_nt(lhs, kd_c), 0.0).astype(BF16)
        res = _dot(att, v_c)
        o_intra = jnp.zeros((GLA_CHUNK, GLA_WIDTH), F32)
        for hh in range(GLA_HEADS):
            o_intra = jnp.where(lane_v == hh, res[hh * GLA_CHUNK:(hh + 1) * GLA_CHUNK], o_intra)
        s_prev = st[...]
        o_inter = _dot_nt(qd_c, s_prev.astype(BF16))
        outs.append(o_intra + o_inter)
        upd = _dot_tn(v_c, kend_c)
        st[...] = s_prev * eblast[ci * GLA_CHUNK:ci * GLA_CHUNK + 1, :] + jnp.where(diag, upd, 0.0)

    o = jnp.concatenate(outs, axis=0) if len(outs) > 1 else outs[0]
    ms = _dot_sel_rhs(o * o, _group_mean_matrix(GLA_WIDTH, GLA_DV), parts=2)
    o = o * lax.rsqrt(ms + EPS) * gn_ref[...]
    o_ref[...] = (o * _silu(gg)).astype(o_ref.dtype)

    @pl.when(t == pl.num_programs(1) - 1)
    def _():
        snew_ref[...] = st[...]


def _gla_call(x, nm, w, wgate, bgate, gn, s0, tt):
    bsz, t, _ = x.shape
    tt = _row_tile(t, tt)
    hk = GLA_HEADS * GLA_DK
    bt = lambda w_: pl.BlockSpec((None, tt, w_), lambda b, i: (b, i, 0))
    per_b = pl.BlockSpec((None, GLA_WIDTH, hk), lambda b, i: (b, 0, 0))
    return pl.pallas_call(
        functools.partial(_gla_body, tt=tt), grid=(bsz, t // tt),
        in_specs=[bt(D_MODEL), _const_spec((1, D_MODEL)), _const_spec((D_MODEL, GLA_COLS), single=True),
                  _const_spec((LANES, hk)), _const_spec((1, hk)), _const_spec((1, GLA_WIDTH)), per_b],
        out_specs=[bt(GLA_WIDTH), per_b],
        out_shape=[jax.ShapeDtypeStruct((bsz, t, GLA_WIDTH), BF16),
                   jax.ShapeDtypeStruct((bsz, GLA_WIDTH, hk), F32)],
        scratch_shapes=[pltpu.VMEM((GLA_WIDTH, hk), F32)],
        compiler_params=_params(("parallel", "arbitrary"), 32),
        name="gla",
    )(x, nm, w, wgate, bgate, gn, s0)


def _fox_proj_body(x_ref, nm_ref, w_ref, qn_ref, kn_ref, fb_ref, pq_ref, pk_ref, arow_ref,
                   k_ref, v_ref, lf_ref, *rest, tt, attn_layout):
    if attn_layout:
        qt_ref, kp_ref, vt_ref, carry = rest
    else:
        qs_ref, carry = rest
    t = pl.program_id(1)

    @pl.when(t == 0)
    def _():
        carry[...] = jnp.zeros_like(carry)

    h = _rms(x_ref[...], nm_ref[...]).astype(BF16)
    u = _dot(h, w_ref[...])
    fq = u[:, 0:FOX_WIDTH]
    fk = u[:, FOX_WIDTH:2 * FOX_WIDTH]
    fv = u[:, 2 * FOX_WIDTH:3 * FOX_WIDTH]
    ff = u[:, 3 * FOX_WIDTH:]

    gmat = _group_mean_matrix(FOX_WIDTH, FOX_HEAD_DIM)
    qn = fq * lax.rsqrt(_dot_sel_rhs(fq * fq, gmat, parts=2) + EPS) * qn_ref[...]
    kn = fk * lax.rsqrt(_dot_sel_rhs(fk * fk, gmat, parts=2) + EPS) * kn_ref[...]
    lf = _log_sigmoid(ff + fb_ref[...])
    k_ref[...] = kn
    v_ref[...] = fv
    lf_ref[...] = lf
    qs = (qn * (FOX_HEAD_DIM ** -0.5)).astype(BF16)

    if not attn_layout:
        qs_ref[...] = qs
        return

    r, c = _tri_mask(tt)
    cum = carry[...] + _dot_sel_lhs((r >= c).astype(BF16), lf)
    carry[...] = cum[tt - 1:tt, :]
    c3 = jnp.concatenate(_split_bf16(cum, AUX_PARTS), axis=1)
    aux_k = (_dot(c3, pk_ref[...]) + arow_ref[0:1, :]).astype(BF16)
    knb = kn.astype(BF16)
    fv_t = fv.T
    ones = jnp.ones((FOX_VT_ROWS - FOX_HEAD_DIM, tt), F32)
    lane = lax.broadcasted_iota(jnp.int32, (tt, LANES), 1)
    for p in range(FOX_HEADS // 2):
        sl = slice(p * LANES, (p + 1) * LANES)
        kp_ref[p, :, 0:LANES] = knb[:, sl]
        kp_ref[p, :, LANES:] = aux_k
        for e in range(2):
            hh = 2 * p + e
            vt = jnp.concatenate([fv_t[hh * FOX_HEAD_DIM:(hh + 1) * FOX_HEAD_DIM, :], ones], axis=0).astype(BF16)
            for j in range(tt // FOX_KB):
                vt_ref[hh, j] = vt[:, j * FOX_KB:(j + 1) * FOX_KB]
            qmask = jnp.where((lane // FOX_HEAD_DIM) == e, qs[:, sl], jnp.zeros_like(qs[:, sl])).astype(F32)
            aux_q = _dot(c3, pq_ref[hh]) + arow_ref[1 + hh:2 + hh, :]
            q_t = jnp.concatenate([qmask.T, aux_q.T], axis=0).astype(BF16)
            for j in range(tt // FOX_QB):
                qt_ref[hh, j] = q_t[:, j * FOX_QB:(j + 1) * FOX_QB]


def _fox_aux_constants():
    nslot = FOX_HEADS * AUX_PARTS
    pq = np.zeros((FOX_HEADS, AUX_PARTS * LANES, LANES), np.float32)
    pk = np.zeros((AUX_PARTS * LANES, LANES), np.float32)
    arow = np.zeros((SUBLANES, LANES), np.float32)
    for hh in range(FOX_HEADS):
        for j in range(AUX_PARTS):
            pq[hh, j * LANES + hh, AUX_PARTS * hh + j] = 1.0
            pk[j * LANES + hh, nslot + AUX_PARTS * hh + j] = -1.0
            arow[0, AUX_PARTS * hh + j] = 1.0
            arow[1 + hh, nslot + AUX_PARTS * hh + j] = 1.0
    return jnp.asarray(pq, BF16), jnp.asarray(pk, BF16), jnp.asarray(arow, F32)


def _fox_proj_call(x, nm, w, qn, kn, fb, tt, attn_layout):
    bsz, t, _ = x.shape
    tt = _row_tile(t, tt)
    pq, pk, arow = _fox_aux_constants()
    bt = lambda w_: pl.BlockSpec((None, tt, w_), lambda b, i: (b, i, 0))
    bht = lambda n_, w_: pl.BlockSpec((None, n_, tt, w_), lambda b, i: (b, 0, i, 0))
    out_specs = [bt(FOX_WIDTH), bt(FOX_WIDTH), bt(LANES)]
    out_shape = [jax.ShapeDtypeStruct((bsz, t, FOX_WIDTH), F32),
                 jax.ShapeDtypeStruct((bsz, t, FOX_WIDTH), F32),
                 jax.ShapeDtypeStruct((bsz, t, LANES), F32)]
    if attn_layout:
        tiled = lambda n_, r_, w_: pl.BlockSpec((None, FOX_HEADS, tt // n_, r_, w_), lambda b, i: (b, 0, i, 0, 0))
        out_specs += [tiled(FOX_QB, 2 * LANES, FOX_QB), bht(FOX_HEADS // 2, 2 * LANES),
                      tiled(FOX_KB, FOX_VT_ROWS, FOX_KB)]
        out_shape += [jax.ShapeDtypeStruct((bsz, FOX_HEADS, t // FOX_QB, 2 * LANES, FOX_QB), BF16),
                      jax.ShapeDtypeStruct((bsz, FOX_HEADS // 2, t, 2 * LANES), BF16),
                      jax.ShapeDtypeStruct((bsz, FOX_HEADS, t // FOX_KB, FOX_VT_ROWS, FOX_KB), BF16)]
    else:
        out_specs += [bt(FOX_WIDTH)]
        out_shape += [jax.ShapeDtypeStruct((bsz, t, FOX_WIDTH), BF16)]
    return pl.pallas_call(
        functools.partial(_fox_proj_body, tt=tt, attn_layout=attn_layout), grid=(bsz, t // tt),
        in_specs=[bt(D_MODEL), _const_spec((1, D_MODEL)), _const_spec((D_MODEL, FOX_COLS), single=True),
                  _const_spec((1, FOX_WIDTH)), _const_spec((1, FOX_WIDTH)), _const_spec((1, LANES)),
                  _const_spec(pq.shape), _const_spec(pk.shape), _const_spec(arow.shape)],
        out_specs=out_specs, out_shape=out_shape,
        scratch_shapes=[pltpu.VMEM((1, LANES), F32)],
        compiler_params=_params(("parallel", "arbitrary"), 32),
        name="fox_proj",
    )(x, nm, w, qn, kn, fb, pq, pk, arow)


def _fox_prompt_body(qt_ref, k_ref, vt_ref, o_ref, m_s, acc_s, s_s, *, nsub):
    qi = pl.program_id(2)
    qtile = nsub * FOX_QB
    chains = [(si, e) for si in range(nsub) for e in range(2)]
    m_s[...] = jnp.full_like(m_s, -jnp.inf)
    acc_s[...] = jnp.zeros_like(acc_s)

    nch = len(chains)

    def scores(c, k0, nk, masked):
        si, e = chains[c]
        s = _dot(k_ref[pl.ds(k0, nk), :], qt_ref[e, si])
        if masked:
            r, col = _tri_mask(FOX_QB)
            tail = jnp.where(r <= col, s[nk - FOX_QB:], -jnp.inf)
            s = tail if nk == FOX_QB else jnp.concatenate([s[:nk - FOX_QB], tail], axis=0)
        return s

    def absorb(c, s, vt0):
        e = chains[c][1]
        m_old = m_s[c]
        m_new = jnp.maximum(m_old, jnp.max(s, axis=0, keepdims=True))
        alpha = jnp.exp(m_old - m_new)
        p = jnp.exp(s - m_new).astype(BF16)
        vts = [vt_ref[e, vt0 + j] for j in range(s.shape[0] // FOX_KB)]
        vt = vts[0] if len(vts) == 1 else jnp.concatenate(vts, axis=1)
        acc_s[c] = alpha * acc_s[c] + _dot(vt, p)
        m_s[c] = m_new

    def sweep(k0, vt0, nks, masked):
        for c in range(nch):
            s_s[c, 0:nks[c], :] = scores(c, k0, nks[c], masked)
        for c in range(nch):
            absorb(c, s_s[c, 0:nks[c], :], vt0)

    def below_diagonal(i, carry):
        sweep(pl.multiple_of(i * FOX_SWEEP, FOX_SWEEP), i * (FOX_SWEEP // FOX_KB), [FOX_SWEEP] * nch, False)
        return carry

    lax.fori_loop(0, qi * (qtile // FOX_SWEEP), below_diagonal, 0)
    sweep(pl.multiple_of(qi * qtile, qtile), qi * (qtile // FOX_KB),
          [(si + 1) * FOX_QB for si, _ in chains], True)

    for si in range(nsub):
        outs = []
        for e in range(2):
            acc = acc_s[chains.index((si, e))]
            outs.append(acc[0:FOX_HEAD_DIM] / acc[FOX_HEAD_DIM:FOX_HEAD_DIM + 1])
        o = jnp.concatenate(outs, axis=0).T
        o_ref[si * FOX_QB:(si + 1) * FOX_QB, :] = o.astype(o_ref.dtype)


def _fox_prompt_call(qt, kp, vt, tq):
    bsz, _, nqb, _, _ = qt.shape
    t = kp.shape[2]
    nsub = _row_tile(nqb, max(FOX_SWEEP // FOX_QB, tq // FOX_QB))
    npair = FOX_HEADS // 2
    nch = 2 * nsub
    return pl.pallas_call(
        functools.partial(_fox_prompt_body, nsub=nsub), grid=(bsz, npair, nqb // nsub),
        in_specs=[pl.BlockSpec((None, 2, nsub, 2 * LANES, FOX_QB), lambda b, p, i: (b, p, i, 0, 0)),
                  pl.BlockSpec((None, None, t, 2 * LANES), lambda b, p, i: (b, p, 0, 0)),
                  pl.BlockSpec((None, 2, t // FOX_KB, FOX_VT_ROWS, FOX_KB), lambda b, p, i: (b, p, 0, 0, 0))],
        out_specs=pl.BlockSpec((None, nsub * FOX_QB, LANES), lambda b, p, i: (b, i, p)),
        out_shape=jax.ShapeDtypeStruct((bsz, t, FOX_WIDTH), BF16),
        scratch_shapes=[pltpu.VMEM((nch, 1, FOX_QB), F32),
                        pltpu.VMEM((nch, FOX_VT_ROWS, FOX_QB), F32),
                        pltpu.VMEM((nch, nsub * FOX_QB, FOX_QB), F32)],
        compiler_params=_params(("parallel", "parallel", "arbitrary"), 40),
        name="fox_attn_prompt",
    )(qt, kp, vt)


def _fox_sample_body(q_ref, kn_ref, vn_ref, lfn_ref, kc_ref, vc_ref, lfp_ref, o_ref, *, tn, past, seg):
    r, c = _tri_mask(seg)
    upper = (r <= c).astype(BF16)
    carry = jnp.zeros((FOX_HEADS, 1), F32)
    cps = []
    for j in range(past // seg):
        cs = carry + _dot_sel_rhs(lfp_ref[:, j * seg:(j + 1) * seg], upper)
        cps.append(cs)
        carry = cs[:, seg - 1:seg]
    cp = jnp.concatenate(cps, axis=1) if len(cps) > 1 else cps[0]
    lfn = lfn_ref[...]
    if tn < LANES:
        lfn = jnp.concatenate([lfn, jnp.zeros((LANES - tn, LANES), F32)], axis=0)
    r, c = _tri_mask(LANES)
    cn = _dot_sel_lhs((r >= c).astype(BF16), lfn)
    cn_t = cn.T

    q = q_ref[...]
    kn = kn_ref[...].astype(BF16)
    vn = vn_ref[...].astype(BF16)
    r, c = _tri_mask(tn)
    outs = []
    for hh in range(FOX_HEADS):
        hs = slice(hh * FOX_HEAD_DIM, (hh + 1) * FOX_HEAD_DIM)
        qh = q[:, hs]
        cq = cn[0:tn, hh:hh + 1]
        s_past = _dot(qh, kc_ref[hh].astype(BF16)) + ((carry[hh:hh + 1, :] + cq) - cp[hh:hh + 1, :])
        s_new = _dot_nt(qh, kn[:, hs]) + (cq - cn_t[hh:hh + 1, 0:tn])
        s_new = jnp.where(r >= c, s_new, -jnp.inf)
        m = jnp.maximum(jnp.max(s_past, axis=-1, keepdims=True), jnp.max(s_new, axis=-1, keepdims=True))
        p_past = jnp.exp(s_past - m)
        p_new = jnp.exp(s_new - m)
        denom = jnp.sum(p_past, axis=-1, keepdims=True) + jnp.sum(p_new, axis=-1, keepdims=True)
        o = _dot_nt(p_past.astype(BF16), vc_ref[hh].astype(BF16)) + _dot(p_new.astype(BF16), vn[:, hs])
        outs.append(o / denom)
    o_ref[...] = jnp.concatenate(outs, axis=1).astype(o_ref.dtype)


def _fox_sample_call(q, kn, vn, lfn, cache_k, cache_v, lfp_t, layer):
    bsz, tn, _ = q.shape
    past = cache_k.shape[-1]
    seg = _row_tile(past, 512)
    bt = lambda w_: pl.BlockSpec((None, tn, w_), lambda b: (b, 0, 0))
    cache = pl.BlockSpec((None, None, FOX_HEADS, FOX_HEAD_DIM, past), lambda b: (layer, b, 0, 0, 0))
    return pl.pallas_call(
        functools.partial(_fox_sample_body, tn=tn, past=past, seg=seg), grid=(bsz,),
        in_specs=[bt(FOX_WIDTH), bt(FOX_WIDTH), bt(FOX_WIDTH), bt(LANES), cache, cache,
                  pl.BlockSpec((None, None, FOX_HEADS, past), lambda b: (layer, b, 0, 0))],
        out_specs=bt(FOX_WIDTH),
        out_shape=jax.ShapeDtypeStruct((bsz, tn, FOX_WIDTH), BF16),
        compiler_params=_params(("parallel",), 48),
        name="fox_attn_sample",
    )(q, kn, vn, lfn, cache_k, cache_v, lfp_t)


def _pad_cols(a, width):
    return jnp.pad(a, ((0, 0), (0, width - a.shape[1])))


def _layer_params(l, norm_ffn1, w1_gate, w1_up, w1_down, norm_mix, w_in, ssd_conv_w, ssd_conv_b,
                  ssd_dt_bias, ssd_a_log, ssd_d, ssd_norm, gla_w_gate, gla_b_gate, gla_norm,
                  fox_q_norm, fox_k_norm, fox_f_bias, w_out, norm_ffn2, w2_gate, w2_up, w2_down):
    wi = w_in[l]
    row = lambda a: a.reshape(1, -1).astype(F32)
    hk = GLA_HEADS * GLA_DK
    p = dict(
        ffn1=(row(norm_ffn1[l]), w1_gate[l].astype(BF16), w1_up[l].astype(BF16), w1_down[l].astype(BF16)),
        ffn2=(row(norm_ffn2[l]), w2_gate[l].astype(BF16), w2_up[l].astype(BF16), w2_down[l].astype(BF16)),
        norm_mix=row(norm_mix[l]),
        w_out=w_out[l].astype(BF16),
        w_ssd=jnp.concatenate([wi[:, _OFF_Z:_OFF_DT], _pad_cols(wi[:, _OFF_DT:_OFF_GQ], LANES)], axis=1).astype(BF16),
        conv_w=ssd_conv_w[l].astype(F32),
        conv_b=row(ssd_conv_b[l]),
        dt_bias=_pad_cols(row(ssd_dt_bias[l]), LANES),
        a_log=_pad_cols(row(ssd_a_log[l]), LANES),
        d_exp=row(jnp.repeat(ssd_d[l], SSD_HEAD_DIM)),
        ssd_norm=row(ssd_norm[l]),
        w_gla=jnp.concatenate([wi[:, _OFF_GQ:_OFF_GR], _pad_cols(wi[:, _OFF_GR:_OFF_FQ], LANES)], axis=1).astype(BF16),
        w_gate=jnp.pad(gla_w_gate[l], ((0, LANES - GLA_RANK), (0, 0))).astype(BF16),
        b_gate=row(gla_b_gate[l]),
        gla_norm=row(jnp.tile(gla_norm[l], GLA_HEADS)),
        w_fox=jnp.concatenate([wi[:, _OFF_FQ:_OFF_FF], _pad_cols(wi[:, _OFF_FF:], LANES)], axis=1).astype(BF16),
        q_norm=row(jnp.tile(fox_q_norm[l], FOX_HEADS)),
        k_norm=row(jnp.tile(fox_k_norm[l], FOX_HEADS)),
        f_bias=_pad_cols(row(fox_f_bias[l]), LANES),
    )
    return p


def _gla_state_in(s):
    bsz = s.shape[0]
    eye = jnp.eye(GLA_HEADS, dtype=s.dtype)
    full = jnp.einsum('bhkv,hg->bhvgk', s, eye)
    return full.reshape(bsz, GLA_HEADS * GLA_DV, GLA_HEADS * GLA_DK)


def _gla_state_out(st):
    bsz = st.shape[0]
    full = st.reshape(bsz, GLA_HEADS, GLA_DV, GLA_HEADS, GLA_DK)
    idx = jnp.arange(GLA_HEADS)
    diag = full[:, idx, :, idx, :]
    return jnp.transpose(diag, (1, 0, 3, 2))


def _mix(x, p, conv_prev, ssd_h0, gla_s0, fox_cache, layer, tiles):
    bsz, t, _ = x.shape
    cprev = jnp.pad(conv_prev, ((0, 0), (SUBLANES - (SSD_CONV - 1), 0), (0, 0)))
    y_ssd, cnew, hnew = _ssd_call(x, p['norm_mix'], p['w_ssd'], p['conv_w'], p['conv_b'], p['dt_bias'],
                                  p['a_log'], p['d_exp'], p['ssd_norm'], cprev,
                                  ssd_h0.reshape(bsz, SSD_WIDTH, SSD_STATE), tiles['ssd'])
    o_gla, snew = _gla_call(x, p['norm_mix'], p['w_gla'], p['w_gate'], p['b_gate'], p['gla_norm'],
                            _gla_state_in(gla_s0), tiles['gla'])
    if fox_cache is None:
        k, v, lf, qt, kp, vt = _fox_proj_call(x, p['norm_mix'], p['w_fox'], p['q_norm'], p['k_norm'],
                                              p['f_bias'], tiles['fox_proj'], True)
        o_fox = _fox_prompt_call(qt, kp, vt, tiles['fox_q'])
    else:
        k, v, lf, qs = _fox_proj_call(x, p['norm_mix'], p['w_fox'], p['q_norm'], p['k_norm'],
                                      p['f_bias'], tiles['fox_proj'], False)
        cache_k, cache_v, lfp_t = fox_cache
        o_fox = _fox_sample_call(qs, k, v, lf, cache_k, cache_v, lfp_t, layer)
    state = (cnew[:, SUBLANES - (SSD_CONV - 1):, :],
             hnew.reshape(bsz, SSD_HEADS, SSD_HEAD_DIM, SSD_STATE),
             _gla_state_out(snew),
             k.reshape(bsz, t, FOX_HEADS, FOX_HEAD_DIM),
             v.reshape(bsz, t, FOX_HEADS, FOX_HEAD_DIM),
             lf[:, :, 0:FOX_HEADS])
    return (y_ssd, o_gla, o_fox), state


def _trunk_layer(x, p, conv_prev, ssd_h0, gla_s0, fox_cache, layer, tiles):
    bsz, t, d = x.shape
    x1 = _ffn_call(x.reshape(bsz * t, d), *p['ffn1'], tiles['ffn']).reshape(bsz, t, d)
    (y_ssd, o_gla, o_fox), state = _mix(x1, p, conv_prev, ssd_h0, gla_s0, fox_cache, layer, tiles)
    flat = lambda a: a.reshape(bsz * t, a.shape[-1])
    x3 = _out_ffn_call(flat(x1), flat(y_ssd), flat(o_gla), flat(o_fox), p['w_out'], *p['ffn2'], tiles['ffn'])
    return x3.reshape(bsz, t, d), state


PROMPT_TILES = dict(ffn=512, ssd=256, gla=256, fox_proj=512, fox_q=1024)
SAMPLE_TILES = dict(ffn=512, ssd=64, gla=64, fox_proj=64, fox_q=64)


def kernel(x_prompt, x_sample, state_ssd_conv, state_ssd, state_gla, cache_fox_k, cache_fox_v, cache_fox_logf, norm_ffn1, w1_gate, w1_up, w1_down, norm_mix, w_in, ssd_conv_w, ssd_conv_b, ssd_dt_bias, ssd_a_log, ssd_d, ssd_norm, gla_w_gate, gla_b_gate, gla_norm, fox_q_norm, fox_k_norm, fox_f_bias, w_out, norm_ffn2, w2_gate, w2_up, w2_down):
    depth = w_in.shape[0]
    bp = x_prompt.shape[0]
    weights = (norm_ffn1, w1_gate, w1_up, w1_down, norm_mix, w_in, ssd_conv_w, ssd_conv_b, ssd_dt_bias,
               ssd_a_log, ssd_d, ssd_norm, gla_w_gate, gla_b_gate, gla_norm, fox_q_norm, fox_k_norm,
               fox_f_bias, w_out, norm_ffn2, w2_gate, w2_up, w2_down)
    cache_k = jnp.transpose(cache_fox_k, (0, 1, 3, 4, 2))
    cache_v = jnp.transpose(cache_fox_v, (0, 1, 3, 4, 2))
    lfp_t = jnp.swapaxes(cache_fox_logf, 2, 3)
    zeros_conv = jnp.zeros((bp, SSD_CONV - 1, SSD_CONV_CH), F32)
    zeros_ssd = jnp.zeros((bp, SSD_HEADS, SSD_HEAD_DIM, SSD_STATE), F32)
    zeros_gla = jnp.zeros((bp, GLA_HEADS, GLA_DK, GLA_DV), F32)

    xp, xs = x_prompt, x_sample
    p_new = [[] for _ in range(6)]
    s_new = [[] for _ in range(6)]
    for l in range(depth):
        p = _layer_params(l, *weights)
        xp, st_p = _trunk_layer(xp, p, zeros_conv, zeros_ssd, zeros_gla, None, l, PROMPT_TILES)
        xs, st_s = _trunk_layer(xs, p, state_ssd_conv[l], state_ssd[l], state_gla[l],
                                (cache_k, cache_v, lfp_t), l, SAMPLE_TILES)
        for i in range(6):
            p_new[i].append(st_p[i])
            s_new[i].append(st_s[i])
    outs_p = [jnp.stack(a) for a in p_new]
    outs_s = [jnp.stack(a) for a in s_new]
    return (xp, xs, *outs_p, *outs_s)
```

```python
import functools

import numpy as np
import jax
import jax.numpy as jnp
from jax import lax
from jax.experimental import pallas as pl
from jax.experimental.pallas import tpu as pltpu

F32 = jnp.float32
BF16 = jnp.bfloat16

EPS = 1e-6
D_MODEL = 1024
D_FF = 2816
SSD_HEADS = 8
SSD_HEAD_DIM = 64
SSD_WIDTH = 512
SSD_GROUPS = 2
SSD_STATE = 128
SSD_CONV = 4
SSD_CONV_CH = 1024
GLA_HEADS = 4
GLA_DK = 32
GLA_DV = 64
GLA_WIDTH = 256
GLA_RANK = 16
GLA_TAU = 16.0
GLA_CHUNK = 64
FOX_HEADS = 4
FOX_HEAD_DIM = 64
FOX_WIDTH = 256
LANES = 128
SUBLANES = 8
MIB = 1024 * 1024

_OFF_Z, _OFF_XBC, _OFF_DT = 0, 512, 1536
_OFF_GQ, _OFF_GR = 1544, 2312
_OFF_FQ, _OFF_FF = 2328, 3096
SSD_COLS = SSD_WIDTH + SSD_CONV_CH + LANES
GLA_COLS = 2 * GLA_HEADS * GLA_DK + 2 * GLA_WIDTH + LANES
FOX_COLS = 3 * FOX_WIDTH + LANES
AUX_PARTS = 3
FOX_QB = 256
FOX_KB = 256
FOX_SWEEP = 512
FOX_VT_ROWS = FOX_HEAD_DIM + 16


def _dot(a, b):
    return jnp.dot(a, b, preferred_element_type=F32)


def _dot_nt(a, b):
    return lax.dot_general(a, b, (((1,), (1,)), ((), ())), preferred_element_type=F32)


def _dot_tn(a, b):
    return lax.dot_general(a, b, (((0,), (0,)), ((), ())), preferred_element_type=F32)


def _split_bf16(x, parts):
    out = []
    r = x
    for i in range(parts):
        p = r.astype(BF16)
        out.append(p)
        if i + 1 < parts:
            r = r - p.astype(F32)
    return out


def _dot_sel_lhs(sel, x, parts=3):
    n = x.shape[1]
    t = _dot(sel, jnp.concatenate(_split_bf16(x, parts), axis=1))
    acc = t[:, 0:n]
    for i in range(1, parts):
        acc = acc + t[:, i * n:(i + 1) * n]
    return acc


def _dot_sel_rhs(x, sel, parts=3):
    m = x.shape[0]
    pieces = _split_bf16(x, parts)
    if m % (2 * SUBLANES):
        acc = _dot(pieces[0], sel)
        for p in pieces[1:]:
            acc = acc + _dot(p, sel)
        return acc
    t = _dot(jnp.concatenate(pieces, axis=0), sel)
    acc = t[0:m]
    for i in range(1, parts):
        acc = acc + t[i * m:(i + 1) * m]
    return acc


def _rms(x, w):
    ms = jnp.mean(x * x, axis=-1, keepdims=True)
    return x * lax.rsqrt(ms + EPS) * w


def _silu(x):
    return x * jax.nn.sigmoid(x)


def _softplus(x):
    return jnp.maximum(x, 0.0) + jnp.log1p(jnp.exp(-jnp.abs(x)))


def _log_sigmoid(x):
    return -_softplus(-x)


def _tri_mask(n, m=None):
    m = n if m is None else m
    r = lax.broadcasted_iota(jnp.int32, (n, m), 0)
    c = lax.broadcasted_iota(jnp.int32, (n, m), 1)
    return r, c


def _lane_pair_select(a_even, a_odd):
    lane = lax.broadcasted_iota(jnp.int32, a_even.shape, 1)
    return jnp.where(lane < 64, a_even, a_odd)


def _expand_heads(v, heads, rows):
    pieces = []
    for i in range(0, len(heads), 2):
        a = jnp.broadcast_to(v[:, heads[i]:heads[i] + 1], (rows, LANES))
        b = jnp.broadcast_to(v[:, heads[i + 1]:heads[i + 1] + 1], (rows, LANES))
        pieces.append(_lane_pair_select(a, b))
    return pieces[0] if len(pieces) == 1 else jnp.concatenate(pieces, axis=1)


def _group_mean_matrix(width, group):
    r, c = _tri_mask(width)
    return jnp.where((r // group) == (c // group), 1.0 / group, 0.0).astype(BF16)


def _params(sem, vmem_mib):
    return pltpu.CompilerParams(dimension_semantics=sem, vmem_limit_bytes=vmem_mib * MIB)


def _const_spec(shape, single=False):
    nd = len(shape)
    if single:
        return pl.BlockSpec(shape, lambda *_: (0,) * nd, pipeline_mode=pl.Buffered(1))
    return pl.BlockSpec(shape, lambda *_: (0,) * nd)


def _swiglu_half(x, g_ref, wg_ref, wu_ref, wd_ref):
    h = _rms(x, g_ref[...]).astype(BF16)
    gate = _dot(h, wg_ref[...])
    up = _dot(h, wu_ref[...])
    a = (_silu(gate) * up).astype(BF16)
    return x + 0.5 * _dot(a, wd_ref[...])


def _ffn_body(x_ref, g_ref, wg_ref, wu_ref, wd_ref, o_ref):
    o_ref[...] = _swiglu_half(x_ref[...], g_ref, wg_ref, wu_ref, wd_ref)


def _out_ffn_body(x_ref, ys_ref, og_ref, of_ref, wo_ref, g_ref, wg_ref, wu_ref, wd_ref, o_ref):
    x = x_ref[...]
    x = x + (_dot(ys_ref[...], wo_ref[0:SSD_WIDTH, :])
             + _dot(og_ref[...], wo_ref[SSD_WIDTH:SSD_WIDTH + GLA_WIDTH, :])
             + _dot(of_ref[...], wo_ref[SSD_WIDTH + GLA_WIDTH:, :]))
    o_ref[...] = _swiglu_half(x, g_ref, wg_ref, wu_ref, wd_ref)


def _ffn_weight_specs():
    return [_const_spec((1, D_MODEL)),
            _const_spec((D_MODEL, D_FF), single=True),
            _const_spec((D_MODEL, D_FF), single=True),
            _const_spec((D_FF, D_MODEL), single=True)]


def _row_tile(n, want):
    t = min(want, n)
    while n % t:
        t //= 2
    return t


def _ffn_call(x, g, wg, wu, wd, tm):
    n = x.shape[0]
    tm = _row_tile(n, tm)
    row = pl.BlockSpec((tm, D_MODEL), lambda i: (i, 0))
    return pl.pallas_call(
        _ffn_body, grid=(n // tm,),
        in_specs=[row] + _ffn_weight_specs(),
        out_specs=row,
        out_shape=jax.ShapeDtypeStruct((n, D_MODEL), F32),
        compiler_params=_params(("parallel",), 56),
        name="ffn",
    )(x, g, wg, wu, wd)


def _out_ffn_call(x, ys, og, of, wo, g, wg, wu, wd, tm):
    n = x.shape[0]
    tm = _row_tile(n, tm)
    row = lambda w: pl.BlockSpec((tm, w), lambda i: (i, 0))
    return pl.pallas_call(
        _out_ffn_body, grid=(n // tm,),
        in_specs=[row(D_MODEL), row(SSD_WIDTH), row(GLA_WIDTH), row(FOX_WIDTH),
                  _const_spec((D_MODEL, D_MODEL), single=True)] + _ffn_weight_specs(),
        out_specs=row(D_MODEL),
        out_shape=jax.ShapeDtypeStruct((n, D_MODEL), F32),
        compiler_params=_params(("parallel",), 56),
        name="out_ffn",
    )(x, ys, og, of, wo, g, wg, wu, wd)


def _ssd_body(x_ref, nm_ref, w_ref, cw_ref, cb_ref, dtb_ref, alog_ref, dexp_ref, nw_ref,
              cprev_ref, h0_ref, y_ref, cnew_ref, hnew_ref, xbuf, hst, *, tt, rows):
    t = pl.program_id(1)

    @pl.when(t == 0)
    def _():
        xbuf[:, 0:SUBLANES, :] = cprev_ref[...]
        hst[...] = h0_ref[...]

    x = x_ref[...].reshape(rows * tt, D_MODEL)
    h = _rms(x, nm_ref[...]).astype(BF16)
    u = _dot(h, w_ref[...])
    z = u[:, 0:SSD_WIDTH]
    xbc = u[:, SSD_WIDTH:SSD_WIDTH + SSD_CONV_CH]
    dt_raw = u[:, SSD_WIDTH + SSD_CONV_CH:]

    convs = []
    for i in range(rows):
        xi = xbc[i * tt:(i + 1) * tt]
        xbuf[i, SUBLANES:SUBLANES + tt, :] = xi
        conv = xbuf[i, SUBLANES - 3:SUBLANES - 3 + tt, :] * cw_ref[0:1, :]
        conv = conv + xbuf[i, SUBLANES - 2:SUBLANES - 2 + tt, :] * cw_ref[1:2, :]
        conv = conv + xbuf[i, SUBLANES - 1:SUBLANES - 1 + tt, :] * cw_ref[2:3, :]
        convs.append(conv + xi * cw_ref[3:4, :])
        xbuf[i, 0:SUBLANES, :] = xi[tt - SUBLANES:, :]
    xa = _silu(jnp.concatenate(convs, axis=0) + cb_ref[...])

    xs = xa[:, 0:SSD_WIDTH]
    bm = xa[:, SSD_WIDTH:SSD_WIDTH + SSD_GROUPS * SSD_STATE].astype(BF16)
    cm = xa[:, SSD_WIDTH + SSD_GROUPS * SSD_STATE:].astype(BF16)
    dt = _softplus(dt_raw + dtb_ref[...])
    a = dt * (-jnp.exp(alog_ref[...]))

    r, c = _tri_mask(tt)
    causal = r >= c
    tri = causal.astype(BF16)
    acum = jnp.concatenate(
        [_dot_sel_lhs(tri, a[i * tt:(i + 1) * tt]) for i in range(rows)], axis=0)
    acum_t = acum.T
    alast = jnp.concatenate(
        [jnp.broadcast_to(acum[(i + 1) * tt - 1:(i + 1) * tt, :], (tt, LANES)) for i in range(rows)], axis=0)
    ea = jnp.exp(acum)
    wend = jnp.exp(alast - acum) * dt

    heads_per_group = SSD_HEADS // SSD_GROUPS
    gw = heads_per_group * SSD_HEAD_DIM
    heads_of = [list(range(g * heads_per_group, (g + 1) * heads_per_group)) for g in range(SSD_GROUPS)]
    nrow = rows * tt
    xdt = jnp.concatenate([xs[:, g * gw:(g + 1) * gw] * _expand_heads(dt, heads_of[g], nrow)
                           for g in range(SSD_GROUPS)], axis=1).astype(BF16)
    xw = jnp.concatenate([xs[:, g * gw:(g + 1) * gw] * _expand_heads(wend, heads_of[g], nrow)
                          for g in range(SSD_GROUPS)], axis=1).astype(BF16)
    ea_x = jnp.concatenate([_expand_heads(ea, heads_of[g], nrow) for g in range(SSD_GROUPS)], axis=1)

    units = [(i, g) for g in range(SSD_GROUPS) for i in range(rows)]
    rs = {i: slice(i * tt, (i + 1) * tt) for i in range(rows)}
    gs = {g: slice(g * SSD_STATE, (g + 1) * SSD_STATE) for g in range(SSD_GROUPS)}
    cbs = {(i, g): _dot_nt(cm[rs[i], gs[g]], bm[rs[i], gs[g]]) for i, g in units}
    upds = {(i, g): _dot_tn(xw[rs[i], g * gw:(g + 1) * gw], bm[rs[i], gs[g]]) for i, g in units}
    y_unit = {}
    for i, g in units:
        heads = heads_of[g]
        intra = []
        for pi in range(heads_per_group // 2):
            ys = []
            for e in range(2):
                hh = heads[2 * pi + e]
                seg = acum[rs[i], hh:hh + 1] - acum_t[hh:hh + 1, rs[i]]
                m = (cbs[(i, g)] * jnp.where(causal, jnp.exp(seg), 0.0)).astype(BF16)
                col = g * gw + pi * LANES
                ys.append(_dot(m, xdt[rs[i], col:col + LANES]))
            intra.append(_lane_pair_select(ys[0], ys[1]))
        hg = hst[i, g * gw:(g + 1) * gw, :]
        y_inter = _dot_nt(cm[rs[i], gs[g]], hg.astype(BF16)) * ea_x[rs[i], g * gw:(g + 1) * gw]
        y_unit[(i, g)] = jnp.concatenate(intra, axis=1) + y_inter
        cd = jnp.concatenate(
            [jnp.broadcast_to(jnp.exp(acum[(i + 1) * tt - 1:(i + 1) * tt, hh:hh + 1]), (SSD_HEAD_DIM, SSD_STATE))
             for hh in heads], axis=0)
        hst[i, g * gw:(g + 1) * gw, :] = hg * cd + upds[(i, g)]

    y = jnp.concatenate(
        [jnp.concatenate([y_unit[(i, g)] for g in range(SSD_GROUPS)], axis=1) for i in range(rows)], axis=0)
    y = (y + dexp_ref[...] * xs) * _silu(z)
    nw = nw_ref[...]
    outs = [_rms(y[:, g * gw:(g + 1) * gw], nw[:, g * gw:(g + 1) * gw]) for g in range(SSD_GROUPS)]
    y_ref[...] = jnp.concatenate(outs, axis=1).astype(y_ref.dtype).reshape(rows, tt, SSD_WIDTH)

    @pl.when(t == pl.num_programs(1) - 1)
    def _():
        cnew_ref[...] = xbuf[:, 0:SUBLANES, :]
        hnew_ref[...] = hst[...]


def _ssd_call(x, nm, w, cw, cb, dtb, alog, dexp, nw, cprev, h0, tt, rows):
    bsz, t, _ = x.shape
    tt = _row_tile(t, tt)
    rows = _row_tile(bsz, rows)
    bt = lambda w_: pl.BlockSpec((rows, tt, w_), lambda b, i: (b, i, 0))
    per_b = lambda r_, w_: pl.BlockSpec((rows, r_, w_), lambda b, i: (b, 0, 0))
    return pl.pallas_call(
        functools.partial(_ssd_body, tt=tt, rows=rows), grid=(bsz // rows, t // tt),
        in_specs=[bt(D_MODEL), _const_spec((1, D_MODEL)), _const_spec((D_MODEL, SSD_COLS), single=True),
                  _const_spec((SSD_CONV, SSD_CONV_CH)), _const_spec((1, SSD_CONV_CH)),
                  _const_spec((1, LANES)), _const_spec((1, LANES)),
                  _const_spec((1, SSD_WIDTH)), _const_spec((1, SSD_WIDTH)),
                  per_b(SUBLANES, SSD_CONV_CH), per_b(SSD_WIDTH, SSD_STATE)],
        out_specs=[bt(SSD_WIDTH), per_b(SUBLANES, SSD_CONV_CH), per_b(SSD_WIDTH, SSD_STATE)],
        out_shape=[jax.ShapeDtypeStruct((bsz, t, SSD_WIDTH), BF16),
                   jax.ShapeDtypeStruct((bsz, SUBLANES, SSD_CONV_CH), F32),
                   jax.ShapeDtypeStruct((bsz, SSD_WIDTH, SSD_STATE), F32)],
        scratch_shapes=[pltpu.VMEM((rows, tt + SUBLANES, SSD_CONV_CH), F32),
                        pltpu.VMEM((rows, SSD_WIDTH, SSD_STATE), F32)],
        compiler_params=_params(("parallel", "arbitrary"), 48),
        name="ssd",
    )(x, nm, w, cw, cb, dtb, alog, dexp, nw, cprev, h0)


def _gla_body(x_ref, nm_ref, w_ref, wgate_ref, bgate_ref, gn_ref, s0_ref, o_ref, snew_ref, st, *, tt, rows):
    t = pl.program_id(1)

    @pl.when(t == 0)
    def _():
        st[...] = s0_ref[...]

    hk = GLA_HEADS * GLA_DK
    nchunk = tt // GLA_CHUNK
    x = x_ref[...].reshape(rows * tt, D_MODEL)
    h = _rms(x, nm_ref[...]).astype(BF16)
    u = _dot(h, w_ref[...])
    q = u[:, 0:hk] * (GLA_DK ** -0.5)
    k = u[:, hk:2 * hk]
    v = u[:, 2 * hk:2 * hk + GLA_WIDTH].astype(BF16)
    gg = u[:, 2 * hk + GLA_WIDTH:2 * hk + 2 * GLA_WIDTH]
    gr = u[:, 2 * hk + 2 * GLA_WIDTH:].astype(BF16)
    la = _log_sigmoid(_dot(gr, wgate_ref[...]) + bgate_ref[...]) / GLA_TAU

    r, c = _tri_mask(tt)
    sel = (((r // GLA_CHUNK) == (c // GLA_CHUNK)) & (r >= c)).astype(BF16)
    bcum = jnp.concatenate(
        [_dot_sel_lhs(sel, la[i * tt:(i + 1) * tt]) for i in range(rows)], axis=0)
    blast = jnp.concatenate(
        [jnp.broadcast_to(bcum[(ci + 1) * GLA_CHUNK - 1:(ci + 1) * GLA_CHUNK, :], (GLA_CHUNK, hk))
         for ci in range(rows * nchunk)], axis=0)
    qd = (q * jnp.exp(bcum)).astype(BF16)
    kd = (k * jnp.exp(-bcum)).astype(BF16)
    kend = (k * jnp.exp(blast - bcum)).astype(BF16)
    eblast = jnp.exp(blast)

    lane_k = lax.broadcasted_iota(jnp.int32, (GLA_CHUNK, hk), 1) // GLA_DK
    lane_v = lax.broadcasted_iota(jnp.int32, (GLA_CHUNK, GLA_WIDTH), 1) // GLA_DV
    ar, ac = _tri_mask(GLA_HEADS * GLA_CHUNK, GLA_CHUNK)
    att_causal = (ar % GLA_CHUNK) >= ac
    sr, sc = _tri_mask(GLA_WIDTH, hk)
    diag = (sr // GLA_DV) == (sc // GLA_DK)

    units = [(i, ci) for ci in range(nchunk) for i in range(rows)]
    sl = {(i, ci): slice(i * tt + ci * GLA_CHUNK, i * tt + (ci + 1) * GLA_CHUNK) for i, ci in units}
    att, upd, o_intra = {}, {}, {}
    for un in units:
        qd_c = qd[sl[un]]
        lhs = jnp.concatenate([jnp.where(lane_k == hh, qd_c, jnp.zeros_like(qd_c))
                               for hh in range(GLA_HEADS)], axis=0)
        att[un] = jnp.where(att_causal, _dot_nt(lhs, kd[sl[un]]), 0.0).astype(BF16)
        upd[un] = jnp.where(diag, _dot_tn(v[sl[un]], kend[sl[un]]), 0.0)
    for un in units:
        res = _dot(att[un], v[sl[un]])
        acc = jnp.zeros((GLA_CHUNK, GLA_WIDTH), F32)
        for hh in range(GLA_HEADS):
            acc = jnp.where(lane_v == hh, res[hh * GLA_CHUNK:(hh + 1) * GLA_CHUNK], acc)
        o_intra[un] = acc
    outs = {}
    for un in units:
        i = un[0]
        s_prev = st[i]
        outs[un] = o_intra[un] + _dot_nt(qd[sl[un]], s_prev.astype(BF16))
        st[i] = s_prev * eblast[sl[un].start:sl[un].start + 1, :] + upd[un]

    o = jnp.concatenate([outs[(i, ci)] for i in range(rows) for ci in range(nchunk)], axis=0)
    ms = _dot_sel_rhs(o * o, _group_mean_matrix(GLA_WIDTH, GLA_DV), parts=2)
    o = o * lax.rsqrt(ms + EPS) * gn_ref[...]
    o_ref[...] = (o * _silu(gg)).astype(o_ref.dtype).reshape(rows, tt, GLA_WIDTH)

    @pl.when(t == pl.num_programs(1) - 1)
    def _():
        snew_ref[...] = st[...]


def _gla_call(x, nm, w, wgate, bgate, gn, s0, tt, rows):
    bsz, t, _ = x.shape
    tt = _row_tile(t, tt)
    rows = _row_tile(bsz, rows)
    hk = GLA_HEADS * GLA_DK
    bt = lambda w_: pl.BlockSpec((rows, tt, w_), lambda b, i: (b, i, 0))
    per_b = pl.BlockSpec((rows, GLA_WIDTH, hk), lambda b, i: (b, 0, 0))
    return pl.pallas_call(
        functools.partial(_gla_body, tt=tt, rows=rows), grid=(bsz // rows, t // tt),
        in_specs=[bt(D_MODEL), _const_spec((1, D_MODEL)), _const_spec((D_MODEL, GLA_COLS), single=True),
                  _const_spec((LANES, hk)), _const_spec((1, hk)), _const_spec((1, GLA_WIDTH)), per_b],
        out_specs=[bt(GLA_WIDTH), per_b],
        out_shape=[jax.ShapeDtypeStruct((bsz, t, GLA_WIDTH), BF16),
                   jax.ShapeDtypeStruct((bsz, GLA_WIDTH, hk), F32)],
        scratch_shapes=[pltpu.VMEM((rows, GLA_WIDTH, hk), F32)],
        compiler_params=_params(("parallel", "arbitrary"), 32),
        name="gla",
    )(x, nm, w, wgate, bgate, gn, s0)


def _fox_proj_body(x_ref, nm_ref, w_ref, qn_ref, kn_ref, fb_ref, pa_ref, arow_ref,
                   k_ref, v_ref, lf_ref, *rest, tt, attn_layout):
    if attn_layout:
        qt_ref, kp_ref, vt_ref, carry = rest
    else:
        qs_ref, carry = rest
    t = pl.program_id(1)

    @pl.when(t == 0)
    def _():
        carry[...] = jnp.zeros_like(carry)

    h = _rms(x_ref[...], nm_ref[...]).astype(BF16)
    u = _dot(h, w_ref[...])
    fq = u[:, 0:FOX_WIDTH]
    fk = u[:, FOX_WIDTH:2 * FOX_WIDTH]
    fv = u[:, 2 * FOX_WIDTH:3 * FOX_WIDTH]
    ff = u[:, 3 * FOX_WIDTH:]

    gmat = _group_mean_matrix(FOX_WIDTH, FOX_HEAD_DIM)
    qn = fq * lax.rsqrt(_dot_sel_rhs(fq * fq, gmat, parts=2) + EPS) * qn_ref[...]
    kn = fk * lax.rsqrt(_dot_sel_rhs(fk * fk, gmat, parts=2) + EPS) * kn_ref[...]
    lf = _log_sigmoid(ff + fb_ref[...])
    k_ref[...] = kn
    v_ref[...] = fv
    lf_ref[...] = lf
    qs = (qn * (FOX_HEAD_DIM ** -0.5)).astype(BF16)

    if not attn_layout:
        qs_ref[...] = qs
        return

    r, c = _tri_mask(tt)
    cum = carry[...] + _dot_sel_lhs((r >= c).astype(BF16), lf)
    carry[...] = cum[tt - 1:tt, :]
    c3 = jnp.concatenate(_split_bf16(cum, AUX_PARTS), axis=1)
    aux = _dot(c3, pa_ref[...])
    aux_k = (aux[:, 0:LANES] + arow_ref[0:1, :]).astype(BF16)
    aux_q_all = aux[:, LANES:]
    knb = kn.astype(BF16)
    fv_t = fv.T
    ones = jnp.ones((FOX_VT_ROWS - FOX_HEAD_DIM, tt), F32)
    lane = lax.broadcasted_iota(jnp.int32, (tt, LANES), 1)
    for p in range(FOX_HEADS // 2):
        sl = slice(p * LANES, (p + 1) * LANES)
        kp_ref[p, :, 0:LANES] = knb[:, sl]
        kp_ref[p, :, LANES:] = aux_k
        for e in range(2):
            hh = 2 * p + e
            vt = jnp.concatenate([fv_t[hh * FOX_HEAD_DIM:(hh + 1) * FOX_HEAD_DIM, :], ones], axis=0).astype(BF16)
            for j in range(tt // FOX_KB):
                vt_ref[hh, j] = vt[:, j * FOX_KB:(j + 1) * FOX_KB]
            qmask = jnp.where((lane // FOX_HEAD_DIM) == e, qs[:, sl], jnp.zeros_like(qs[:, sl])).astype(F32)
            aux_q = jnp.where((lane // AUX_PARTS) == hh, aux_q_all, 0.0) + arow_ref[1 + hh:2 + hh, :]
            q_t = jnp.concatenate([qmask.T, aux_q.T], axis=0).astype(BF16)
            for j in range(tt // FOX_QB):
                qt_ref[hh, j] = q_t[:, j * FOX_QB:(j + 1) * FOX_QB]


def _fox_aux_constants():
    nslot = FOX_HEADS * AUX_PARTS
    pa = np.zeros((AUX_PARTS * LANES, 2 * LANES), np.float32)
    arow = np.zeros((SUBLANES, LANES), np.float32)
    for hh in range(FOX_HEADS):
        for j in range(AUX_PARTS):
            pa[j * LANES + hh, LANES + AUX_PARTS * hh + j] = 1.0
            pa[j * LANES + hh, nslot + AUX_PARTS * hh + j] = -1.0
            arow[0, AUX_PARTS * hh + j] = 1.0
            arow[1 + hh, nslot + AUX_PARTS * hh + j] = 1.0
    return jnp.asarray(pa, BF16), jnp.asarray(arow, F32)


def _fox_proj_call(x, nm, w, qn, kn, fb, tt, attn_layout):
    bsz, t, _ = x.shape
    tt = _row_tile(t, tt)
    pa, arow = _fox_aux_constants()
    bt = lambda w_: pl.BlockSpec((None, tt, w_), lambda b, i: (b, i, 0))
    bht = lambda n_, w_: pl.BlockSpec((None, n_, tt, w_), lambda b, i: (b, 0, i, 0))
    out_specs = [bt(FOX_WIDTH), bt(FOX_WIDTH), bt(LANES)]
    out_shape = [jax.ShapeDtypeStruct((bsz, t, FOX_WIDTH), F32),
                 jax.ShapeDtypeStruct((bsz, t, FOX_WIDTH), F32),
                 jax.ShapeDtypeStruct((bsz, t, LANES), F32)]
    if attn_layout:
        tiled = lambda n_, r_, w_: pl.BlockSpec((None, FOX_HEADS, tt // n_, r_, w_), lambda b, i: (b, 0, i, 0, 0))
        out_specs += [tiled(FOX_QB, 2 * LANES, FOX_QB), bht(FOX_HEADS // 2, 2 * LANES),
                      tiled(FOX_KB, FOX_VT_ROWS, FOX_KB)]
        out_shape += [jax.ShapeDtypeStruct((bsz, FOX_HEADS, t // FOX_QB, 2 * LANES, FOX_QB), BF16),
                      jax.ShapeDtypeStruct((bsz, FOX_HEADS // 2, t, 2 * LANES), BF16),
                      jax.ShapeDtypeStruct((bsz, FOX_HEADS, t // FOX_KB, FOX_VT_ROWS, FOX_KB), BF16)]
    else:
        out_specs += [bt(FOX_WIDTH)]
        out_shape += [jax.ShapeDtypeStruct((bsz, t, FOX_WIDTH), BF16)]
    return pl.pallas_call(
        functools.partial(_fox_proj_body, tt=tt, attn_layout=attn_layout), grid=(bsz, t // tt),
        in_specs=[bt(D_MODEL), _const_spec((1, D_MODEL)), _const_spec((D_MODEL, FOX_COLS), single=True),
                  _const_spec((1, FOX_WIDTH)), _const_spec((1, FOX_WIDTH)), _const_spec((1, LANES)),
                  _const_spec(pa.shape), _const_spec(arow.shape)],
        out_specs=out_specs, out_shape=out_shape,
        scratch_shapes=[pltpu.VMEM((1, LANES), F32)],
        compiler_params=_params(("parallel", "arbitrary"), 32),
        name="fox_proj",
    )(x, nm, w, qn, kn, fb, pa, arow)


def _fox_prompt_body(qt_ref, k_ref, vt_ref, o_ref, m_s, acc_s, s_s, *, nsub):
    qi = pl.program_id(2)
    qtile = nsub * FOX_QB
    chains = [(si, e) for si in range(nsub) for e in range(2)]
    m_s[...] = jnp.full_like(m_s, -jnp.inf)
    acc_s[...] = jnp.zeros_like(acc_s)

    nch = len(chains)

    def scores(c, k0, nk, masked):
        si, e = chains[c]
        s = _dot(k_ref[pl.ds(k0, nk), :], qt_ref[e, si])
        if masked:
            r, col = _tri_mask(FOX_QB)
            tail = jnp.where(r <= col, s[nk - FOX_QB:], -jnp.inf)
            s = tail if nk == FOX_QB else jnp.concatenate([s[:nk - FOX_QB], tail], axis=0)
        return s

    def absorb(c, s, vt0):
        e = chains[c][1]
        m_old = m_s[c]
        m_new = jnp.maximum(m_old, jnp.max(s, axis=0, keepdims=True))
        alpha = jnp.exp(m_old - m_new)
        p = jnp.exp(s - m_new).astype(BF16)
        vts = [vt_ref[e, vt0 + j] for j in range(s.shape[0] // FOX_KB)]
        vt = vts[0] if len(vts) == 1 else jnp.concatenate(vts, axis=1)
        acc_s[c] = alpha * acc_s[c] + _dot(vt, p)
        m_s[c] = m_new

    def sweep(k0, vt0, nks, masked):
        for c in range(nch):
            s_s[c, 0:nks[c], :] = scores(c, k0, nks[c], masked)
        for c in range(nch):
            absorb(c, s_s[c, 0:nks[c], :], vt0)

    def below_diagonal(i, carry):
        sweep(pl.multiple_of(i * FOX_SWEEP, FOX_SWEEP), i * (FOX_SWEEP // FOX_KB), [FOX_SWEEP] * nch, False)
        return carry

    lax.fori_loop(0, qi * (qtile // FOX_SWEEP), below_diagonal, 0)
    sweep(pl.multiple_of(qi * qtile, qtile), qi * (qtile // FOX_KB),
          [(si + 1) * FOX_QB for si, _ in chains], True)

    for si in range(nsub):
        outs = []
        for e in range(2):
            acc = acc_s[chains.index((si, e))]
            outs.append(acc[0:FOX_HEAD_DIM] / acc[FOX_HEAD_DIM:FOX_HEAD_DIM + 1])
        o = jnp.concatenate(outs, axis=0).T
        o_ref[si * FOX_QB:(si + 1) * FOX_QB, :] = o.astype(o_ref.dtype)


def _fox_prompt_call(qt, kp, vt, tq):
    bsz, _, nqb, _, _ = qt.shape
    t = kp.shape[2]
    nsub = _row_tile(nqb, max(FOX_SWEEP // FOX_QB, tq // FOX_QB))
    npair = FOX_HEADS // 2
    nch = 2 * nsub
    return pl.pallas_call(
        functools.partial(_fox_prompt_body, nsub=nsub), grid=(bsz, npair, nqb // nsub),
        in_specs=[pl.BlockSpec((None, 2, nsub, 2 * LANES, FOX_QB), lambda b, p, i: (b, p, i, 0, 0)),
                  pl.BlockSpec((None, None, t, 2 * LANES), lambda b, p, i: (b, p, 0, 0)),
                  pl.BlockSpec((None, 2, t // FOX_KB, FOX_VT_ROWS, FOX_KB), lambda b, p, i: (b, p, 0, 0, 0))],
        out_specs=pl.BlockSpec((None, nsub * FOX_QB, LANES), lambda b, p, i: (b, i, p)),
        out_shape=jax.ShapeDtypeStruct((bsz, t, FOX_WIDTH), BF16),
        scratch_shapes=[pltpu.VMEM((nch, 1, FOX_QB), F32),
                        pltpu.VMEM((nch, FOX_VT_ROWS, FOX_QB), F32),
                        pltpu.VMEM((nch, nsub * FOX_QB, FOX_QB), F32)],
        compiler_params=_params(("parallel", "parallel", "arbitrary"), 40),
        name="fox_attn_prompt",
    )(qt, kp, vt)


def _fox_sample_body(q_ref, kn_ref, vn_ref, lfn_ref, kc_ref, vc_ref, lfp_ref, o_ref, *, tn, past, seg):
    r, c = _tri_mask(seg)
    upper = (r <= c).astype(BF16)
    carry = jnp.zeros((FOX_HEADS, 1), F32)
    cps = []
    for j in range(past // seg):
        cs = carry + _dot_sel_rhs(lfp_ref[:, j * seg:(j + 1) * seg], upper)
        cps.append(cs)
        carry = cs[:, seg - 1:seg]
    cp = jnp.concatenate(cps, axis=1) if len(cps) > 1 else cps[0]
    lfn = lfn_ref[...]
    if tn < LANES:
        lfn = jnp.concatenate([lfn, jnp.zeros((LANES - tn, LANES), F32)], axis=0)
    r, c = _tri_mask(LANES)
    cn = _dot_sel_lhs((r >= c).astype(BF16), lfn)
    cn_t = cn.T

    q = q_ref[...]
    kn = kn_ref[...].astype(BF16)
    vn = vn_ref[...].astype(BF16)
    r, c = _tri_mask(tn)
    outs = []
    for hh in range(FOX_HEADS):
        hs = slice(hh * FOX_HEAD_DIM, (hh + 1) * FOX_HEAD_DIM)
        qh = q[:, hs]
        cq = cn[0:tn, hh:hh + 1]
        s_past = _dot(qh, kc_ref[hh].astype(BF16)) + ((carry[hh:hh + 1, :] + cq) - cp[hh:hh + 1, :])
        s_new = _dot_nt(qh, kn[:, hs]) + (cq - cn_t[hh:hh + 1, 0:tn])
        s_new = jnp.where(r >= c, s_new, -jnp.inf)
        m = jnp.maximum(jnp.max(s_past, axis=-1, keepdims=True), jnp.max(s_new, axis=-1, keepdims=True))
        p_past = jnp.exp(s_past - m)
        p_new = jnp.exp(s_new - m)
        denom = jnp.sum(p_past, axis=-1, keepdims=True) + jnp.sum(p_new, axis=-1, keepdims=True)
        o = _dot_nt(p_past.astype(BF16), vc_ref[hh].astype(BF16)) + _dot(p_new.astype(BF16), vn[:, hs])
        outs.append(o / denom)
    o_ref[...] = jnp.concatenate(outs, axis=1).astype(o_ref.dtype)


def _fox_sample_call(q, kn, vn, lfn, cache_k, cache_v, lfp_t, layer):
    bsz, tn, _ = q.shape
    past = cache_k.shape[-1]
    seg = _row_tile(past, 512)
    bt = lambda w_: pl.BlockSpec((None, tn, w_), lambda b: (b, 0, 0))
    cache = pl.BlockSpec((None, None, FOX_HEADS, FOX_HEAD_DIM, past), lambda b: (layer, b, 0, 0, 0))
    return pl.pallas_call(
        functools.partial(_fox_sample_body, tn=tn, past=past, seg=seg), grid=(bsz,),
        in_specs=[bt(FOX_WIDTH), bt(FOX_WIDTH), bt(FOX_WIDTH), bt(LANES), cache, cache,
                  pl.BlockSpec((None, None, FOX_HEADS, past), lambda b: (layer, b, 0, 0))],
        out_specs=bt(FOX_WIDTH),
        out_shape=jax.ShapeDtypeStruct((bsz, tn, FOX_WIDTH), BF16),
        compiler_params=_params(("parallel",), 48),
        name="fox_attn_sample",
    )(q, kn, vn, lfn, cache_k, cache_v, lfp_t)


def _pad_cols(a, width):
    return jnp.pad(a, ((0, 0), (0, width - a.shape[1])))


def _layer_params(l, norm_ffn1, w1_gate, w1_up, w1_down, norm_mix, w_in, ssd_conv_w, ssd_conv_b,
                  ssd_dt_bias, ssd_a_log, ssd_d, ssd_norm, gla_w_gate, gla_b_gate, gla_norm,
                  fox_q_norm, fox_k_norm, fox_f_bias, w_out, norm_ffn2, w2_gate, w2_up, w2_down):
    wi = w_in[l]
    row = lambda a: a.reshape(1, -1).astype(F32)
    hk = GLA_HEADS * GLA_DK
    p = dict(
        ffn1=(row(norm_ffn1[l]), w1_gate[l].astype(BF16), w1_up[l].astype(BF16), w1_down[l].astype(BF16)),
        ffn2=(row(norm_ffn2[l]), w2_gate[l].astype(BF16), w2_up[l].astype(BF16), w2_down[l].astype(BF16)),
        norm_mix=row(norm_mix[l]),
        w_out=w_out[l].astype(BF16),
        w_ssd=jnp.concatenate([wi[:, _OFF_Z:_OFF_DT], _pad_cols(wi[:, _OFF_DT:_OFF_GQ], LANES)], axis=1).astype(BF16),
        conv_w=ssd_conv_w[l].astype(F32),
        conv_b=row(ssd_conv_b[l]),
        dt_bias=_pad_cols(row(ssd_dt_bias[l]), LANES),
        a_log=_pad_cols(row(ssd_a_log[l]), LANES),
        d_exp=row(jnp.repeat(ssd_d[l], SSD_HEAD_DIM)),
        ssd_norm=row(ssd_norm[l]),
        w_gla=jnp.concatenate([wi[:, _OFF_GQ:_OFF_GR], _pad_cols(wi[:, _OFF_GR:_OFF_FQ], LANES)], axis=1).astype(BF16),
        w_gate=jnp.pad(gla_w_gate[l], ((0, LANES - GLA_RANK), (0, 0))).astype(BF16),
        b_gate=row(gla_b_gate[l]),
        gla_norm=row(jnp.tile(gla_norm[l], GLA_HEADS)),
        w_fox=jnp.concatenate([wi[:, _OFF_FQ:_OFF_FF], _pad_cols(wi[:, _OFF_FF:], LANES)], axis=1).astype(BF16),
        q_norm=row(jnp.tile(fox_q_norm[l], FOX_HEADS)),
        k_norm=row(jnp.tile(fox_k_norm[l], FOX_HEADS)),
        f_bias=_pad_cols(row(fox_f_bias[l]), LANES),
    )
    return p


def _gla_state_in(s):
    bsz = s.shape[0]
    eye = jnp.eye(GLA_HEADS, dtype=s.dtype)
    full = jnp.einsum('bhkv,hg->bhvgk', s, eye)
    return full.reshape(bsz, GLA_HEADS * GLA_DV, GLA_HEADS * GLA_DK)


def _gla_state_out(st):
    bsz = st.shape[0]
    full = st.reshape(bsz, GLA_HEADS, GLA_DV, GLA_HEADS, GLA_DK)
    idx = jnp.arange(GLA_HEADS)
    diag = full[:, idx, :, idx, :]
    return jnp.transpose(diag, (1, 0, 3, 2))


def _mix(x, p, conv_prev, ssd_h0, gla_s0, fox_cache, layer, tiles):
    bsz, t, _ = x.shape
    cprev = jnp.pad(conv_prev, ((0, 0), (SUBLANES - (SSD_CONV - 1), 0), (0, 0)))
    y_ssd, cnew, hnew = _ssd_call(x, p['norm_mix'], p['w_ssd'], p['conv_w'], p['conv_b'], p['dt_bias'],
                                  p['a_log'], p['d_exp'], p['ssd_norm'], cprev,
                                  ssd_h0.reshape(bsz, SSD_WIDTH, SSD_STATE), tiles['ssd'], tiles['ssd_rows'])
    o_gla, snew = _gla_call(x, p['norm_mix'], p['w_gla'], p['w_gate'], p['b_gate'], p['gla_norm'],
                            _gla_state_in(gla_s0), tiles['gla'], tiles['gla_rows'])
    if fox_cache is None:
        k, v, lf, qt, kp, vt = _fox_proj_call(x, p['norm_mix'], p['w_fox'], p['q_norm'], p['k_norm'],
                                              p['f_bias'], tiles['fox_proj'], True)
        o_fox = _fox_prompt_call(qt, kp, vt, tiles['fox_q'])
    else:
        k, v, lf, qs = _fox_proj_call(x, p['norm_mix'], p['w_fox'], p['q_norm'], p['k_norm'],
                                      p['f_bias'], tiles['fox_proj'], False)
        cache_k, cache_v, lfp_t = fox_cache
        o_fox = _fox_sample_call(qs, k, v, lf, cache_k, cache_v, lfp_t, layer)
    state = (cnew[:, SUBLANES - (SSD_CONV - 1):, :],
             hnew.reshape(bsz, SSD_HEADS, SSD_HEAD_DIM, SSD_STATE),
             _gla_state_out(snew),
             k.reshape(bsz, t, FOX_HEADS, FOX_HEAD_DIM),
             v.reshape(bsz, t, FOX_HEADS, FOX_HEAD_DIM),
             lf[:, :, 0:FOX_HEADS])
    return (y_ssd, o_gla, o_fox), state


def _trunk_layer(x, p, conv_prev, ssd_h0, gla_s0, fox_cache, layer, tiles):
    bsz, t, d = x.shape
    x1 = _ffn_call(x.reshape(bsz * t, d), *p['ffn1'], tiles['ffn']).reshape(bsz, t, d)
    (y_ssd, o_gla, o_fox), state = _mix(x1, p, conv_prev, ssd_h0, gla_s0, fox_cache, layer, tiles)
    flat = lambda a: a.reshape(bsz * t, a.shape[-1])
    x3 = _out_ffn_call(flat(x1), flat(y_ssd), flat(o_gla), flat(o_fox), p['w_out'], *p['ffn2'], tiles['ffn'])
    return x3.reshape(bsz, t, d), state


PROMPT_TILES = dict(ffn=512, ssd=256, ssd_rows=2, gla=256, gla_rows=4, fox_proj=512, fox_q=1024)
SAMPLE_TILES = dict(ffn=512, ssd=64, ssd_rows=4, gla=64, gla_rows=4, fox_proj=64, fox_q=64)


def kernel(x_prompt, x_sample, state_ssd_conv, state_ssd, state_gla, cache_fox_k, cache_fox_v, cache_fox_logf, norm_ffn1, w1_gate, w1_up, w1_down, norm_mix, w_in, ssd_conv_w, ssd_conv_b, ssd_dt_bias, ssd_a_log, ssd_d, ssd_norm, gla_w_gate, gla_b_gate, gla_norm, fox_q_norm, fox_k_norm, fox_f_bias, w_out, norm_ffn2, w2_gate, w2_up, w2_down):
    depth = w_in.shape[0]
    bp = x_prompt.shape[0]
    weights = (norm_ffn1, w1_gate, w1_up, w1_down, norm_mix, w_in, ssd_conv_w, ssd_conv_b, ssd_dt_bias,
               ssd_a_log, ssd_d, ssd_norm, gla_w_gate, gla_b_gate, gla_norm, fox_q_norm, fox_k_norm,
               fox_f_bias, w_out, norm_ffn2, w2_gate, w2_up, w2_down)
    cache_k = jnp.transpose(cache_fox_k, (0, 1, 3, 4, 2))
    cache_v = jnp.transpose(cache_fox_v, (0, 1, 3, 4, 2))
    lfp_t = jnp.swapaxes(cache_fox_logf, 2, 3)
    zeros_conv = jnp.zeros((bp, SSD_CONV - 1, SSD_CONV_CH), F32)
    zeros_ssd = jnp.zeros((bp, SSD_HEADS, SSD_HEAD_DIM, SSD_STATE), F32)
    zeros_gla = jnp.zeros((bp, GLA_HEADS, GLA_DK, GLA_DV), F32)

    xp, xs = x_prompt, x_sample
    p_new = [[] for _ in range(6)]
    s_new = [[] for _ in range(6)]
    for l in range(depth):
        p = _layer_params(l, *weights)
        xp, st_p = _trunk_layer(xp, p, zeros_conv, zeros_ssd, zeros_gla, None, l, PROMPT_TILES)
        xs, st_s = _trunk_layer(xs, p, state_ssd_conv[l], state_ssd[l], state_gla[l],
                                (cache_k, cache_v, lfp_t), l, SAMPLE_TILES)
        for i in range(6):
            p_new[i].append(st_p[i])
            s_new[i].append(st_s[i])
    outs_p = [jnp.stack(a) for a in p_new]
    outs_s = [jnp.stack(a) for a in s_new]
    return (xp, xs, *outs_p, *outs_s)
```

```python
import functools

import numpy as np
import jax
import jax.numpy as jnp
from jax import lax
from jax.experimental import pallas as pl
from jax.experimental.pallas import tpu as pltpu

F32 = jnp.float32
BF16 = jnp.bfloat16

EPS = 1e-6
D_MODEL = 1024
D_FF = 2816
SSD_HEADS = 8
SSD_HEAD_DIM = 64
SSD_WIDTH = 512
SSD_GROUPS = 2
SSD_STATE = 128
SSD_CONV = 4
SSD_CONV_CH = 1024
GLA_HEADS = 4
GLA_DK = 32
GLA_DV = 64
GLA_WIDTH = 256
GLA_RANK = 16
GLA_TAU = 16.0
GLA_CHUNK = 64
FOX_HEADS = 4
FOX_HEAD_DIM = 64
FOX_WIDTH = 256
LANES = 128
SUBLANES = 8
MIB = 1024 * 1024

_OFF_Z, _OFF_XBC, _OFF_DT = 0, 512, 1536
_OFF_GQ, _OFF_GR = 1544, 2312
_OFF_FQ, _OFF_FF = 2328, 3096
SSD_COLS = SSD_WIDTH + SSD_CONV_CH + LANES
GLA_COLS = 2 * GLA_HEADS * GLA_DK + 2 * GLA_WIDTH + LANES
FOX_COLS = 3 * FOX_WIDTH + LANES
AUX_PARTS = 3
FOX_QB = 256
FOX_KB = 256
FOX_SWEEP = 512
FOX_VT_ROWS = FOX_HEAD_DIM + 16


def _dot(a, b):
    return jnp.dot(a, b, preferred_element_type=F32)


def _dot_nt(a, b):
    return lax.dot_general(a, b, (((1,), (1,)), ((), ())), preferred_element_type=F32)


def _dot_tn(a, b):
    return lax.dot_general(a, b, (((0,), (0,)), ((), ())), preferred_element_type=F32)


def _split_bf16(x, parts):
    out = []
    r = x
    for i in range(parts):
        p = r.astype(BF16)
        out.append(p)
        if i + 1 < parts:
            r = r - p.astype(F32)
    return out


def _dot_sel_lhs(sel, x, parts=3):
    n = x.shape[1]
    t = _dot(sel, jnp.concatenate(_split_bf16(x, parts), axis=1))
    acc = t[:, 0:n]
    for i in range(1, parts):
        acc = acc + t[:, i * n:(i + 1) * n]
    return acc


def _dot_sel_rhs(x, sel, parts=3):
    m = x.shape[0]
    pieces = _split_bf16(x, parts)
    if m % (2 * SUBLANES):
        acc = _dot(pieces[0], sel)
        for p in pieces[1:]:
            acc = acc + _dot(p, sel)
        return acc
    t = _dot(jnp.concatenate(pieces, axis=0), sel)
    acc = t[0:m]
    for i in range(1, parts):
        acc = acc + t[i * m:(i + 1) * m]
    return acc


def _rms(x, w):
    ms = jnp.mean(x * x, axis=-1, keepdims=True)
    return x * lax.rsqrt(ms + EPS) * w


def _silu(x):
    return x * jax.nn.sigmoid(x)


def _softplus(x):
    return jnp.maximum(x, 0.0) + jnp.log1p(jnp.exp(-jnp.abs(x)))


def _log_sigmoid(x):
    return -_softplus(-x)


def _tri_mask(n, m=None):
    m = n if m is None else m
    r = lax.broadcasted_iota(jnp.int32, (n, m), 0)
    c = lax.broadcasted_iota(jnp.int32, (n, m), 1)
    return r, c


def _lane_pair_select(a_even, a_odd):
    lane = lax.broadcasted_iota(jnp.int32, a_even.shape, 1)
    return jnp.where(lane < 64, a_even, a_odd)


def _expand_heads(v, heads, rows):
    pieces = []
    for i in range(0, len(heads), 2):
        a = jnp.broadcast_to(v[:, heads[i]:heads[i] + 1], (rows, LANES))
        b = jnp.broadcast_to(v[:, heads[i + 1]:heads[i + 1] + 1], (rows, LANES))
        pieces.append(_lane_pair_select(a, b))
    return pieces[0] if len(pieces) == 1 else jnp.concatenate(pieces, axis=1)


def _group_mean_matrix(width, group):
    r, c = _tri_mask(width)
    return jnp.where((r // group) == (c // group), 1.0 / group, 0.0).astype(BF16)


def _params(sem, vmem_mib):
    return pltpu.CompilerParams(dimension_semantics=sem, vmem_limit_bytes=vmem_mib * MIB)


def _const_spec(shape, single=False):
    nd = len(shape)
    if single:
        return pl.BlockSpec(shape, lambda *_: (0,) * nd, pipeline_mode=pl.Buffered(1))
    return pl.BlockSpec(shape, lambda *_: (0,) * nd)


def _swiglu_half(x, g_ref, wg_ref, wu_ref, wd_ref):
    h = _rms(x, g_ref[...]).astype(BF16)
    gate = _dot(h, wg_ref[...])
    up = _dot(h, wu_ref[...])
    a = (_silu(gate) * up).astype(BF16)
    return x + 0.5 * _dot(a, wd_ref[...])


def _ffn_body(x_ref, g_ref, wg_ref, wu_ref, wd_ref, o_ref):
    o_ref[...] = _swiglu_half(x_ref[...], g_ref, wg_ref, wu_ref, wd_ref)


def _out_ffn_body(x_ref, ys_ref, og_ref, of_ref, wo_ref, g_ref, wg_ref, wu_ref, wd_ref, o_ref):
    x = x_ref[...]
    x = x + (_dot(ys_ref[...], wo_ref[0:SSD_WIDTH, :])
             + _dot(og_ref[...], wo_ref[SSD_WIDTH:SSD_WIDTH + GLA_WIDTH, :])
             + _dot(of_ref[...], wo_ref[SSD_WIDTH + GLA_WIDTH:, :]))
    o_ref[...] = _swiglu_half(x, g_ref, wg_ref, wu_ref, wd_ref)


def _ffn_weight_specs():
    return [_const_spec((1, D_MODEL)),
            _const_spec((D_MODEL, D_FF), single=True),
            _const_spec((D_MODEL, D_FF), single=True),
            _const_spec((D_FF, D_MODEL), single=True)]


def _row_tile(n, want):
    t = min(want, n)
    while n % t:
        t //= 2
    return t


def _ffn_call(x, g, wg, wu, wd, tm):
    n = x.shape[0]
    tm = _row_tile(n, tm)
    row = pl.BlockSpec((tm, D_MODEL), lambda i: (i, 0))
    return pl.pallas_call(
        _ffn_body, grid=(n // tm,),
        in_specs=[row] + _ffn_weight_specs(),
        out_specs=row,
        out_shape=jax.ShapeDtypeStruct((n, D_MODEL), F32),
        compiler_params=_params(("parallel",), 56),
        name="ffn",
    )(x, g, wg, wu, wd)


def _out_ffn_call(x, ys, og, of, wo, g, wg, wu, wd, tm):
    n = x.shape[0]
    tm = _row_tile(n, tm)
    row = lambda w: pl.BlockSpec((tm, w), lambda i: (i, 0))
    return pl.pallas_call(
        _out_ffn_body, grid=(n // tm,),
        in_specs=[row(D_MODEL), row(SSD_WIDTH), row(GLA_WIDTH), row(FOX_WIDTH),
                  _const_spec((D_MODEL, D_MODEL), single=True)] + _ffn_weight_specs(),
        out_specs=row(D_MODEL),
        out_shape=jax.ShapeDtypeStruct((n, D_MODEL), F32),
        compiler_params=_params(("parallel",), 56),
        name="out_ffn",
    )(x, ys, og, of, wo, g, wg, wu, wd)


def _ssd_body(x_ref, nm_ref, w_ref, cw_ref, cb_ref, dtb_ref, alog_ref, dexp_ref, nw_ref,
              cprev_ref, h0_ref, y_ref, cnew_ref, hnew_ref, xbuf, hst, *, tt, rows):
    t = pl.program_id(1)

    @pl.when(t == 0)
    def _():
        xbuf[...] = cprev_ref[...]
        hst[...] = h0_ref[...]

    x = x_ref[...].reshape(rows * tt, D_MODEL)
    h = _rms(x, nm_ref[...]).astype(BF16)
    u = _dot(h, w_ref[...])
    z = u[:, 0:SSD_WIDTH]
    xbc = u[:, SSD_WIDTH:SSD_WIDTH + SSD_CONV_CH]
    dt_raw = u[:, SSD_WIDTH + SSD_CONV_CH:]

    convs = []
    row8 = lax.broadcasted_iota(jnp.int32, (SUBLANES, SSD_CONV_CH), 0)
    for i in range(rows):
        xi = xbc[i * tt:(i + 1) * tt]
        prev = xbuf[i]
        conv = None
        for j in range(SSD_CONV - 1, 0, -1):
            rolled = pltpu.roll(xi, j, 0)
            head = jnp.where(row8 < j, pltpu.roll(prev, j, 0), rolled[0:SUBLANES])
            term = jnp.concatenate([head, rolled[SUBLANES:]], axis=0) * cw_ref[SSD_CONV - 1 - j:SSD_CONV - j, :]
            conv = term if conv is None else conv + term
        convs.append(conv + xi * cw_ref[SSD_CONV - 1:SSD_CONV, :])
        xbuf[i] = xi[tt - SUBLANES:, :]
    xa = _silu(jnp.concatenate(convs, axis=0) + cb_ref[...])

    xs = xa[:, 0:SSD_WIDTH]
    bm = xa[:, SSD_WIDTH:SSD_WIDTH + SSD_GROUPS * SSD_STATE].astype(BF16)
    cm = xa[:, SSD_WIDTH + SSD_GROUPS * SSD_STATE:].astype(BF16)
    dt = _softplus(dt_raw + dtb_ref[...])
    a = dt * (-jnp.exp(alog_ref[...]))

    r, c = _tri_mask(tt)
    causal = r >= c
    tri = causal.astype(BF16)
    acum = jnp.concatenate(
        [_dot_sel_lhs(tri, a[i * tt:(i + 1) * tt]) for i in range(rows)], axis=0)
    shift_t = (acum - jnp.log(dt)).T
    alast = jnp.concatenate(
        [jnp.broadcast_to(acum[(i + 1) * tt - 1:(i + 1) * tt, :], (tt, LANES)) for i in range(rows)], axis=0)
    ea = jnp.exp(acum)
    wend = jnp.exp(alast - acum) * dt

    heads_per_group = SSD_HEADS // SSD_GROUPS
    gw = heads_per_group * SSD_HEAD_DIM
    heads_of = [list(range(g * heads_per_group, (g + 1) * heads_per_group)) for g in range(SSD_GROUPS)]
    nrow = rows * tt
    xb = xs.astype(BF16)
    xw =jnp.concatenate([xs[:, g * gw:(g + 1) * gw] * _expand_heads(wend, heads_of[g], nrow)
                          for g in range(SSD_GROUPS)], axis=1).astype(BF16)
    ea_x = jnp.concatenate([_expand_heads(ea, heads_of[g], nrow) for g in range(SSD_GROUPS)], axis=1)

    units = [(i, g) for g in range(SSD_GROUPS) for i in range(rows)]
    rs = {i: slice(i * tt, (i + 1) * tt) for i in range(rows)}
    gs = {g: slice(g * SSD_STATE, (g + 1) * SSD_STATE) for g in range(SSD_GROUPS)}
    cbs = {(i, g): _dot_nt(cm[rs[i], gs[g]], bm[rs[i], gs[g]]) for i, g in units}
    upds = {(i, g): _dot_tn(xw[rs[i], g * gw:(g + 1) * gw], bm[rs[i], gs[g]]) for i, g in units}
    y_unit = {}
    for i, g in units:
        heads = heads_of[g]
        intra = []
        for pi in range(heads_per_group // 2):
            ys = []
            for e in range(2):
                hh = heads[2 * pi + e]
                seg = acum[rs[i], hh:hh + 1] - shift_t[hh:hh + 1, rs[i]]
                m = (cbs[(i, g)] * jnp.where(causal, jnp.exp(seg), 0.0)).astype(BF16)
                col = g * gw + pi * LANES
                ys.append(_dot(m, xb[rs[i], col:col + LANES]))
            intra.append(_lane_pair_select(ys[0], ys[1]))
        hg = hst[i, g * gw:(g + 1) * gw, :]
        y_inter = _dot_nt(cm[rs[i], gs[g]], hg.astype(BF16)) * ea_x[rs[i], g * gw:(g + 1) * gw]
        y_unit[(i, g)] = jnp.concatenate(intra, axis=1) + y_inter
        cd = jnp.concatenate(
            [jnp.broadcast_to(jnp.exp(acum[(i + 1) * tt - 1:(i + 1) * tt, hh:hh + 1]), (SSD_HEAD_DIM, SSD_STATE))
             for hh in heads], axis=0)
        hst[i, g * gw:(g + 1) * gw, :] = hg * cd + upds[(i, g)]

    y = jnp.concatenate(
        [jnp.concatenate([y_unit[(i, g)] for g in range(SSD_GROUPS)], axis=1) for i in range(rows)], axis=0)
    y = (y + dexp_ref[...] * xs) * _silu(z)
    nw = nw_ref[...]
    outs = [_rms(y[:, g * gw:(g + 1) * gw], nw[:, g * gw:(g + 1) * gw]) for g in range(SSD_GROUPS)]
    y_ref[...] = jnp.concatenate(outs, axis=1).astype(y_ref.dtype).reshape(rows, tt, SSD_WIDTH)

    @pl.when(t == pl.num_programs(1) - 1)
    def _():
        cnew_ref[...] = xbuf[...]
        hnew_ref[...] = hst[...]


def _ssd_call(x, nm, w, cw, cb, dtb, alog, dexp, nw, cprev, h0, tt, rows):
    bsz, t, _ = x.shape
    tt = _row_tile(t, tt)
    rows = _row_tile(bsz, rows)
    bt = lambda w_: pl.BlockSpec((rows, tt, w_), lambda b, i: (b, i, 0))
    per_b = lambda r_, w_: pl.BlockSpec((rows, r_, w_), lambda b, i: (b, 0, 0))
    return pl.pallas_call(
        functools.partial(_ssd_body, tt=tt, rows=rows), grid=(bsz // rows, t // tt),
        in_specs=[bt(D_MODEL), _const_spec((1, D_MODEL)), _const_spec((D_MODEL, SSD_COLS), single=True),
                  _const_spec((SSD_CONV, SSD_CONV_CH)), _const_spec((1, SSD_CONV_CH)),
                  _const_spec((1, LANES)), _const_spec((1, LANES)),
                  _const_spec((1, SSD_WIDTH)), _const_spec((1, SSD_WIDTH)),
                  per_b(SUBLANES, SSD_CONV_CH), per_b(SSD_WIDTH, SSD_STATE)],
        out_specs=[bt(SSD_WIDTH), per_b(SUBLANES, SSD_CONV_CH), per_b(SSD_WIDTH, SSD_STATE)],
        out_shape=[jax.ShapeDtypeStruct((bsz, t, SSD_WIDTH), BF16),
                   jax.ShapeDtypeStruct((bsz, SUBLANES, SSD_CONV_CH), F32),
                   jax.ShapeDtypeStruct((bsz, SSD_WIDTH, SSD_STATE), F32)],
        scratch_shapes=[pltpu.VMEM((rows, SUBLANES, SSD_CONV_CH), F32),
                        pltpu.VMEM((rows, SSD_WIDTH, SSD_STATE), F32)],
        compiler_params=_params(("parallel", "arbitrary"), 48),
        name="ssd",
    )(x, nm, w, cw, cb, dtb, alog, dexp, nw, cprev, h0)


def _gla_body(x_ref, nm_ref, w_ref, wgate_ref, bgate_ref, gn_ref, s0_ref, o_ref, snew_ref, st, *, tt, rows):
    t = pl.program_id(1)

    @pl.when(t == 0)
    def _():
        st[...] = s0_ref[...]

    hk = GLA_HEADS * GLA_DK
    nchunk = tt // GLA_CHUNK
    x = x_ref[...].reshape(rows * tt, D_MODEL)
    h = _rms(x, nm_ref[...]).astype(BF16)
    u = _dot(h, w_ref[...])
    q = u[:, 0:hk] * (GLA_DK ** -0.5)
    k = u[:, hk:2 * hk]
    v = u[:, 2 * hk:2 * hk + GLA_WIDTH].astype(BF16)
    gg = u[:, 2 * hk + GLA_WIDTH:2 * hk + 2 * GLA_WIDTH]
    gr = u[:, 2 * hk + 2 * GLA_WIDTH:].astype(BF16)
    la = _log_sigmoid(_dot(gr, wgate_ref[...]) + bgate_ref[...]) / GLA_TAU

    r, c = _tri_mask(tt)
    sel = (((r // GLA_CHUNK) == (c // GLA_CHUNK)) & (r >= c)).astype(BF16)
    bcum = jnp.concatenate(
        [_dot_sel_lhs(sel, la[i * tt:(i + 1) * tt]) for i in range(rows)], axis=0)
    blast = jnp.concatenate(
        [jnp.broadcast_to(bcum[(ci + 1) * GLA_CHUNK - 1:(ci + 1) * GLA_CHUNK, :], (GLA_CHUNK, hk))
         for ci in range(rows * nchunk)], axis=0)
    qd = (q * jnp.exp(bcum)).astype(BF16)
    kd = (k * jnp.exp(-bcum)).astype(BF16)
    kend = (k * jnp.exp(blast - bcum)).astype(BF16)
    eblast = jnp.exp(blast)

    lane_k = lax.broadcasted_iota(jnp.int32, (GLA_CHUNK, hk), 1) // GLA_DK
    lane_v = lax.broadcasted_iota(jnp.int32, (GLA_CHUNK, GLA_WIDTH), 1) // GLA_DV
    ar, ac = _tri_mask(GLA_HEADS * GLA_CHUNK, GLA_CHUNK)
    att_causal = (ar % GLA_CHUNK) >= ac
    sr, sc = _tri_mask(GLA_WIDTH, hk)
    diag = (sr // GLA_DV) == (sc // GLA_DK)

    units = [(i, ci) for ci in range(nchunk) for i in range(rows)]
    sl = {(i, ci): slice(i * tt + ci * GLA_CHUNK, i * tt + (ci + 1) * GLA_CHUNK) for i, ci in units}
    att, upd, o_intra = {}, {}, {}
    for un in units:
        qd_c = qd[sl[un]]
        lhs = jnp.concatenate([jnp.where(lane_k == hh, qd_c, jnp.zeros_like(qd_c))
                               for hh in range(GLA_HEADS)], axis=0)
        att[un] = jnp.where(att_causal, _dot_nt(lhs, kd[sl[un]]), 0.0).astype(BF16)
        upd[un] = jnp.where(diag, _dot_tn(v[sl[un]], kend[sl[un]]), 0.0)
    for un in units:
        res = _dot(att[un], v[sl[un]])
        acc = jnp.zeros((GLA_CHUNK, GLA_WIDTH), F32)
        for hh in range(GLA_HEADS):
            acc = jnp.where(lane_v == hh, res[hh * GLA_CHUNK:(hh + 1) * GLA_CHUNK], acc)
        o_intra[un] = acc
    outs = {}
    for un in units:
        i = un[0]
        s_prev = st[i]
        outs[un] = o_intra[un] + _dot_nt(qd[sl[un]], s_prev.astype(BF16))
        st[i] = s_prev * eblast[sl[un].start:sl[un].start + 1, :] + upd[un]

    o = jnp.concatenate([outs[(i, ci)] for i in range(rows) for ci in range(nchunk)], axis=0)
    ms = _dot_sel_rhs(o * o, _group_mean_matrix(GLA_WIDTH, GLA_DV), parts=2)
    o = o * lax.rsqrt(ms + EPS) * gn_ref[...]
    o_ref[...] = (o * _silu(gg)).astype(o_ref.dtype).reshape(rows, tt, GLA_WIDTH)

    @pl.when(t == pl.num_programs(1) - 1)
    def _():
        snew_ref[...] = st[...]


def _gla_call(x, nm, w, wgate, bgate, gn, s0, tt, rows):
    bsz, t, _ = x.shape
    tt = _row_tile(t, tt)
    rows = _row_tile(bsz, rows)
    hk = GLA_HEADS * GLA_DK
    bt = lambda w_: pl.BlockSpec((rows, tt, w_), lambda b, i: (b, i, 0))
    per_b = pl.BlockSpec((rows, GLA_WIDTH, hk), lambda b, i: (b, 0, 0))
    return pl.pallas_call(
        functools.partial(_gla_body, tt=tt, rows=rows), grid=(bsz // rows, t // tt),
        in_specs=[bt(D_MODEL), _const_spec((1, D_MODEL)), _const_spec((D_MODEL, GLA_COLS), single=True),
                  _const_spec((LANES, hk)), _const_spec((1, hk)), _const_spec((1, GLA_WIDTH)), per_b],
        out_specs=[bt(GLA_WIDTH), per_b],
        out_shape=[jax.ShapeDtypeStruct((bsz, t, GLA_WIDTH), BF16),
                   jax.ShapeDtypeStruct((bsz, GLA_WIDTH, hk), F32)],
        scratch_shapes=[pltpu.VMEM((rows, GLA_WIDTH, hk), F32)],
        compiler_params=_params(("parallel", "arbitrary"), 32),
        name="gla",
    )(x, nm, w, wgate, bgate, gn, s0)


def _fox_proj_body(x_ref, nm_ref, w_ref, qn_ref, kn_ref, fb_ref, pa_ref, arow_ref,
                   k_ref, v_ref, lf_ref, *rest, tt, attn_layout):
    if attn_layout:
        qt_ref, kp_ref, vt_ref, carry = rest
    else:
        qs_ref, carry = rest
    t = pl.program_id(1)

    @pl.when(t == 0)
    def _():
        carry[...] = jnp.zeros_like(carry)

    h = _rms(x_ref[...], nm_ref[...]).astype(BF16)
    u = _dot(h, w_ref[...])
    fq = u[:, 0:FOX_WIDTH]
    fk = u[:, FOX_WIDTH:2 * FOX_WIDTH]
    fv = u[:, 2 * FOX_WIDTH:3 * FOX_WIDTH]
    ff = u[:, 3 * FOX_WIDTH:]

    gmat = _group_mean_matrix(FOX_WIDTH, FOX_HEAD_DIM)
    qn = fq * lax.rsqrt(_dot_sel_rhs(fq * fq, gmat, parts=2) + EPS) * qn_ref[...]
    kn = fk * lax.rsqrt(_dot_sel_rhs(fk * fk, gmat, parts=2) + EPS) * kn_ref[...]
    lf = _log_sigmoid(ff + fb_ref[...])
    k_ref[...] = kn
    v_ref[...] = fv
    lf_ref[...] = lf.T[0:SUBLANES] if attn_layout else lf
    qs = (qn * (FOX_HEAD_DIM ** -0.5)).astype(BF16)

    if not attn_layout:
        qs_ref[...] = qs
        return

    r, c = _tri_mask(tt)
    cum = carry[...] + _dot_sel_lhs((r >= c).astype(BF16), lf)
    carry[...] = cum[tt - 1:tt, :]
    c3 = jnp.concatenate(_split_bf16(cum, AUX_PARTS), axis=1)
    aux = _dot(c3, pa_ref[...])
    aux_k = (aux[:, 0:LANES] + arow_ref[0:1, :]).astype(BF16)
    aux_q_all = aux[:, LANES:]
    knb = kn.astype(BF16)
    fv_t = fv.T
    ones = jnp.ones((FOX_VT_ROWS - FOX_HEAD_DIM, tt), F32)
    lane = lax.broadcasted_iota(jnp.int32, (tt, LANES), 1)
    for p in range(FOX_HEADS // 2):
        sl = slice(p * LANES, (p + 1) * LANES)
        kp_ref[p, :, 0:LANES] = knb[:, sl]
        kp_ref[p, :, LANES:] = aux_k
        for e in range(2):
            hh = 2 * p + e
            vt = jnp.concatenate([fv_t[hh * FOX_HEAD_DIM:(hh + 1) * FOX_HEAD_DIM, :], ones], axis=0).astype(BF16)
            for j in range(tt // FOX_KB):
                vt_ref[hh, j] = vt[:, j * FOX_KB:(j + 1) * FOX_KB]
            qmask = jnp.where((lane // FOX_HEAD_DIM) == e, qs[:, sl], jnp.zeros_like(qs[:, sl])).astype(F32)
            aux_q = jnp.where((lane // AUX_PARTS) == hh, aux_q_all, 0.0) + arow_ref[1 + hh:2 + hh, :]
            q_t = jnp.concatenate([qmask.T, aux_q.T], axis=0).astype(BF16)
            for j in range(tt // FOX_QB):
                qt_ref[hh, j] = q_t[:, j * FOX_QB:(j + 1) * FOX_QB]


def _fox_aux_constants():
    nslot = FOX_HEADS * AUX_PARTS
    pa = np.zeros((AUX_PARTS * LANES, 2 * LANES), np.float32)
    arow = np.zeros((SUBLANES, LANES), np.float32)
    for hh in range(FOX_HEADS):
        for j in range(AUX_PARTS):
            pa[j * LANES + hh, LANES + AUX_PARTS * hh + j] = 1.0
            pa[j * LANES + hh, nslot + AUX_PARTS * hh + j] = -1.0
            arow[0, AUX_PARTS * hh + j] = 1.0
            arow[1 + hh, nslot + AUX_PARTS * hh + j] = 1.0
    return jnp.asarray(pa, BF16), jnp.asarray(arow, F32)


def _fox_proj_call(x, nm, w, qn, kn, fb, tt, attn_layout):
    bsz, t, _ = x.shape
    tt = _row_tile(t, tt)
    pa, arow = _fox_aux_constants()
    bt = lambda w_: pl.BlockSpec((None, tt, w_), lambda b, i: (b, i, 0))
    bht = lambda n_, w_: pl.BlockSpec((None, n_, tt, w_), lambda b, i: (b, 0, i, 0))
    out_specs = [bt(FOX_WIDTH), bt(FOX_WIDTH)]
    out_shape = [jax.ShapeDtypeStruct((bsz, t, FOX_WIDTH), F32),
                 jax.ShapeDtypeStruct((bsz, t, FOX_WIDTH), F32)]
    if attn_layout:
        out_specs += [pl.BlockSpec((None, SUBLANES, tt), lambda b, i: (b, 0, i))]
        out_shape += [jax.ShapeDtypeStruct((bsz, SUBLANES, t), F32)]
    else:
        out_specs += [bt(LANES)]
        out_shape += [jax.ShapeDtypeStruct((bsz, t, LANES), F32)]
    if attn_layout:
        tiled = lambda n_, r_, w_: pl.BlockSpec((None, FOX_HEADS, tt // n_, r_, w_), lambda b, i: (b, 0, i, 0, 0))
        out_specs += [tiled(FOX_QB, 2 * LANES, FOX_QB), bht(FOX_HEADS // 2, 2 * LANES),
                      tiled(FOX_KB, FOX_VT_ROWS, FOX_KB)]
        out_shape += [jax.ShapeDtypeStruct((bsz, FOX_HEADS, t // FOX_QB, 2 * LANES, FOX_QB), BF16),
                      jax.ShapeDtypeStruct((bsz, FOX_HEADS // 2, t, 2 * LANES), BF16),
                      jax.ShapeDtypeStruct((bsz, FOX_HEADS, t // FOX_KB, FOX_VT_ROWS, FOX_KB), BF16)]
    else:
        out_specs += [bt(FOX_WIDTH)]
        out_shape += [jax.ShapeDtypeStruct((bsz, t, FOX_WIDTH), BF16)]
    return pl.pallas_call(
        functools.partial(_fox_proj_body, tt=tt, attn_layout=attn_layout), grid=(bsz, t // tt),
        in_specs=[bt(D_MODEL), _const_spec((1, D_MODEL)), _const_spec((D_MODEL, FOX_COLS), single=True),
                  _const_spec((1, FOX_WIDTH)), _const_spec((1, FOX_WIDTH)), _const_spec((1, LANES)),
                  _const_spec(pa.shape), _const_spec(arow.shape)],
        out_specs=out_specs, out_shape=out_shape,
        scratch_shapes=[pltpu.VMEM((1, LANES), F32)],
        compiler_params=_params(("parallel", "arbitrary"), 32),
        name="fox_proj",
    )(x, nm, w, qn, kn, fb, pa, arow)


def _fox_prompt_body(qt_ref, k_ref, vt_ref, o_ref, m_s, acc_s, s_s, *, nsub):
    qi = pl.program_id(2)
    qtile = nsub * FOX_QB
    chains = [(si, e) for si in range(nsub) for e in range(2)]
    m_s[...] = jnp.full_like(m_s, -jnp.inf)
    acc_s[...] = jnp.zeros_like(acc_s)

    nch = len(chains)

    def scores(c, k0, nk, masked):
        si, e = chains[c]
        s = _dot(k_ref[pl.ds(k0, nk), :], qt_ref[e, si])
        if masked:
            r, col = _tri_mask(FOX_QB)
            tail = jnp.where(r <= col, s[nk - FOX_QB:], -jnp.inf)
            s = tail if nk == FOX_QB else jnp.concatenate([s[:nk - FOX_QB], tail], axis=0)
        return s

    def absorb(c, s, vt0):
        e = chains[c][1]
        m_old = m_s[c]
        m_new = jnp.maximum(m_old, jnp.max(s, axis=0, keepdims=True))
        alpha = jnp.exp(m_old - m_new)
        p = jnp.exp(s - m_new).astype(BF16)
        vts = [vt_ref[e, vt0 + j] for j in range(s.shape[0] // FOX_KB)]
        vt = vts[0] if len(vts) == 1 else jnp.concatenate(vts, axis=1)
        acc_s[c] = alpha * acc_s[c] + _dot(vt, p)
        m_s[c] = m_new

    def sweep(k0, vt0, nks, masked):
        for c in range(nch):
            s_s[c, 0:nks[c], :] = scores(c, k0, nks[c], masked)
        for c in range(nch):
            absorb(c, s_s[c, 0:nks[c], :], vt0)

    def below_diagonal(i, carry):
        sweep(pl.multiple_of(i * FOX_SWEEP, FOX_SWEEP), i * (FOX_SWEEP // FOX_KB), [FOX_SWEEP] * nch, False)
        return carry

    lax.fori_loop(0, qi * (qtile // FOX_SWEEP), below_diagonal, 0)
    sweep(pl.multiple_of(qi * qtile, qtile), qi * (qtile // FOX_KB),
          [(si + 1) * FOX_QB for si, _ in chains], True)

    for si in range(nsub):
        outs = []
        for e in range(2):
            acc = acc_s[chains.index((si, e))]
            outs.append(acc[0:FOX_HEAD_DIM] / acc[FOX_HEAD_DIM:FOX_HEAD_DIM + 1])
        o = jnp.concatenate(outs, axis=0).T
        o_ref[si * FOX_QB:(si + 1) * FOX_QB, :] = o.astype(o_ref.dtype)


def _fox_prompt_call(qt, kp, vt, tq):
    bsz, _, nqb, _, _ = qt.shape
    t = kp.shape[2]
    nsub = _row_tile(nqb, max(FOX_SWEEP // FOX_QB, tq // FOX_QB))
    npair = FOX_HEADS // 2
    nch = 2 * nsub
    return pl.pallas_call(
        functools.partial(_fox_prompt_body, nsub=nsub), grid=(bsz, npair, nqb // nsub),
        in_specs=[pl.BlockSpec((None, 2, nsub, 2 * LANES, FOX_QB), lambda b, p, i: (b, p, i, 0, 0)),
                  pl.BlockSpec((None, None, t, 2 * LANES), lambda b, p, i: (b, p, 0, 0)),
                  pl.BlockSpec((None, 2, t // FOX_KB, FOX_VT_ROWS, FOX_KB), lambda b, p, i: (b, p, 0, 0, 0))],
        out_specs=pl.BlockSpec((None, nsub * FOX_QB, LANES), lambda b, p, i: (b, i, p)),
        out_shape=jax.ShapeDtypeStruct((bsz, t, FOX_WIDTH), BF16),
        scratch_shapes=[pltpu.VMEM((nch, 1, FOX_QB), F32),
                        pltpu.VMEM((nch, FOX_VT_ROWS, FOX_QB), F32),
                        pltpu.VMEM((nch, nsub * FOX_QB, FOX_QB), F32)],
        compiler_params=_params(("parallel", "parallel", "arbitrary"), 40),
        name="fox_attn_prompt",
    )(qt, kp, vt)


def _fox_sample_body(q_ref, kn_ref, vn_ref, lfn_ref, kc_ref, vc_ref, lfp_ref, o_ref, *, tn, past, seg):
    r, c = _tri_mask(seg)
    upper = (r <= c).astype(BF16)
    carry = jnp.zeros((FOX_HEADS, 1), F32)
    cps = []
    for j in range(past // seg):
        cs = carry + _dot_sel_rhs(lfp_ref[:, j * seg:(j + 1) * seg], upper)
        cps.append(cs)
        carry = cs[:, seg - 1:seg]
    cp = jnp.concatenate(cps, axis=1) if len(cps) > 1 else cps[0]
    lfn = lfn_ref[...]
    if tn < LANES:
        lfn = jnp.concatenate([lfn, jnp.zeros((LANES - tn, LANES), F32)], axis=0)
    r, c = _tri_mask(LANES)
    cn = _dot_sel_lhs((r >= c).astype(BF16), lfn)
    cn_t = cn.T

    q = q_ref[...]
    kn = kn_ref[...].astype(BF16)
    vn = vn_ref[...].astype(BF16)
    r, c = _tri_mask(tn)
    outs = []
    for hh in range(FOX_HEADS):
        hs = slice(hh * FOX_HEAD_DIM, (hh + 1) * FOX_HEAD_DIM)
        qh = q[:, hs]
        cq = cn[0:tn, hh:hh + 1]
        s_past = _dot(qh, kc_ref[hh].astype(BF16)) + ((carry[hh:hh + 1, :] + cq) - cp[hh:hh + 1, :])
        s_new = _dot_nt(qh, kn[:, hs]) + (cq - cn_t[hh:hh + 1, 0:tn])
        s_new = jnp.where(r >= c, s_new, -jnp.inf)
        m = jnp.maximum(jnp.max(s_past, axis=-1, keepdims=True), jnp.max(s_new, axis=-1, keepdims=True))
        p_past = jnp.exp(s_past - m)
        p_new = jnp.exp(s_new - m)
        denom = jnp.sum(p_past, axis=-1, keepdims=True) + jnp.sum(p_new, axis=-1, keepdims=True)
        o = _dot_nt(p_past.astype(BF16), vc_ref[hh].astype(BF16)) + _dot(p_new.astype(BF16), vn[:, hs])
        outs.append(o / denom)
    o_ref[...] = jnp.concatenate(outs, axis=1).astype(o_ref.dtype)


def _fox_sample_call(q, kn, vn, lfn, cache_k, cache_v, lfp_t, layer):
    bsz, tn, _ = q.shape
    past = cache_k.shape[-1]
    seg = _row_tile(past, 512)
    bt = lambda w_: pl.BlockSpec((None, tn, w_), lambda b: (b, 0, 0))
    cache = pl.BlockSpec((None, None, FOX_HEADS, FOX_HEAD_DIM, past), lambda b: (layer, b, 0, 0, 0))
    return pl.pallas_call(
        functools.partial(_fox_sample_body, tn=tn, past=past, seg=seg), grid=(bsz,),
        in_specs=[bt(FOX_WIDTH), bt(FOX_WIDTH), bt(FOX_WIDTH), bt(LANES), cache, cache,
                  pl.BlockSpec((None, None, FOX_HEADS, past), lambda b: (layer, b, 0, 0))],
        out_specs=bt(FOX_WIDTH),
        out_shape=jax.ShapeDtypeStruct((bsz, tn, FOX_WIDTH), BF16),
        compiler_params=_params(("parallel",), 48),
        name="fox_attn_sample",
    )(q, kn, vn, lfn, cache_k, cache_v, lfp_t)


def _pad_cols(a, width):
    return jnp.pad(a, ((0, 0), (0, width - a.shape[1])))


def _layer_params(l, norm_ffn1, w1_gate, w1_up, w1_down, norm_mix, w_in, ssd_conv_w, ssd_conv_b,
                  ssd_dt_bias, ssd_a_log, ssd_d, ssd_norm, gla_w_gate, gla_b_gate, gla_norm,
                  fox_q_norm, fox_k_norm, fox_f_bias, w_out, norm_ffn2, w2_gate, w2_up, w2_down):
    wi = w_in[l]
    row = lambda a: a.reshape(1, -1).astype(F32)
    hk = GLA_HEADS * GLA_DK
    p = dict(
        ffn1=(row(norm_ffn1[l]), w1_gate[l].astype(BF16), w1_up[l].astype(BF16), w1_down[l].astype(BF16)),
        ffn2=(row(norm_ffn2[l]), w2_gate[l].astype(BF16), w2_up[l].astype(BF16), w2_down[l].astype(BF16)),
        norm_mix=row(norm_mix[l]),
        w_out=w_out[l].astype(BF16),
        w_ssd=jnp.concatenate([wi[:, _OFF_Z:_OFF_DT], _pad_cols(wi[:, _OFF_DT:_OFF_GQ], LANES)], axis=1).astype(BF16),
        conv_w=ssd_conv_w[l].astype(F32),
        conv_b=row(ssd_conv_b[l]),
        dt_bias=_pad_cols(row(ssd_dt_bias[l]), LANES),
        a_log=_pad_cols(row(ssd_a_log[l]), LANES),
        d_exp=row(jnp.repeat(ssd_d[l], SSD_HEAD_DIM)),
        ssd_norm=row(ssd_norm[l]),
        w_gla=jnp.concatenate([wi[:, _OFF_GQ:_OFF_GR], _pad_cols(wi[:, _OFF_GR:_OFF_FQ], LANES)], axis=1).astype(BF16),
        w_gate=jnp.pad(gla_w_gate[l], ((0, LANES - GLA_RANK), (0, 0))).astype(BF16),
        b_gate=row(gla_b_gate[l]),
        gla_norm=row(jnp.tile(gla_norm[l], GLA_HEADS)),
        w_fox=jnp.concatenate([wi[:, _OFF_FQ:_OFF_FF], _pad_cols(wi[:, _OFF_FF:], LANES)], axis=1).astype(BF16),
        q_norm=row(jnp.tile(fox_q_norm[l], FOX_HEADS)),
        k_norm=row(jnp.tile(fox_k_norm[l], FOX_HEADS)),
        f_bias=_pad_cols(row(fox_f_bias[l]), LANES),
    )
    return p


def _gla_state_in(s):
    bsz = s.shape[0]
    eye = jnp.eye(GLA_HEADS, dtype=s.dtype)
    full = jnp.einsum('bhkv,hg->bhvgk', s, eye)
    return full.reshape(bsz, GLA_HEADS * GLA_DV, GLA_HEADS * GLA_DK)


def _gla_state_out(st):
    bsz = st.shape[0]
    full = st.reshape(bsz, GLA_HEADS, GLA_DV, GLA_HEADS, GLA_DK)
    idx = jnp.arange(GLA_HEADS)
    diag = full[:, idx, :, idx, :]
    return jnp.transpose(diag, (1, 0, 3, 2))


def _mix(x, p, conv_prev, ssd_h0, gla_s0, fox_cache, layer, tiles):
    bsz, t, _ = x.shape
    cprev = jnp.pad(conv_prev, ((0, 0), (SUBLANES - (SSD_CONV - 1), 0), (0, 0)))
    y_ssd, cnew, hnew = _ssd_call(x, p['norm_mix'], p['w_ssd'], p['conv_w'], p['conv_b'], p['dt_bias'],
                                  p['a_log'], p['d_exp'], p['ssd_norm'], cprev,
                                  ssd_h0.reshape(bsz, SSD_WIDTH, SSD_STATE), tiles['ssd'], tiles['ssd_rows'])
    o_gla, snew = _gla_call(x, p['norm_mix'], p['w_gla'], p['w_gate'], p['b_gate'], p['gla_norm'],
                            _gla_state_in(gla_s0), tiles['gla'], tiles['gla_rows'])
    if fox_cache is None:
        k, v, lf_t, qt, kp, vt = _fox_proj_call(x, p['norm_mix'], p['w_fox'], p['q_norm'], p['k_norm'],
                                                p['f_bias'], tiles['fox_proj'], True)
        o_fox = _fox_prompt_call(qt, kp, vt, tiles['fox_q'])
        lf_state = jnp.swapaxes(lf_t[:, 0:FOX_HEADS, :], 1, 2)
    else:
        k, v, lf, qs = _fox_proj_call(x, p['norm_mix'], p['w_fox'], p['q_norm'], p['k_norm'],
                                      p['f_bias'], tiles['fox_proj'], False)
        cache_k, cache_v, lfp_t = fox_cache
        o_fox = _fox_sample_call(qs, k, v, lf, cache_k, cache_v, lfp_t, layer)
        lf_state = lf[:, :, 0:FOX_HEADS]
    state = (cnew[:, SUBLANES - (SSD_CONV - 1):, :],
             hnew.reshape(bsz, SSD_HEADS, SSD_HEAD_DIM, SSD_STATE),
             _gla_state_out(snew),
             k.reshape(bsz, t, FOX_HEADS, FOX_HEAD_DIM),
             v.reshape(bsz, t, FOX_HEADS, FOX_HEAD_DIM),
             lf_state)
    return (y_ssd, o_gla, o_fox), state


def _trunk_layer(x, p, conv_prev, ssd_h0, gla_s0, fox_cache, layer, tiles):
    bsz, t, d = x.shape
    x1 = _ffn_call(x.reshape(bsz * t, d), *p['ffn1'], tiles['ffn']).reshape(bsz, t, d)
    (y_ssd, o_gla, o_fox), state = _mix(x1, p, conv_prev, ssd_h0, gla_s0, fox_cache, layer, tiles)
    flat = lambda a: a.reshape(bsz * t, a.shape[-1])
    x3 = _out_ffn_call(flat(x1), flat(y_ssd), flat(o_gla), flat(o_fox), p['w_out'], *p['ffn2'], tiles['ffn'])
    return x3.reshape(bsz, t, d), state


PROMPT_TILES = dict(ffn=512, ssd=256, ssd_rows=2, gla=256, gla_rows=4, fox_proj=512, fox_q=1024)
SAMPLE_TILES = dict(ffn=512, ssd=64, ssd_rows=4, gla=64, gla_rows=4, fox_proj=64, fox_q=64)


def kernel(x_prompt, x_sample, state_ssd_conv, state_ssd, state_gla, cache_fox_k, cache_fox_v, cache_fox_logf, norm_ffn1, w1_gate, w1_up, w1_down, norm_mix, w_in, ssd_conv_w, ssd_conv_b, ssd_dt_bias, ssd_a_log, ssd_d, ssd_norm, gla_w_gate, gla_b_gate, gla_norm, fox_q_norm, fox_k_norm, fox_f_bias, w_out, norm_ffn2, w2_gate, w2_up, w2_down):
    depth = w_in.shape[0]
    bp = x_prompt.shape[0]
    weights = (norm_ffn1, w1_gate, w1_up, w1_down, norm_mix, w_in, ssd_conv_w, ssd_conv_b, ssd_dt_bias,
               ssd_a_log, ssd_d, ssd_norm, gla_w_gate, gla_b_gate, gla_norm, fox_q_norm, fox_k_norm,
               fox_f_bias, w_out, norm_ffn2, w2_gate, w2_up, w2_down)
    cache_k = jnp.transpose(cache_fox_k, (0, 1, 3, 4, 2))
    cache_v = jnp.transpose(cache_fox_v, (0, 1, 3, 4, 2))
    lfp_t = jnp.swapaxes(cache_fox_logf, 2, 3)
    zeros_conv = jnp.zeros((bp, SSD_CONV - 1, SSD_CONV_CH), F32)
    zeros_ssd = jnp.zeros((bp, SSD_HEADS, SSD_HEAD_DIM, SSD_STATE), F32)
    zeros_gla = jnp.zeros((bp, GLA_HEADS, GLA_DK, GLA_DV), F32)

    xp, xs = x_prompt, x_sample
    p_new = [[] for _ in range(6)]
    s_new = [[] for _ in range(6)]
    for l in range(depth):
        p = _layer_params(l, *weights)
        xp, st_p = _trunk_layer(xp, p, zeros_conv, zeros_ssd, zeros_gla, None, l, PROMPT_TILES)
        xs, st_s = _trunk_layer(xs, p, state_ssd_conv[l], state_ssd[l], state_gla[l],
                                (cache_k, cache_v, lfp_t), l, SAMPLE_TILES)
        for i in range(6):
            p_new[i].append(st_p[i])
            s_new[i].append(st_s[i])
    outs_p = [jnp.stack(a) for a in p_new]
    outs_s = [jnp.stack(a) for a in s_new]
    return (xp, xs, *outs_p, *outs_s)
```

```python
import functools

import numpy as np
import jax
import jax.numpy as jnp
from jax import lax
from jax.experimental import pallas as pl
from jax.experimental.pallas import tpu as pltpu

F32 = jnp.float32
BF16 = jnp.bfloat16

EPS = 1e-6
D_MODEL = 1024
D_FF = 2816
SSD_HEADS = 8
SSD_HEAD_DIM = 64
SSD_WIDTH = 512
SSD_GROUPS = 2
SSD_STATE = 128
SSD_CONV = 4
SSD_CONV_CH = 1024
GLA_HEADS = 4
GLA_DK = 32
GLA_DV = 64
GLA_WIDTH = 256
GLA_RANK = 16
GLA_TAU = 16.0
GLA_CHUNK = 64
FOX_HEADS = 4
FOX_HEAD_DIM = 64
FOX_WIDTH = 256
LANES = 128
SUBLANES = 8
MIB = 1024 * 1024

_OFF_Z, _OFF_XBC, _OFF_DT = 0, 512, 1536
_OFF_GQ, _OFF_GR = 1544, 2312
_OFF_FQ, _OFF_FF = 2328, 3096
SSD_COLS = SSD_WIDTH + SSD_CONV_CH + LANES
GLA_COLS = 2 * GLA_HEADS * GLA_DK + 2 * GLA_WIDTH + LANES
FOX_COLS = 3 * FOX_WIDTH + LANES
AUX_PARTS = 3
FOX_QB = 256
FOX_KB = 256
FOX_SWEEP = 1024
FOX_VT_ROWS = FOX_HEAD_DIM + 16


def _dot(a, b):
    return jnp.dot(a, b, preferred_element_type=F32)


def _dot_nt(a, b):
    return lax.dot_general(a, b, (((1,), (1,)), ((), ())), preferred_element_type=F32)


def _dot_tn(a, b):
    return lax.dot_general(a, b, (((0,), (0,)), ((), ())), preferred_element_type=F32)


def _split_bf16(x, parts):
    out = []
    r = x
    for i in range(parts):
        p = r.astype(BF16)
        out.append(p)
        if i + 1 < parts:
            r = r - p.astype(F32)
    return out


def _dot_sel_lhs(sel, x, parts=3):
    n = x.shape[1]
    t = _dot(sel, jnp.concatenate(_split_bf16(x, parts), axis=1))
    acc = t[:, 0:n]
    for i in range(1, parts):
        acc = acc + t[:, i * n:(i + 1) * n]
    return acc


def _dot_sel_rhs(x, sel, parts=3):
    m = x.shape[0]
    pieces = _split_bf16(x, parts)
    if m % (2 * SUBLANES):
        acc = _dot(pieces[0], sel)
        for p in pieces[1:]:
            acc = acc + _dot(p, sel)
        return acc
    t = _dot(jnp.concatenate(pieces, axis=0), sel)
    acc = t[0:m]
    for i in range(1, parts):
        acc = acc + t[i * m:(i + 1) * m]
    return acc


def _rms(x, w):
    ms = jnp.mean(x * x, axis=-1, keepdims=True)
    return x * lax.rsqrt(ms + EPS) * w


def _silu(x):
    return x * jax.nn.sigmoid(x)


def _softplus(x):
    return jnp.maximum(x, 0.0) + jnp.log1p(jnp.exp(-jnp.abs(x)))


def _log_sigmoid(x):
    return -_softplus(-x)


def _tri_mask(n, m=None):
    m = n if m is None else m
    r = lax.broadcasted_iota(jnp.int32, (n, m), 0)
    c = lax.broadcasted_iota(jnp.int32, (n, m), 1)
    return r, c


def _lane_pair_select(a_even, a_odd):
    lane = lax.broadcasted_iota(jnp.int32, a_even.shape, 1)
    return jnp.where(lane < 64, a_even, a_odd)


def _expand_heads(v, heads, rows):
    pieces = []
    for i in range(0, len(heads), 2):
        a = jnp.broadcast_to(v[:, heads[i]:heads[i] + 1], (rows, LANES))
        b = jnp.broadcast_to(v[:, heads[i + 1]:heads[i + 1] + 1], (rows, LANES))
        pieces.append(_lane_pair_select(a, b))
    return pieces[0] if len(pieces) == 1 else jnp.concatenate(pieces, axis=1)


def _group_mean_matrix(width, group):
    r, c = _tri_mask(width)
    return jnp.where((r // group) == (c // group), 1.0 / group, 0.0).astype(BF16)


def _params(sem, vmem_mib):
    return pltpu.CompilerParams(dimension_semantics=sem, vmem_limit_bytes=vmem_mib * MIB)


def _const_spec(shape, single=False):
    nd = len(shape)
    if single:
        return pl.BlockSpec(shape, lambda *_: (0,) * nd, pipeline_mode=pl.Buffered(1))
    return pl.BlockSpec(shape, lambda *_: (0,) * nd)


def _swiglu_half(x, g_ref, wg_ref, wu_ref, wd_ref):
    h = _rms(x, g_ref[...]).astype(BF16)
    gate = _dot(h, wg_ref[...])
    up = _dot(h, wu_ref[...])
    a = (_silu(gate) * up).astype(BF16)
    return x + 0.5 * _dot(a, wd_ref[...])


def _ffn_body(x_ref, g_ref, wg_ref, wu_ref, wd_ref, o_ref):
    o_ref[...] = _swiglu_half(x_ref[...], g_ref, wg_ref, wu_ref, wd_ref)


def _out_ffn_body(x_ref, ys_ref, og_ref, of_ref, wo_ref, g_ref, wg_ref, wu_ref, wd_ref, o_ref):
    x = x_ref[...]
    x = x + (_dot(ys_ref[...], wo_ref[0:SSD_WIDTH, :])
             + _dot(og_ref[...], wo_ref[SSD_WIDTH:SSD_WIDTH + GLA_WIDTH, :])
             + _dot(of_ref[...], wo_ref[SSD_WIDTH + GLA_WIDTH:, :]))
    o_ref[...] = _swiglu_half(x, g_ref, wg_ref, wu_ref, wd_ref)


def _ffn_weight_specs():
    return [_const_spec((1, D_MODEL)),
            _const_spec((D_MODEL, D_FF), single=True),
            _const_spec((D_MODEL, D_FF), single=True),
            _const_spec((D_FF, D_MODEL), single=True)]


def _row_tile(n, want):
    t = min(want, n)
    while n % t:
        t //= 2
    return t


def _ffn_call(x, g, wg, wu, wd, tm):
    n = x.shape[0]
    tm = _row_tile(n, tm)
    row = pl.BlockSpec((tm, D_MODEL), lambda i: (i, 0))
    return pl.pallas_call(
        _ffn_body, grid=(n // tm,),
        in_specs=[row] + _ffn_weight_specs(),
        out_specs=row,
        out_shape=jax.ShapeDtypeStruct((n, D_MODEL), F32),
        compiler_params=_params(("parallel",), 56),
        name="ffn",
    )(x, g, wg, wu, wd)


def _out_ffn_call(x, ys, og, of, wo, g, wg, wu, wd, tm):
    n = x.shape[0]
    tm = _row_tile(n, tm)
    row = lambda w: pl.BlockSpec((tm, w), lambda i: (i, 0))
    return pl.pallas_call(
        _out_ffn_body, grid=(n // tm,),
        in_specs=[row(D_MODEL), row(SSD_WIDTH), row(GLA_WIDTH), row(FOX_WIDTH),
                  _const_spec((D_MODEL, D_MODEL), single=True)] + _ffn_weight_specs(),
        out_specs=row(D_MODEL),
        out_shape=jax.ShapeDtypeStruct((n, D_MODEL), F32),
        compiler_params=_params(("parallel",), 56),
        name="out_ffn",
    )(x, ys, og, of, wo, g, wg, wu, wd)


def _ssd_body(x_ref, nm_ref, w_ref, cw_ref, cb_ref, dtb_ref, alog_ref, dexp_ref, nw_ref,
              cprev_ref, h0_ref, y_ref, cnew_ref, hnew_ref, xbuf, hst, *, tt, rows):
    t = pl.program_id(1)

    @pl.when(t == 0)
    def _():
        xbuf[...] = cprev_ref[...]
        hst[...] = h0_ref[...]

    x = x_ref[...].reshape(rows * tt, D_MODEL)
    h = _rms(x, nm_ref[...]).astype(BF16)
    u = _dot(h, w_ref[...])
    z = u[:, 0:SSD_WIDTH]
    xbc = u[:, SSD_WIDTH:SSD_WIDTH + SSD_CONV_CH]
    dt_raw = u[:, SSD_WIDTH + SSD_CONV_CH:]

    convs = []
    row8 = lax.broadcasted_iota(jnp.int32, (SUBLANES, SSD_CONV_CH), 0)
    for i in range(rows):
        xi = xbc[i * tt:(i + 1) * tt]
        prev = xbuf[i]
        conv = None
        for j in range(SSD_CONV - 1, 0, -1):
            rolled = pltpu.roll(xi, j, 0)
            head = jnp.where(row8 < j, pltpu.roll(prev, j, 0), rolled[0:SUBLANES])
            term = jnp.concatenate([head, rolled[SUBLANES:]], axis=0) * cw_ref[SSD_CONV - 1 - j:SSD_CONV - j, :]
            conv = term if conv is None else conv + term
        convs.append(conv + xi * cw_ref[SSD_CONV - 1:SSD_CONV, :])
        xbuf[i] = xi[tt - SUBLANES:, :]
    xa = _silu(jnp.concatenate(convs, axis=0) + cb_ref[...])

    xs = xa[:, 0:SSD_WIDTH]
    bm = xa[:, SSD_WIDTH:SSD_WIDTH + SSD_GROUPS * SSD_STATE].astype(BF16)
    cm = xa[:, SSD_WIDTH + SSD_GROUPS * SSD_STATE:].astype(BF16)
    dt = _softplus(dt_raw + dtb_ref[...])
    a = dt * (-jnp.exp(alog_ref[...]))

    r, c = _tri_mask(tt)
    causal = r >= c
    tri = causal.astype(BF16)
    acum = jnp.concatenate(
        [_dot_sel_lhs(tri, a[i * tt:(i + 1) * tt]) for i in range(rows)], axis=0)
    shift_t = (acum - jnp.log(dt)).T
    alast = jnp.concatenate(
        [jnp.broadcast_to(acum[(i + 1) * tt - 1:(i + 1) * tt, :], (tt, LANES)) for i in range(rows)], axis=0)
    ea = jnp.exp(acum)
    wend = jnp.exp(alast - acum) * dt

    heads_per_group = SSD_HEADS // SSD_GROUPS
    gw = heads_per_group * SSD_HEAD_DIM
    heads_of = [list(range(g * heads_per_group, (g + 1) * heads_per_group)) for g in range(SSD_GROUPS)]
    nrow = rows * tt
    xb = xs.astype(BF16)
    xw =jnp.concatenate([xs[:, g * gw:(g + 1) * gw] * _expand_heads(wend, heads_of[g], nrow)
                          for g in range(SSD_GROUPS)], axis=1).astype(BF16)
    ea_x = jnp.concatenate([_expand_heads(ea, heads_of[g], nrow) for g in range(SSD_GROUPS)], axis=1)

    units = [(i, g) for g in range(SSD_GROUPS) for i in range(rows)]
    rs = {i: slice(i * tt, (i + 1) * tt) for i in range(rows)}
    gs = {g: slice(g * SSD_STATE, (g + 1) * SSD_STATE) for g in range(SSD_GROUPS)}
    cbs = {(i, g): _dot_nt(cm[rs[i], gs[g]], bm[rs[i], gs[g]]) for i, g in units}
    upds = {(i, g): _dot_tn(xw[rs[i], g * gw:(g + 1) * gw], bm[rs[i], gs[g]]) for i, g in units}
    y_unit = {}
    for i, g in units:
        heads = heads_of[g]
        intra = []
        for pi in range(heads_per_group // 2):
            ys = []
            for e in range(2):
                hh = heads[2 * pi + e]
                seg = acum[rs[i], hh:hh + 1] - shift_t[hh:hh + 1, rs[i]]
                m = (cbs[(i, g)] * jnp.where(causal, jnp.exp(seg), 0.0)).astype(BF16)
                col = g * gw + pi * LANES
                ys.append(_dot(m, xb[rs[i], col:col + LANES]))
            intra.append(_lane_pair_select(ys[0], ys[1]))
        hg = hst[i, g * gw:(g + 1) * gw, :]
        y_inter = _dot_nt(cm[rs[i], gs[g]], hg.astype(BF16)) * ea_x[rs[i], g * gw:(g + 1) * gw]
        y_unit[(i, g)] = jnp.concatenate(intra, axis=1) + y_inter
        cd = jnp.concatenate(
            [jnp.broadcast_to(jnp.exp(acum[(i + 1) * tt - 1:(i + 1) * tt, hh:hh + 1]), (SSD_HEAD_DIM, SSD_STATE))
             for hh in heads], axis=0)
        hst[i, g * gw:(g + 1) * gw, :] = hg * cd + upds[(i, g)]

    y = jnp.concatenate(
        [jnp.concatenate([y_unit[(i, g)] for g in range(SSD_GROUPS)], axis=1) for i in range(rows)], axis=0)
    y = (y + dexp_ref[...] * xs) * _silu(z)
    nw = nw_ref[...]
    outs = [_rms(y[:, g * gw:(g + 1) * gw], nw[:, g * gw:(g + 1) * gw]) for g in range(SSD_GROUPS)]
    y_ref[...] = jnp.concatenate(outs, axis=1).astype(y_ref.dtype).reshape(rows, tt, SSD_WIDTH)

    @pl.when(t == pl.num_programs(1) - 1)
    def _():
        cnew_ref[...] = xbuf[...]
        hnew_ref[...] = hst[...]


def _ssd_call(x, nm, w, cw, cb, dtb, alog, dexp, nw, cprev, h0, tt, rows):
    bsz, t, _ = x.shape
    tt = _row_tile(t, tt)
    rows = _row_tile(bsz, rows)
    bt = lambda w_: pl.BlockSpec((rows, tt, w_), lambda b, i: (b, i, 0))
    per_b = lambda r_, w_: pl.BlockSpec((rows, r_, w_), lambda b, i: (b, 0, 0))
    return pl.pallas_call(
        functools.partial(_ssd_body, tt=tt, rows=rows), grid=(bsz // rows, t // tt),
        in_specs=[bt(D_MODEL), _const_spec((1, D_MODEL)), _const_spec((D_MODEL, SSD_COLS), single=True),
                  _const_spec((SSD_CONV, SSD_CONV_CH)), _const_spec((1, SSD_CONV_CH)),
                  _const_spec((1, LANES)), _const_spec((1, LANES)),
                  _const_spec((1, SSD_WIDTH)), _const_spec((1, SSD_WIDTH)),
                  per_b(SUBLANES, SSD_CONV_CH), per_b(SSD_WIDTH, SSD_STATE)],
        out_specs=[bt(SSD_WIDTH), per_b(SUBLANES, SSD_CONV_CH), per_b(SSD_WIDTH, SSD_STATE)],
        out_shape=[jax.ShapeDtypeStruct((bsz, t, SSD_WIDTH), BF16),
                   jax.ShapeDtypeStruct((bsz, SUBLANES, SSD_CONV_CH), F32),
                   jax.ShapeDtypeStruct((bsz, SSD_WIDTH, SSD_STATE), F32)],
        scratch_shapes=[pltpu.VMEM((rows, SUBLANES, SSD_CONV_CH), F32),
                        pltpu.VMEM((rows, SSD_WIDTH, SSD_STATE), F32)],
        compiler_params=_params(("parallel", "arbitrary"), 48),
        name="ssd",
    )(x, nm, w, cw, cb, dtb, alog, dexp, nw, cprev, h0)


def _gla_body(x_ref, nm_ref, w_ref, wgate_ref, bgate_ref, gn_ref, s0_ref, o_ref, snew_ref, st, *, tt, rows):
    t = pl.program_id(1)

    @pl.when(t == 0)
    def _():
        st[...] = s0_ref[...]

    hk = GLA_HEADS * GLA_DK
    nchunk = tt // GLA_CHUNK
    x = x_ref[...].reshape(rows * tt, D_MODEL)
    h = _rms(x, nm_ref[...]).astype(BF16)
    u = _dot(h, w_ref[...])
    q = u[:, 0:hk] * (GLA_DK ** -0.5)
    k = u[:, hk:2 * hk]
    v = u[:, 2 * hk:2 * hk + GLA_WIDTH].astype(BF16)
    gg = u[:, 2 * hk + GLA_WIDTH:2 * hk + 2 * GLA_WIDTH]
    gr = u[:, 2 * hk + 2 * GLA_WIDTH:].astype(BF16)
    la = _log_sigmoid(_dot(gr, wgate_ref[...]) + bgate_ref[...]) / GLA_TAU

    r, c = _tri_mask(tt)
    sel = (((r // GLA_CHUNK) == (c // GLA_CHUNK)) & (r >= c)).astype(BF16)
    bcum = jnp.concatenate(
        [_dot_sel_lhs(sel, la[i * tt:(i + 1) * tt]) for i in range(rows)], axis=0)
    blast = jnp.concatenate(
        [jnp.broadcast_to(bcum[(ci + 1) * GLA_CHUNK - 1:(ci + 1) * GLA_CHUNK, :], (GLA_CHUNK, hk))
         for ci in range(rows * nchunk)], axis=0)
    qd = (q * jnp.exp(bcum)).astype(BF16)
    kd = (k * jnp.exp(-bcum)).astype(BF16)
    kend = (k * jnp.exp(blast - bcum)).astype(BF16)
    eblast = jnp.exp(blast)

    lane_k = lax.broadcasted_iota(jnp.int32, (GLA_CHUNK, hk), 1) // GLA_DK
    lane_v = lax.broadcasted_iota(jnp.int32, (GLA_CHUNK, GLA_WIDTH), 1) // GLA_DV
    ar, ac = _tri_mask(GLA_HEADS * GLA_CHUNK, GLA_CHUNK)
    att_causal = (ar % GLA_CHUNK) >= ac
    sr, sc = _tri_mask(GLA_WIDTH, hk)
    diag = (sr // GLA_DV) == (sc // GLA_DK)

    units = [(i, ci) for ci in range(nchunk) for i in range(rows)]
    sl = {(i, ci): slice(i * tt + ci * GLA_CHUNK, i * tt + (ci + 1) * GLA_CHUNK) for i, ci in units}
    att, upd, o_intra = {}, {}, {}
    for un in units:
        qd_c = qd[sl[un]]
        lhs = jnp.concatenate([jnp.where(lane_k == hh, qd_c, jnp.zeros_like(qd_c))
                               for hh in range(GLA_HEADS)], axis=0)
        att[un] = jnp.where(att_causal, _dot_nt(lhs, kd[sl[un]]), 0.0).astype(BF16)
        upd[un] = jnp.where(diag, _dot_tn(v[sl[un]], kend[sl[un]]), 0.0)
    for un in units:
        res = _dot(att[un], v[sl[un]])
        acc = jnp.zeros((GLA_CHUNK, GLA_WIDTH), F32)
        for hh in range(GLA_HEADS):
            acc = jnp.where(lane_v == hh, res[hh * GLA_CHUNK:(hh + 1) * GLA_CHUNK], acc)
        o_intra[un] = acc
    outs = {}
    for un in units:
        i = un[0]
        s_prev = st[i]
        outs[un] = o_intra[un] + _dot_nt(qd[sl[un]], s_prev.astype(BF16))
        st[i] = s_prev * eblast[sl[un].start:sl[un].start + 1, :] + upd[un]

    o = jnp.concatenate([outs[(i, ci)] for i in range(rows) for ci in range(nchunk)], axis=0)
    ms = _dot_sel_rhs(o * o, _group_mean_matrix(GLA_WIDTH, GLA_DV), parts=2)
    o = o * lax.rsqrt(ms + EPS) * gn_ref[...]
    o_ref[...] = (o * _silu(gg)).astype(o_ref.dtype).reshape(rows, tt, GLA_WIDTH)

    @pl.when(t == pl.num_programs(1) - 1)
    def _():
        snew_ref[...] = st[...]


def _gla_call(x, nm, w, wgate, bgate, gn, s0, tt, rows):
    bsz, t, _ = x.shape
    tt = _row_tile(t, tt)
    rows = _row_tile(bsz, rows)
    hk = GLA_HEADS * GLA_DK
    bt = lambda w_: pl.BlockSpec((rows, tt, w_), lambda b, i: (b, i, 0))
    per_b = pl.BlockSpec((rows, GLA_WIDTH, hk), lambda b, i: (b, 0, 0))
    return pl.pallas_call(
        functools.partial(_gla_body, tt=tt, rows=rows), grid=(bsz // rows, t // tt),
        in_specs=[bt(D_MODEL), _const_spec((1, D_MODEL)), _const_spec((D_MODEL, GLA_COLS), single=True),
                  _const_spec((LANES, hk)), _const_spec((1, hk)), _const_spec((1, GLA_WIDTH)), per_b],
        out_specs=[bt(GLA_WIDTH), per_b],
        out_shape=[jax.ShapeDtypeStruct((bsz, t, GLA_WIDTH), BF16),
                   jax.ShapeDtypeStruct((bsz, GLA_WIDTH, hk), F32)],
        scratch_shapes=[pltpu.VMEM((rows, GLA_WIDTH, hk), F32)],
        compiler_params=_params(("parallel", "arbitrary"), 32),
        name="gla",
    )(x, nm, w, wgate, bgate, gn, s0)


def _fox_proj_body(x_ref, nm_ref, w_ref, qn_ref, kn_ref, fb_ref, pa_ref, arow_ref,
                   k_ref, v_ref, lf_ref, *rest, tt, rows, attn_layout):
    if attn_layout:
        qt_ref, kp_ref, vt_ref, carry = rest
    else:
        qs_ref, carry = rest
    t = pl.program_id(1)

    @pl.when(t == 0)
    def _():
        carry[...] = jnp.zeros_like(carry)

    nrow = rows * tt
    h = _rms(x_ref[...].reshape(nrow, D_MODEL), nm_ref[...]).astype(BF16)
    u = _dot(h, w_ref[...])
    fq = u[:, 0:FOX_WIDTH]
    fk = u[:, FOX_WIDTH:2 * FOX_WIDTH]
    fv = u[:, 2 * FOX_WIDTH:3 * FOX_WIDTH]
    ff = u[:, 3 * FOX_WIDTH:]

    gmat = _group_mean_matrix(FOX_WIDTH, FOX_HEAD_DIM)
    qn = fq * lax.rsqrt(_dot_sel_rhs(fq * fq, gmat, parts=2) + EPS) * qn_ref[...]
    kn = fk * lax.rsqrt(_dot_sel_rhs(fk * fk, gmat, parts=2) + EPS) * kn_ref[...]
    lf = _log_sigmoid(ff + fb_ref[...])
    k_ref[...] = kn.reshape(rows, tt, FOX_WIDTH)
    v_ref[...] = fv.reshape(rows, tt, FOX_WIDTH)
    qs = (qn * (FOX_HEAD_DIM ** -0.5)).astype(BF16)

    if not attn_layout:
        lf_ref[...] = lf.reshape(rows, tt, LANES)
        qs_ref[...] = qs.reshape(rows, tt, FOX_WIDTH)
        return

    r, c = _tri_mask(tt)
    tri = (r >= c).astype(BF16)
    cums = []
    for i in range(rows):
        lf_i = lf[i * tt:(i + 1) * tt]
        lf_ref[i] = lf_i.T[0:SUBLANES]
        cum_i = carry[i] + _dot_sel_lhs(tri, lf_i)
        carry[i] = cum_i[tt - 1:tt, :]
        cums.append(cum_i)
    cum = jnp.concatenate(cums, axis=0) if rows > 1 else cums[0]
    c3 = jnp.concatenate(_split_bf16(cum, AUX_PARTS), axis=1)
    aux = _dot(c3, pa_ref[...])
    aux_k = (aux[:, 0:LANES] + arow_ref[0:1, :]).astype(BF16)
    aux_q_all = aux[:, LANES:]
    knb = kn.astype(BF16)
    fv_t = fv.T
    ones = jnp.ones((FOX_VT_ROWS - FOX_HEAD_DIM, nrow), F32)
    lane = lax.broadcasted_iota(jnp.int32, (nrow, LANES), 1)
    for p in range(FOX_HEADS // 2):
        sl = slice(p * LANES, (p + 1) * LANES)
        for i in range(rows):
            kp_ref[i, p, :, 0:LANES] = knb[i * tt:(i + 1) * tt, sl]
            kp_ref[i, p, :, LANES:] = aux_k[i * tt:(i + 1) * tt]
        for e in range(2):
            hh = 2 * p + e
            vt = jnp.concatenate([fv_t[hh * FOX_HEAD_DIM:(hh + 1) * FOX_HEAD_DIM, :], ones], axis=0).astype(BF16)
            qmask = jnp.where((lane // FOX_HEAD_DIM) == e, qs[:, sl], jnp.zeros_like(qs[:, sl])).astype(F32)
            aux_q = jnp.where((lane // AUX_PARTS) == hh, aux_q_all, 0.0) + arow_ref[1 + hh:2 + hh, :]
            q_t = jnp.concatenate([qmask.T, aux_q.T], axis=0).astype(BF16)
            for i in range(rows):
                for j in range(tt // FOX_KB):
                    vt_ref[i, hh, j] = vt[:, i * tt + j * FOX_KB:i * tt + (j + 1) * FOX_KB]
                for j in range(tt // FOX_QB):
                    qt_ref[i, hh, j] = q_t[:, i * tt + j * FOX_QB:i * tt + (j + 1) * FOX_QB]


def _fox_aux_constants():
    nslot = FOX_HEADS * AUX_PARTS
    pa = np.zeros((AUX_PARTS * LANES, 2 * LANES), np.float32)
    arow = np.zeros((SUBLANES, LANES), np.float32)
    for hh in range(FOX_HEADS):
        for j in range(AUX_PARTS):
            pa[j * LANES + hh, LANES + AUX_PARTS * hh + j] = 1.0
            pa[j * LANES + hh, nslot + AUX_PARTS * hh + j] = -1.0
            arow[0, AUX_PARTS * hh + j] = 1.0
            arow[1 + hh, nslot + AUX_PARTS * hh + j] = 1.0
    return jnp.asarray(pa, BF16), jnp.asarray(arow, F32)


def _fox_proj_call(x, nm, w, qn, kn, fb, tt, rows, attn_layout):
    bsz, t, _ = x.shape
    tt = _row_tile(t, tt)
    rows = _row_tile(bsz, rows)
    pa, arow = _fox_aux_constants()
    bt = lambda w_: pl.BlockSpec((rows, tt, w_), lambda b, i: (b, i, 0))
    bht = lambda n_, w_: pl.BlockSpec((rows, n_, tt, w_), lambda b, i: (b, 0, i, 0))
    out_specs = [bt(FOX_WIDTH), bt(FOX_WIDTH)]
    out_shape = [jax.ShapeDtypeStruct((bsz, t, FOX_WIDTH), F32),
                 jax.ShapeDtypeStruct((bsz, t, FOX_WIDTH), F32)]
    if attn_layout:
        out_specs += [pl.BlockSpec((rows, SUBLANES, tt), lambda b, i: (b, 0, i))]
        out_shape += [jax.ShapeDtypeStruct((bsz, SUBLANES, t), F32)]
    else:
        out_specs += [bt(LANES)]
        out_shape += [jax.ShapeDtypeStruct((bsz, t, LANES), F32)]
    if attn_layout:
        tiled = lambda n_, r_, w_: pl.BlockSpec((rows, FOX_HEADS, tt // n_, r_, w_), lambda b, i: (b, 0, i, 0, 0))
        out_specs += [tiled(FOX_QB, 2 * LANES, FOX_QB), bht(FOX_HEADS // 2, 2 * LANES),
                      tiled(FOX_KB, FOX_VT_ROWS, FOX_KB)]
        out_shape += [jax.ShapeDtypeStruct((bsz, FOX_HEADS, t // FOX_QB, 2 * LANES, FOX_QB), BF16),
                      jax.ShapeDtypeStruct((bsz, FOX_HEADS // 2, t, 2 * LANES), BF16),
                      jax.ShapeDtypeStruct((bsz, FOX_HEADS, t // FOX_KB, FOX_VT_ROWS, FOX_KB), BF16)]
    else:
        out_specs += [bt(FOX_WIDTH)]
        out_shape += [jax.ShapeDtypeStruct((bsz, t, FOX_WIDTH), BF16)]
    return pl.pallas_call(
        functools.partial(_fox_proj_body, tt=tt, rows=rows, attn_layout=attn_layout),
        grid=(bsz // rows, t // tt),
        in_specs=[bt(D_MODEL), _const_spec((1, D_MODEL)), _const_spec((D_MODEL, FOX_COLS), single=True),
                  _const_spec((1, FOX_WIDTH)), _const_spec((1, FOX_WIDTH)), _const_spec((1, LANES)),
                  _const_spec(pa.shape), _const_spec(arow.shape)],
        out_specs=out_specs, out_shape=out_shape,
        scratch_shapes=[pltpu.VMEM((rows, 1, LANES), F32)],
        compiler_params=_params(("parallel", "arbitrary"), 40),
        name="fox_proj",
    )(x, nm, w, qn, kn, fb, pa, arow)


def _fox_prompt_body(qt_ref, k_ref, vt_ref, o_ref, m_s, acc_s, s_s, *, nsub):
    qi = pl.program_id(2)
    qtile = nsub * FOX_QB
    chains = [(si, e) for si in range(nsub) for e in range(2)]
    m_s[...] = jnp.full_like(m_s, -jnp.inf)
    acc_s[...] = jnp.zeros_like(acc_s)

    nch = len(chains)

    def scores(c, k0, nk, masked):
        si, e = chains[c]
        s = _dot(k_ref[pl.ds(k0, nk), :], qt_ref[e, si])
        if masked:
            r, col = _tri_mask(FOX_QB)
            tail = jnp.where(r <= col, s[nk - FOX_QB:], -jnp.inf)
            s = tail if nk == FOX_QB else jnp.concatenate([s[:nk - FOX_QB], tail], axis=0)
        return s

    def absorb(c, s, vt0):
        e = chains[c][1]
        m_old = m_s[c]
        m_new = jnp.maximum(m_old, jnp.max(s, axis=0, keepdims=True))
        alpha = jnp.exp(m_old - m_new)
        p = jnp.exp(s - m_new).astype(BF16)
        vts = [vt_ref[e, vt0 + j] for j in range(s.shape[0] // FOX_KB)]
        vt = vts[0] if len(vts) == 1 else jnp.concatenate(vts, axis=1)
        acc_s[c] = alpha * acc_s[c] + _dot(vt, p)
        m_s[c] = m_new

    def sweep(k0, vt0, nks, masked):
        for c in range(nch):
            s_s[c, 0:nks[c], :] = scores(c, k0, nks[c], masked)
        for c in range(nch):
            absorb(c, s_s[c, 0:nks[c], :], vt0)

    def below_diagonal(i, carry):
        sweep(pl.multiple_of(i * FOX_SWEEP, FOX_SWEEP), i * (FOX_SWEEP // FOX_KB), [FOX_SWEEP] * nch, False)
        return carry

    lax.fori_loop(0, qi * (qtile // FOX_SWEEP), below_diagonal, 0)
    sweep(pl.multiple_of(qi * qtile, qtile), qi * (qtile // FOX_KB),
          [(si + 1) * FOX_QB for si, _ in chains], True)

    for si in range(nsub):
        outs = []
        for e in range(2):
            acc = acc_s[chains.index((si, e))]
            outs.append(acc[0:FOX_HEAD_DIM] / acc[FOX_HEAD_DIM:FOX_HEAD_DIM + 1])
        o = jnp.concatenate(outs, axis=0).T
        o_ref[si * FOX_QB:(si + 1) * FOX_QB, :] = o.astype(o_ref.dtype)


def _fox_prompt_call(qt, kp, vt, tq):
    bsz, _, nqb, _, _ = qt.shape
    t = kp.shape[2]
    nsub = _row_tile(nqb, max(FOX_SWEEP // FOX_QB, tq // FOX_QB))
    npair = FOX_HEADS // 2
    nch = 2 * nsub
    return pl.pallas_call(
        functools.partial(_fox_prompt_body, nsub=nsub), grid=(bsz, npair, nqb // nsub),
        in_specs=[pl.BlockSpec((None, 2, nsub, 2 * LANES, FOX_QB), lambda b, p, i: (b, p, i, 0, 0)),
                  pl.BlockSpec((None, None, t, 2 * LANES), lambda b, p, i: (b, p, 0, 0)),
                  pl.BlockSpec((None, 2, t // FOX_KB, FOX_VT_ROWS, FOX_KB), lambda b, p, i: (b, p, 0, 0, 0))],
        out_specs=pl.BlockSpec((None, nsub * FOX_QB, LANES), lambda b, p, i: (b, i, p)),
        out_shape=jax.ShapeDtypeStruct((bsz, t, FOX_WIDTH), BF16),
        scratch_shapes=[pltpu.VMEM((nch, 1, FOX_QB), F32),
                        pltpu.VMEM((nch, FOX_VT_ROWS, FOX_QB), F32),
                        pltpu.VMEM((nch, nsub * FOX_QB, FOX_QB), F32)],
        compiler_params=_params(("parallel", "parallel", "arbitrary"), 40),
        name="fox_attn_prompt",
    )(qt, kp, vt)


def _fox_sample_body(q_ref, kn_ref, vn_ref, lfn_ref, kc_ref, vc_ref, lfp_ref, o_ref, *, tn, past, seg):
    r, c = _tri_mask(seg)
    upper = (r <= c).astype(BF16)
    carry = jnp.zeros((FOX_HEADS, 1), F32)
    cps = []
    for j in range(past // seg):
        cs = carry + _dot_sel_rhs(lfp_ref[:, j * seg:(j + 1) * seg], upper)
        cps.append(cs)
        carry = cs[:, seg - 1:seg]
    cp = jnp.concatenate(cps, axis=1) if len(cps) > 1 else cps[0]
    lfn = lfn_ref[...]
    if tn < LANES:
        lfn = jnp.concatenate([lfn, jnp.zeros((LANES - tn, LANES), F32)], axis=0)
    r, c = _tri_mask(LANES)
    cn = _dot_sel_lhs((r >= c).astype(BF16), lfn)
    cn_t = cn.T

    q = q_ref[...]
    kn = kn_ref[...].astype(BF16)
    vn = vn_ref[...].astype(BF16)
    r, c = _tri_mask(tn)
    outs = []
    for hh in range(FOX_HEADS):
        hs = slice(hh * FOX_HEAD_DIM, (hh + 1) * FOX_HEAD_DIM)
        qh = q[:, hs]
        cq = cn[0:tn, hh:hh + 1]
        s_past = _dot(qh, kc_ref[hh].astype(BF16)) + ((carry[hh:hh + 1, :] + cq) - cp[hh:hh + 1, :])
        s_new = _dot_nt(qh, kn[:, hs]) + (cq - cn_t[hh:hh + 1, 0:tn])
        s_new = jnp.where(r >= c, s_new, -jnp.inf)
        m = jnp.maximum(jnp.max(s_past, axis=-1, keepdims=True), jnp.max(s_new, axis=-1, keepdims=True))
        p_past = jnp.exp(s_past - m)
        p_new = jnp.exp(s_new - m)
        denom = jnp.sum(p_past, axis=-1, keepdims=True) + jnp.sum(p_new, axis=-1, keepdims=True)
        o = _dot_nt(p_past.astype(BF16), vc_ref[hh].astype(BF16)) + _dot(p_new.astype(BF16), vn[:, hs])
        outs.append(o / denom)
    o_ref[...] = jnp.concatenate(outs, axis=1).astype(o_ref.dtype)


def _fox_sample_call(q, kn, vn, lfn, cache_k, cache_v, lfp_t, layer):
    bsz, tn, _ = q.shape
    past = cache_k.shape[-1]
    seg = _row_tile(past, 512)
    bt = lambda w_: pl.BlockSpec((None, tn, w_), lambda b: (b, 0, 0))
    cache = pl.BlockSpec((None, None, FOX_HEADS, FOX_HEAD_DIM, past), lambda b: (layer, b, 0, 0, 0))
    return pl.pallas_call(
        functools.partial(_fox_sample_body, tn=tn, past=past, seg=seg), grid=(bsz,),
        in_specs=[bt(FOX_WIDTH), bt(FOX_WIDTH), bt(FOX_WIDTH), bt(LANES), cache, cache,
                  pl.BlockSpec((None, None, FOX_HEADS, past), lambda b: (layer, b, 0, 0))],
        out_specs=bt(FOX_WIDTH),
        out_shape=jax.ShapeDtypeStruct((bsz, tn, FOX_WIDTH), BF16),
        compiler_params=_params(("parallel",), 48),
        name="fox_attn_sample",
    )(q, kn, vn, lfn, cache_k, cache_v, lfp_t)


def _pad_cols(a, width):
    return jnp.pad(a, ((0, 0), (0, width - a.shape[1])))


def _layer_params(l, norm_ffn1, w1_gate, w1_up, w1_down, norm_mix, w_in, ssd_conv_w, ssd_conv_b,
                  ssd_dt_bias, ssd_a_log, ssd_d, ssd_norm, gla_w_gate, gla_b_gate, gla_norm,
                  fox_q_norm, fox_k_norm, fox_f_bias, w_out, norm_ffn2, w2_gate, w2_up, w2_down):
    wi = w_in[l]
    row = lambda a: a.reshape(1, -1).astype(F32)
    hk = GLA_HEADS * GLA_DK
    p = dict(
        ffn1=(row(norm_ffn1[l]), w1_gate[l].astype(BF16), w1_up[l].astype(BF16), w1_down[l].astype(BF16)),
        ffn2=(row(norm_ffn2[l]), w2_gate[l].astype(BF16), w2_up[l].astype(BF16), w2_down[l].astype(BF16)),
        norm_mix=row(norm_mix[l]),
        w_out=w_out[l].astype(BF16),
        w_ssd=jnp.concatenate([wi[:, _OFF_Z:_OFF_DT], _pad_cols(wi[:, _OFF_DT:_OFF_GQ], LANES)], axis=1).astype(BF16),
        conv_w=ssd_conv_w[l].astype(F32),
        conv_b=row(ssd_conv_b[l]),
        dt_bias=_pad_cols(row(ssd_dt_bias[l]), LANES),
        a_log=_pad_cols(row(ssd_a_log[l]), LANES),
        d_exp=row(jnp.repeat(ssd_d[l], SSD_HEAD_DIM)),
        ssd_norm=row(ssd_norm[l]),
        w_gla=jnp.concatenate([wi[:, _OFF_GQ:_OFF_GR], _pad_cols(wi[:, _OFF_GR:_OFF_FQ], LANES)], axis=1).astype(BF16),
        w_gate=jnp.pad(gla_w_gate[l], ((0, LANES - GLA_RANK), (0, 0))).astype(BF16),
        b_gate=row(gla_b_gate[l]),
        gla_norm=row(jnp.tile(gla_norm[l], GLA_HEADS)),
        w_fox=jnp.concatenate([wi[:, _OFF_FQ:_OFF_FF], _pad_cols(wi[:, _OFF_FF:], LANES)], axis=1).astype(BF16),
        q_norm=row(jnp.tile(fox_q_norm[l], FOX_HEADS)),
        k_norm=row(jnp.tile(fox_k_norm[l], FOX_HEADS)),
        f_bias=_pad_cols(row(fox_f_bias[l]), LANES),
    )
    return p


def _gla_state_in(s):
    bsz = s.shape[0]
    eye = jnp.eye(GLA_HEADS, dtype=s.dtype)
    full = jnp.einsum('bhkv,hg->bhvgk', s, eye)
    return full.reshape(bsz, GLA_HEADS * GLA_DV, GLA_HEADS * GLA_DK)


def _gla_state_out(st):
    bsz = st.shape[0]
    full = st.reshape(bsz, GLA_HEADS, GLA_DV, GLA_HEADS, GLA_DK)
    idx = jnp.arange(GLA_HEADS)
    diag = full[:, idx, :, idx, :]
    return jnp.transpose(diag, (1, 0, 3, 2))


def _mix(x, p, conv_prev, ssd_h0, gla_s0, fox_cache, layer, tiles):
    bsz, t, _ = x.shape
    cprev = jnp.pad(conv_prev, ((0, 0), (SUBLANES - (SSD_CONV - 1), 0), (0, 0)))
    y_ssd, cnew, hnew = _ssd_call(x, p['norm_mix'], p['w_ssd'], p['conv_w'], p['conv_b'], p['dt_bias'],
                                  p['a_log'], p['d_exp'], p['ssd_norm'], cprev,
                                  ssd_h0.reshape(bsz, SSD_WIDTH, SSD_STATE), tiles['ssd'], tiles['ssd_rows'])
    o_gla, snew = _gla_call(x, p['norm_mix'], p['w_gla'], p['w_gate'], p['b_gate'], p['gla_norm'],
                            _gla_state_in(gla_s0), tiles['gla'], tiles['gla_rows'])
    if fox_cache is None:
        k, v, lf_t, qt, kp, vt = _fox_proj_call(x, p['norm_mix'], p['w_fox'], p['q_norm'], p['k_norm'],
                                                p['f_bias'], tiles['fox_proj'], tiles['fox_rows'], True)
        o_fox = _fox_prompt_call(qt, kp, vt, tiles['fox_q'])
        lf_state = jnp.swapaxes(lf_t[:, 0:FOX_HEADS, :], 1, 2)
    else:
        k, v, lf, qs = _fox_proj_call(x, p['norm_mix'], p['w_fox'], p['q_norm'], p['k_norm'],
                                      p['f_bias'], tiles['fox_proj'], tiles['fox_rows'], False)
        cache_k, cache_v, lfp_t = fox_cache
        o_fox = _fox_sample_call(qs, k, v, lf, cache_k, cache_v, lfp_t, layer)
        lf_state = lf[:, :, 0:FOX_HEADS]
    state = (cnew[:, SUBLANES - (SSD_CONV - 1):, :],
             hnew.reshape(bsz, SSD_HEADS, SSD_HEAD_DIM, SSD_STATE),
             _gla_state_out(snew),
             k.reshape(bsz, t, FOX_HEADS, FOX_HEAD_DIM),
             v.reshape(bsz, t, FOX_HEADS, FOX_HEAD_DIM),
             lf_state)
    return (y_ssd, o_gla, o_fox), state


def _trunk_layer(x, p, conv_prev, ssd_h0, gla_s0, fox_cache, layer, tiles):
    bsz, t, d = x.shape
    x1 = _ffn_call(x.reshape(bsz * t, d), *p['ffn1'], tiles['ffn']).reshape(bsz, t, d)
    (y_ssd, o_gla, o_fox), state = _mix(x1, p, conv_prev, ssd_h0, gla_s0, fox_cache, layer, tiles)
    flat = lambda a: a.reshape(bsz * t, a.shape[-1])
    x3 = _out_ffn_call(flat(x1), flat(y_ssd), flat(o_gla), flat(o_fox), p['w_out'], *p['ffn2'], tiles['ffn'])
    return x3.reshape(bsz, t, d), state


PROMPT_TILES = dict(ffn=512, ssd=256, ssd_rows=2, gla=256, gla_rows=4, fox_proj=512, fox_rows=2,
                    fox_q=1024)
SAMPLE_TILES = dict(ffn=512, ssd=64, ssd_rows=4, gla=64, gla_rows=4, fox_proj=64, fox_rows=8, fox_q=64)


def kernel(x_prompt, x_sample, state_ssd_conv, state_ssd, state_gla, cache_fox_k, cache_fox_v, cache_fox_logf, norm_ffn1, w1_gate, w1_up, w1_down, norm_mix, w_in, ssd_conv_w, ssd_conv_b, ssd_dt_bias, ssd_a_log, ssd_d, ssd_norm, gla_w_gate, gla_b_gate, gla_norm, fox_q_norm, fox_k_norm, fox_f_bias, w_out, norm_ffn2, w2_gate, w2_up, w2_down):
    depth = w_in.shape[0]
    bp = x_prompt.shape[0]
    weights = (norm_ffn1, w1_gate, w1_up, w1_down, norm_mix, w_in, ssd_conv_w, ssd_conv_b, ssd_dt_bias,
               ssd_a_log, ssd_d, ssd_norm, gla_w_gate, gla_b_gate, gla_norm, fox_q_norm, fox_k_norm,
               fox_f_bias, w_out, norm_ffn2, w2_gate, w2_up, w2_down)
    cache_k = jnp.transpose(cache_fox_k, (0, 1, 3, 4, 2))
    cache_v = jnp.transpose(cache_fox_v, (0, 1, 3, 4, 2))
    lfp_t = jnp.swapaxes(cache_fox_logf, 2, 3)
    zeros_conv = jnp.zeros((bp, SSD_CONV - 1, SSD_CONV_CH), F32)
    zeros_ssd = jnp.zeros((bp, SSD_HEADS, SSD_HEAD_DIM, SSD_STATE), F32)
    zeros_gla = jnp.zeros((bp, GLA_HEADS, GLA_DK, GLA_DV), F32)

    xp, xs = x_prompt, x_sample
    p_new = [[] for _ in range(6)]
    s_new = [[] for _ in range(6)]
    for l in range(depth):
        p = _layer_params(l, *weights)
        xp, st_p = _trunk_layer(xp, p, zeros_conv, zeros_ssd, zeros_gla, None, l, PROMPT_TILES)
        xs, st_s = _trunk_layer(xs, p, state_ssd_conv[l], state_ssd[l], state_gla[l],
                                (cache_k, cache_v, lfp_t), l, SAMPLE_TILES)
        for i in range(6):
            p_new[i].append(st_p[i])
            s_new[i].append(st_s[i])
    outs_p = [jnp.stack(a) for a in p_new]
    outs_s = [jnp.stack(a) for a in s_new]
    return (xp, xs, *outs_p, *outs_s)
```

```python
import functools

import numpy as np
import jax
import jax.numpy as jnp
from jax import lax
from jax.experimental import pallas as pl
from jax.experimental.pallas import tpu as pltpu

F32 = jnp.float32
BF16 = jnp.bfloat16

EPS = 1e-6
D_MODEL = 1024
D_FF = 2816
SSD_HEADS = 8
SSD_HEAD_DIM = 64
SSD_WIDTH = 512
SSD_GROUPS = 2
SSD_STATE = 128
SSD_CONV = 4
SSD_CONV_CH = 1024
GLA_HEADS = 4
GLA_DK = 32
GLA_DV = 64
GLA_WIDTH = 256
GLA_RANK = 16
GLA_TAU = 16.0
GLA_CHUNK = 64
FOX_HEADS = 4
FOX_HEAD_DIM = 64
FOX_WIDTH = 256
LANES = 128
SUBLANES = 8
MIB = 1024 * 1024

_OFF_Z, _OFF_XBC, _OFF_DT = 0, 512, 1536
_OFF_GQ, _OFF_GR = 1544, 2312
_OFF_FQ, _OFF_FF = 2328, 3096
SSD_COLS = SSD_WIDTH + SSD_CONV_CH + LANES
GLA_COLS = 2 * GLA_HEADS * GLA_DK + 2 * GLA_WIDTH + LANES
FOX_COLS = 3 * FOX_WIDTH + LANES
AUX_PARTS = 3
FOX_QB = 256
FOX_KB = 256
FOX_SWEEP = 1024
FOX_VT_ROWS = FOX_HEAD_DIM + 16


def _dot(a, b):
    return jnp.dot(a, b, preferred_element_type=F32)


def _dot_nt(a, b):
    return lax.dot_general(a, b, (((1,), (1,)), ((), ())), preferred_element_type=F32)


def _dot_tn(a, b):
    return lax.dot_general(a, b, (((0,), (0,)), ((), ())), preferred_element_type=F32)


def _split_bf16(x, parts):
    out = []
    r = x
    for i in range(parts):
        p = r.astype(BF16)
        out.append(p)
        if i + 1 < parts:
            r = r - p.astype(F32)
    return out


def _dot_sel_lhs(sel, x, parts=3):
    n = x.shape[1]
    t = _dot(sel, jnp.concatenate(_split_bf16(x, parts), axis=1))
    acc = t[:, 0:n]
    for i in range(1, parts):
        acc = acc + t[:, i * n:(i + 1) * n]
    return acc


def _dot_sel_rhs(x, sel, parts=3):
    m = x.shape[0]
    pieces = _split_bf16(x, parts)
    if m % (2 * SUBLANES):
        acc = _dot(pieces[0], sel)
        for p in pieces[1:]:
            acc = acc + _dot(p, sel)
        return acc
    t = _dot(jnp.concatenate(pieces, axis=0), sel)
    acc = t[0:m]
    for i in range(1, parts):
        acc = acc + t[i * m:(i + 1) * m]
    return acc


def _rms(x, w):
    ms = jnp.mean(x * x, axis=-1, keepdims=True)
    return x * lax.rsqrt(ms + EPS) * w


def _silu(x):
    return x * jax.nn.sigmoid(x)


def _softplus(x):
    return jnp.maximum(x, 0.0) + jnp.log1p(jnp.exp(-jnp.abs(x)))


def _log_sigmoid(x):
    return -_softplus(-x)


def _tri_mask(n, m=None):
    m = n if m is None else m
    r = lax.broadcasted_iota(jnp.int32, (n, m), 0)
    c = lax.broadcasted_iota(jnp.int32, (n, m), 1)
    return r, c


def _lane_pair_select(a_even, a_odd):
    lane = lax.broadcasted_iota(jnp.int32, a_even.shape, 1)
    return jnp.where(lane < 64, a_even, a_odd)


def _expand_heads(v, heads, rows):
    pieces = []
    for i in range(0, len(heads), 2):
        a = jnp.broadcast_to(v[:, heads[i]:heads[i] + 1], (rows, LANES))
        b = jnp.broadcast_to(v[:, heads[i + 1]:heads[i + 1] + 1], (rows, LANES))
        pieces.append(_lane_pair_select(a, b))
    return pieces[0] if len(pieces) == 1 else jnp.concatenate(pieces, axis=1)


def _group_mean_matrix(width, group):
    r, c = _tri_mask(width)
    return jnp.where((r // group) == (c // group), 1.0 / group, 0.0).astype(BF16)


def _params(sem, vmem_mib):
    return pltpu.CompilerParams(dimension_semantics=sem, vmem_limit_bytes=vmem_mib * MIB)


def _const_spec(shape, single=False):
    nd = len(shape)
    if single:
        return pl.BlockSpec(shape, lambda *_: (0,) * nd, pipeline_mode=pl.Buffered(1))
    return pl.BlockSpec(shape, lambda *_: (0,) * nd)


def _swiglu_half(x, g_ref, wg_ref, wu_ref, wd_ref):
    h = _rms(x, g_ref[...]).astype(BF16)
    gate = _dot(h, wg_ref[...])
    up = _dot(h, wu_ref[...])
    a = (_silu(gate) * up).astype(BF16)
    return x + 0.5 * _dot(a, wd_ref[...])


def _ffn_body(x_ref, g_ref, wg_ref, wu_ref, wd_ref, o_ref):
    o_ref[...] = _swiglu_half(x_ref[...], g_ref, wg_ref, wu_ref, wd_ref)


def _out_ffn_body(x_ref, ys_ref, og_ref, of_ref, wo_ref, g_ref, wg_ref, wu_ref, wd_ref, o_ref):
    x = x_ref[...]
    x = x + (_dot(ys_ref[...], wo_ref[0:SSD_WIDTH, :])
             + _dot(og_ref[...], wo_ref[SSD_WIDTH:SSD_WIDTH + GLA_WIDTH, :])
             + _dot(of_ref[...], wo_ref[SSD_WIDTH + GLA_WIDTH:, :]))
    o_ref[...] = _swiglu_half(x, g_ref, wg_ref, wu_ref, wd_ref)


def _ffn_weight_specs():
    return [_const_spec((1, D_MODEL)),
            _const_spec((D_MODEL, D_FF), single=True),
            _const_spec((D_MODEL, D_FF), single=True),
            _const_spec((D_FF, D_MODEL), single=True)]


def _row_tile(n, want):
    t = min(want, n)
    while n % t:
        t //= 2
    return t


def _ffn_call(x, g, wg, wu, wd, tm):
    n = x.shape[0]
    tm = _row_tile(n, tm)
    row = pl.BlockSpec((tm, D_MODEL), lambda i: (i, 0))
    return pl.pallas_call(
        _ffn_body, grid=(n // tm,),
        in_specs=[row] + _ffn_weight_specs(),
        out_specs=row,
        out_shape=jax.ShapeDtypeStruct((n, D_MODEL), F32),
        compiler_params=_params(("parallel",), 56),
        name="ffn",
    )(x, g, wg, wu, wd)


def _out_ffn_call(x, ys, og, of, wo, g, wg, wu, wd, tm):
    n = x.shape[0]
    tm = _row_tile(n, tm)
    row = lambda w: pl.BlockSpec((tm, w), lambda i: (i, 0))
    return pl.pallas_call(
        _out_ffn_body, grid=(n // tm,),
        in_specs=[row(D_MODEL), row(SSD_WIDTH), row(GLA_WIDTH), row(FOX_WIDTH),
                  _const_spec((D_MODEL, D_MODEL), single=True)] + _ffn_weight_specs(),
        out_specs=row(D_MODEL),
        out_shape=jax.ShapeDtypeStruct((n, D_MODEL), F32),
        compiler_params=_params(("parallel",), 56),
        name="out_ffn",
    )(x, ys, og, of, wo, g, wg, wu, wd)


def _ssd_body(x_ref, nm_ref, w_ref, cw_ref, cb_ref, dtb_ref, alog_ref, dexp_ref, nw_ref,
              cprev_ref, h0_ref, y_ref, cnew_ref, hnew_ref, xbuf, hst, *, tt, rows):
    t = pl.program_id(1)

    @pl.when(t == 0)
    def _():
        xbuf[...] = cprev_ref[...]
        hst[...] = h0_ref[...]

    x = x_ref[...].reshape(rows * tt, D_MODEL)
    h = _rms(x, nm_ref[...]).astype(BF16)
    u = _dot(h, w_ref[...])
    z = u[:, 0:SSD_WIDTH]
    xbc = u[:, SSD_WIDTH:SSD_WIDTH + SSD_CONV_CH]
    dt_raw = u[:, SSD_WIDTH + SSD_CONV_CH:]

    convs = []
    row8 = lax.broadcasted_iota(jnp.int32, (SUBLANES, SSD_CONV_CH), 0)
    for i in range(rows):
        xi = xbc[i * tt:(i + 1) * tt]
        prev = xbuf[i]
        conv = None
        for j in range(SSD_CONV - 1, 0, -1):
            rolled = pltpu.roll(xi, j, 0)
            head = jnp.where(row8 < j, pltpu.roll(prev, j, 0), rolled[0:SUBLANES])
            term = jnp.concatenate([head, rolled[SUBLANES:]], axis=0) * cw_ref[SSD_CONV - 1 - j:SSD_CONV - j, :]
            conv = term if conv is None else conv + term
        convs.append(conv + xi * cw_ref[SSD_CONV - 1:SSD_CONV, :])
        xbuf[i] = xi[tt - SUBLANES:, :]
    xa = _silu(jnp.concatenate(convs, axis=0) + cb_ref[...])

    xs = xa[:, 0:SSD_WIDTH]
    bm = xa[:, SSD_WIDTH:SSD_WIDTH + SSD_GROUPS * SSD_STATE].astype(BF16)
    cm = xa[:, SSD_WIDTH + SSD_GROUPS * SSD_STATE:].astype(BF16)
    dt = _softplus(dt_raw + dtb_ref[...])
    a = dt * (-jnp.exp(alog_ref[...]))

    r, c = _tri_mask(tt)
    causal = r >= c
    tri = causal.astype(BF16)
    acum = jnp.concatenate(
        [_dot_sel_lhs(tri, a[i * tt:(i + 1) * tt]) for i in range(rows)], axis=0)
    shift_t = (acum - jnp.log(dt)).T
    alast = jnp.concatenate(
        [jnp.broadcast_to(acum[(i + 1) * tt - 1:(i + 1) * tt, :], (tt, LANES)) for i in range(rows)], axis=0)
    ea = jnp.exp(acum)
    wend = jnp.exp(alast - acum) * dt

    heads_per_group = SSD_HEADS // SSD_GROUPS
    gw = heads_per_group * SSD_HEAD_DIM
    heads_of = [list(range(g * heads_per_group, (g + 1) * heads_per_group)) for g in range(SSD_GROUPS)]
    nrow = rows * tt
    xb = xs.astype(BF16)
    xw =jnp.concatenate([xs[:, g * gw:(g + 1) * gw] * _expand_heads(wend, heads_of[g], nrow)
                          for g in range(SSD_GROUPS)], axis=1).astype(BF16)
    ea_x = jnp.concatenate([_expand_heads(ea, heads_of[g], nrow) for g in range(SSD_GROUPS)], axis=1)

    units = [(i, g) for g in range(SSD_GROUPS) for i in range(rows)]
    rs = {i: slice(i * tt, (i + 1) * tt) for i in range(rows)}
    gs = {g: slice(g * SSD_STATE, (g + 1) * SSD_STATE) for g in range(SSD_GROUPS)}
    cbs = {(i, g): _dot_nt(cm[rs[i], gs[g]], bm[rs[i], gs[g]]) for i, g in units}
    upds = {(i, g): _dot_tn(xw[rs[i], g * gw:(g + 1) * gw], bm[rs[i], gs[g]]) for i, g in units}
    y_unit = {}
    for i, g in units:
        heads = heads_of[g]
        intra = []
        for pi in range(heads_per_group // 2):
            ys = []
            for e in range(2):
                hh = heads[2 * pi + e]
                seg = acum[rs[i], hh:hh + 1] - shift_t[hh:hh + 1, rs[i]]
                m = (cbs[(i, g)] * jnp.where(causal, jnp.exp(seg), 0.0)).astype(BF16)
                col = g * gw + pi * LANES
                ys.append(_dot(m, xb[rs[i], col:col + LANES]))
            intra.append(_lane_pair_select(ys[0], ys[1]))
        hg = hst[i, g * gw:(g + 1) * gw, :]
        y_inter = _dot_nt(cm[rs[i], gs[g]], hg.astype(BF16)) * ea_x[rs[i], g * gw:(g + 1) * gw]
        y_unit[(i, g)] = jnp.concatenate(intra, axis=1) + y_inter
        cd = jnp.concatenate(
            [jnp.broadcast_to(jnp.exp(acum[(i + 1) * tt - 1:(i + 1) * tt, hh:hh + 1]), (SSD_HEAD_DIM, SSD_STATE))
             for hh in heads], axis=0)
        hst[i, g * gw:(g + 1) * gw, :] = hg * cd + upds[(i, g)]

    y = jnp.concatenate(
        [jnp.concatenate([y_unit[(i, g)] for g in range(SSD_GROUPS)], axis=1) for i in range(rows)], axis=0)
    y = (y + dexp_ref[...] * xs) * _silu(z)
    nw = nw_ref[...]
    outs = [_rms(y[:, g * gw:(g + 1) * gw], nw[:, g * gw:(g + 1) * gw]) for g in range(SSD_GROUPS)]
    y_ref[...] = jnp.concatenate(outs, axis=1).astype(y_ref.dtype).reshape(rows, tt, SSD_WIDTH)

    @pl.when(t == pl.num_programs(1) - 1)
    def _():
        cnew_ref[...] = xbuf[...]
        hnew_ref[...] = hst[...]


def _ssd_call(x, nm, w, cw, cb, dtb, alog, dexp, nw, cprev, h0, tt, rows):
    bsz, t, _ = x.shape
    tt = _row_tile(t, tt)
    rows = _row_tile(bsz, rows)
    bt = lambda w_: pl.BlockSpec((rows, tt, w_), lambda b, i: (b, i, 0))
    per_b = lambda r_, w_: pl.BlockSpec((rows, r_, w_), lambda b, i: (b, 0, 0))
    return pl.pallas_call(
        functools.partial(_ssd_body, tt=tt, rows=rows), grid=(bsz // rows, t // tt),
        in_specs=[bt(D_MODEL), _const_spec((1, D_MODEL)), _const_spec((D_MODEL, SSD_COLS), single=True),
                  _const_spec((SSD_CONV, SSD_CONV_CH)), _const_spec((1, SSD_CONV_CH)),
                  _const_spec((1, LANES)), _const_spec((1, LANES)),
                  _const_spec((1, SSD_WIDTH)), _const_spec((1, SSD_WIDTH)),
                  per_b(SUBLANES, SSD_CONV_CH), per_b(SSD_WIDTH, SSD_STATE)],
        out_specs=[bt(SSD_WIDTH), per_b(SUBLANES, SSD_CONV_CH), per_b(SSD_WIDTH, SSD_STATE)],
        out_shape=[jax.ShapeDtypeStruct((bsz, t, SSD_WIDTH), BF16),
                   jax.ShapeDtypeStruct((bsz, SUBLANES, SSD_CONV_CH), F32),
                   jax.ShapeDtypeStruct((bsz, SSD_WIDTH, SSD_STATE), F32)],
        scratch_shapes=[pltpu.VMEM((rows, SUBLANES, SSD_CONV_CH), F32),
                        pltpu.VMEM((rows, SSD_WIDTH, SSD_STATE), F32)],
        compiler_params=_params(("parallel", "arbitrary"), 48),
        name="ssd",
    )(x, nm, w, cw, cb, dtb, alog, dexp, nw, cprev, h0)


def _gla_body(x_ref, nm_ref, w_ref, wgate_ref, bgate_ref, gn_ref, s0_ref, o_ref, snew_ref, st, *, tt, rows):
    t = pl.program_id(1)

    @pl.when(t == 0)
    def _():
        st[...] = s0_ref[...]

    hk = GLA_HEADS * GLA_DK
    nchunk = tt // GLA_CHUNK
    x = x_ref[...].reshape(rows * tt, D_MODEL)
    h = _rms(x, nm_ref[...]).astype(BF16)
    u = _dot(h, w_ref[...])
    q = u[:, 0:hk] * (GLA_DK ** -0.5)
    k = u[:, hk:2 * hk]
    v = u[:, 2 * hk:2 * hk + GLA_WIDTH].astype(BF16)
    gg = u[:, 2 * hk + GLA_WIDTH:2 * hk + 2 * GLA_WIDTH]
    gr = u[:, 2 * hk + 2 * GLA_WIDTH:].astype(BF16)
    la = _log_sigmoid(_dot(gr, wgate_ref[...]) + bgate_ref[...]) / GLA_TAU

    r, c = _tri_mask(tt)
    sel = (((r // GLA_CHUNK) == (c // GLA_CHUNK)) & (r >= c)).astype(BF16)
    bcum = jnp.concatenate(
        [_dot_sel_lhs(sel, la[i * tt:(i + 1) * tt]) for i in range(rows)], axis=0)
    blast = jnp.concatenate(
        [jnp.broadcast_to(bcum[(ci + 1) * GLA_CHUNK - 1:(ci + 1) * GLA_CHUNK, :], (GLA_CHUNK, hk))
         for ci in range(rows * nchunk)], axis=0)
    qd = (q * jnp.exp(bcum)).astype(BF16)
    kd = (k * jnp.exp(-bcum)).astype(BF16)
    kend = (k * jnp.exp(blast - bcum)).astype(BF16)
    eblast = jnp.exp(blast)

    lane_k = lax.broadcasted_iota(jnp.int32, (GLA_CHUNK, hk), 1) // GLA_DK
    lane_v = lax.broadcasted_iota(jnp.int32, (GLA_CHUNK, GLA_WIDTH), 1) // GLA_DV
    ar, ac = _tri_mask(GLA_HEADS * GLA_CHUNK, GLA_CHUNK)
    att_causal = (ar % GLA_CHUNK) >= ac
    sr, sc = _tri_mask(GLA_WIDTH, hk)
    diag = (sr // GLA_DV) == (sc // GLA_DK)

    units = [(i, ci) for ci in range(nchunk) for i in range(rows)]
    sl = {(i, ci): slice(i * tt + ci * GLA_CHUNK, i * tt + (ci + 1) * GLA_CHUNK) for i, ci in units}
    att, upd, o_intra = {}, {}, {}
    for un in units:
        qd_c = qd[sl[un]]
        lhs = jnp.concatenate([jnp.where(lane_k == hh, qd_c, jnp.zeros_like(qd_c))
                               for hh in range(GLA_HEADS)], axis=0)
        att[un] = jnp.where(att_causal, _dot_nt(lhs, kd[sl[un]]), 0.0).astype(BF16)
        upd[un] = jnp.where(diag, _dot_tn(v[sl[un]], kend[sl[un]]), 0.0)
    for un in units:
        res = _dot(att[un], v[sl[un]])
        acc = jnp.zeros((GLA_CHUNK, GLA_WIDTH), F32)
        for hh in range(GLA_HEADS):
            acc = jnp.where(lane_v == hh, res[hh * GLA_CHUNK:(hh + 1) * GLA_CHUNK], acc)
        o_intra[un] = acc
    outs = {}
    for un in units:
        i = un[0]
        s_prev = st[i]
        outs[un] = o_intra[un] + _dot_nt(qd[sl[un]], s_prev.astype(BF16))
        st[i] = s_prev * eblast[sl[un].start:sl[un].start + 1, :] + upd[un]

    o = jnp.concatenate([outs[(i, ci)] for i in range(rows) for ci in range(nchunk)], axis=0)
    ms = _dot_sel_rhs(o * o, _group_mean_matrix(GLA_WIDTH, GLA_DV), parts=2)
    o = o * lax.rsqrt(ms + EPS) * gn_ref[...]
    o_ref[...] = (o * _silu(gg)).astype(o_ref.dtype).reshape(rows, tt, GLA_WIDTH)

    @pl.when(t == pl.num_programs(1) - 1)
    def _():
        snew_ref[...] = st[...]


def _gla_call(x, nm, w, wgate, bgate, gn, s0, tt, rows):
    bsz, t, _ = x.shape
    tt = _row_tile(t, tt)
    rows = _row_tile(bsz, rows)
    hk = GLA_HEADS * GLA_DK
    bt = lambda w_: pl.BlockSpec((rows, tt, w_), lambda b, i: (b, i, 0))
    per_b = pl.BlockSpec((rows, GLA_WIDTH, hk), lambda b, i: (b, 0, 0))
    return pl.pallas_call(
        functools.partial(_gla_body, tt=tt, rows=rows), grid=(bsz // rows, t // tt),
        in_specs=[bt(D_MODEL), _const_spec((1, D_MODEL)), _const_spec((D_MODEL, GLA_COLS), single=True),
                  _const_spec((LANES, hk)), _const_spec((1, hk)), _const_spec((1, GLA_WIDTH)), per_b],
        out_specs=[bt(GLA_WIDTH), per_b],
        out_shape=[jax.ShapeDtypeStruct((bsz, t, GLA_WIDTH), BF16),
                   jax.ShapeDtypeStruct((bsz, GLA_WIDTH, hk), F32)],
        scratch_shapes=[pltpu.VMEM((rows, GLA_WIDTH, hk), F32)],
        compiler_params=_params(("parallel", "arbitrary"), 32),
        name="gla",
    )(x, nm, w, wgate, bgate, gn, s0)


def _fox_proj_body(x_ref, nm_ref, w_ref, qn_ref, kn_ref, fb_ref, pa_ref, arow_ref, *rest,
                   tt, rows, attn_layout, n_alias):
    k_ref, v_ref, lf_ref, *rest = rest[n_alias:]
    if attn_layout:
        qt_ref, kp_ref, vt_ref, carry = rest
    else:
        qs_ref, carry = rest
    t = pl.program_id(1)

    @pl.when(t == 0)
    def _():
        carry[...] = jnp.zeros_like(carry)

    nrow = rows * tt
    h = _rms(x_ref[...].reshape(nrow, D_MODEL), nm_ref[...]).astype(BF16)
    u = _dot(h, w_ref[...])
    fq = u[:, 0:FOX_WIDTH]
    fk = u[:, FOX_WIDTH:2 * FOX_WIDTH]
    fv = u[:, 2 * FOX_WIDTH:3 * FOX_WIDTH]
    ff = u[:, 3 * FOX_WIDTH:]

    gmat = _group_mean_matrix(FOX_WIDTH, FOX_HEAD_DIM)
    qn = fq * lax.rsqrt(_dot_sel_rhs(fq * fq, gmat, parts=2) + EPS) * qn_ref[...]
    kn = fk * lax.rsqrt(_dot_sel_rhs(fk * fk, gmat, parts=2) + EPS) * kn_ref[...]
    lf = _log_sigmoid(ff + fb_ref[...])
    k_ref[...] = kn.reshape(rows, tt, FOX_WIDTH)
    v_ref[...] = fv.reshape(rows, tt, FOX_WIDTH)
    qs = (qn * (FOX_HEAD_DIM ** -0.5)).astype(BF16)

    if not attn_layout:
        lf_ref[...] = lf.reshape(rows, tt, LANES)
        qs_ref[...] = qs.reshape(rows, tt, FOX_WIDTH)
        return

    r, c = _tri_mask(tt)
    tri = (r >= c).astype(BF16)
    cums = []
    for i in range(rows):
        lf_i = lf[i * tt:(i + 1) * tt]
        lf_ref[i] = lf_i.T[0:SUBLANES]
        cum_i = carry[i] + _dot_sel_lhs(tri, lf_i)
        carry[i] = cum_i[tt - 1:tt, :]
        cums.append(cum_i)
    cum = jnp.concatenate(cums, axis=0) if rows > 1 else cums[0]
    c3 = jnp.concatenate(_split_bf16(cum, AUX_PARTS), axis=1)
    aux = _dot(c3, pa_ref[...])
    aux_k = (aux[:, 0:LANES] + arow_ref[0:1, :]).astype(BF16)
    aux_q_all = aux[:, LANES:]
    knb = kn.astype(BF16)
    fv_t = fv.T
    ones = jnp.ones((FOX_VT_ROWS - FOX_HEAD_DIM, nrow), F32)
    lane = lax.broadcasted_iota(jnp.int32, (nrow, LANES), 1)
    for p in range(FOX_HEADS // 2):
        sl = slice(p * LANES, (p + 1) * LANES)
        for i in range(rows):
            kp_ref[i, p, :, 0:LANES] = knb[i * tt:(i + 1) * tt, sl]
            kp_ref[i, p, :, LANES:] = aux_k[i * tt:(i + 1) * tt]
        for e in range(2):
            hh = 2 * p + e
            vt = jnp.concatenate([fv_t[hh * FOX_HEAD_DIM:(hh + 1) * FOX_HEAD_DIM, :], ones], axis=0).astype(BF16)
            qmask = jnp.where((lane // FOX_HEAD_DIM) == e, qs[:, sl], jnp.zeros_like(qs[:, sl])).astype(F32)
            aux_q = jnp.where((lane // AUX_PARTS) == hh, aux_q_all, 0.0) + arow_ref[1 + hh:2 + hh, :]
            q_t = jnp.concatenate([qmask.T, aux_q.T], axis=0).astype(BF16)
            for i in range(rows):
                for j in range(tt // FOX_KB):
                    vt_ref[i, hh, j] = vt[:, i * tt + j * FOX_KB:i * tt + (j + 1) * FOX_KB]
                for j in range(tt // FOX_QB):
                    qt_ref[i, hh, j] = q_t[:, i * tt + j * FOX_QB:i * tt + (j + 1) * FOX_QB]


def _fox_aux_constants():
    nslot = FOX_HEADS * AUX_PARTS
    pa = np.zeros((AUX_PARTS * LANES, 2 * LANES), np.float32)
    arow = np.zeros((SUBLANES, LANES), np.float32)
    for hh in range(FOX_HEADS):
        for j in range(AUX_PARTS):
            pa[j * LANES + hh, LANES + AUX_PARTS * hh + j] = 1.0
            pa[j * LANES + hh, nslot + AUX_PARTS * hh + j] = -1.0
            arow[0, AUX_PARTS * hh + j] = 1.0
            arow[1 + hh, nslot + AUX_PARTS * hh + j] = 1.0
    return jnp.asarray(pa, BF16), jnp.asarray(arow, F32)


def _fox_proj_call(x, nm, w, qn, kn, fb, tt, rows, attn_layout, layer, depth, kv_bufs):
    bsz, t, _ = x.shape
    tt = _row_tile(t, tt)
    rows = _row_tile(bsz, rows)
    pa, arow = _fox_aux_constants()
    bt = lambda w_: pl.BlockSpec((rows, tt, w_), lambda b, i: (b, i, 0))
    bht = lambda n_, w_: pl.BlockSpec((rows, n_, tt, w_), lambda b, i: (b, 0, i, 0))
    kv_spec = pl.BlockSpec((None, rows, tt, FOX_WIDTH), lambda b, i: (layer, b, i, 0))
    out_specs = [kv_spec, kv_spec]
    out_shape = [jax.ShapeDtypeStruct((depth, bsz, t, FOX_WIDTH), F32),
                 jax.ShapeDtypeStruct((depth, bsz, t, FOX_WIDTH), F32)]
    n_fixed = 8
    if kv_bufs is None:
        kv_bufs = tuple(jnp.zeros(s.shape, s.dtype) for s in out_shape)
    aliased = list(kv_bufs)
    aliases = {n_fixed + j: j for j in range(len(aliased))}
    if attn_layout:
        out_specs += [pl.BlockSpec((rows, SUBLANES, tt), lambda b, i: (b, 0, i))]
        out_shape += [jax.ShapeDtypeStruct((bsz, SUBLANES, t), F32)]
    else:
        out_specs += [bt(LANES)]
        out_shape += [jax.ShapeDtypeStruct((bsz, t, LANES), F32)]
    if attn_layout:
        tiled = lambda n_, r_, w_: pl.BlockSpec((rows, FOX_HEADS, tt // n_, r_, w_), lambda b, i: (b, 0, i, 0, 0))
        out_specs += [tiled(FOX_QB, 2 * LANES, FOX_QB), bht(FOX_HEADS // 2, 2 * LANES),
                      tiled(FOX_KB, FOX_VT_ROWS, FOX_KB)]
        out_shape += [jax.ShapeDtypeStruct((bsz, FOX_HEADS, t // FOX_QB, 2 * LANES, FOX_QB), BF16),
                      jax.ShapeDtypeStruct((bsz, FOX_HEADS // 2, t, 2 * LANES), BF16),
                      jax.ShapeDtypeStruct((bsz, FOX_HEADS, t // FOX_KB, FOX_VT_ROWS, FOX_KB), BF16)]
    else:
        out_specs += [bt(FOX_WIDTH)]
        out_shape += [jax.ShapeDtypeStruct((bsz, t, FOX_WIDTH), BF16)]
    return pl.pallas_call(
        functools.partial(_fox_proj_body, tt=tt, rows=rows, attn_layout=attn_layout, n_alias=len(aliased)),
        grid=(bsz // rows, t // tt),
        in_specs=[bt(D_MODEL), _const_spec((1, D_MODEL)), _const_spec((D_MODEL, FOX_COLS), single=True),
                  _const_spec((1, FOX_WIDTH)), _const_spec((1, FOX_WIDTH)), _const_spec((1, LANES)),
                  _const_spec(pa.shape), _const_spec(arow.shape)]
                 + [pl.BlockSpec(memory_space=pl.ANY)] * len(aliased),
        out_specs=out_specs, out_shape=out_shape,
        input_output_aliases=aliases,
        scratch_shapes=[pltpu.VMEM((rows, 1, LANES), F32)],
        compiler_params=_params(("parallel", "arbitrary"), 40),
        name="fox_proj",
    )(x, nm, w, qn, kn, fb, pa, arow, *aliased)


def _fox_prompt_body(qt_ref, k_ref, vt_ref, o_ref, m_s, acc_s, s_s, *, nsub):
    qi = pl.program_id(2)
    qtile = nsub * FOX_QB
    chains = [(si, e) for si in range(nsub) for e in range(2)]
    m_s[...] = jnp.full_like(m_s, -jnp.inf)
    acc_s[...] = jnp.zeros_like(acc_s)

    nch = len(chains)

    def scores(c, k0, nk, masked):
        si, e = chains[c]
        s = _dot(k_ref[pl.ds(k0, nk), :], qt_ref[e, si])
        if masked:
            r, col = _tri_mask(FOX_QB)
            tail = jnp.where(r <= col, s[nk - FOX_QB:], -jnp.inf)
            s = tail if nk == FOX_QB else jnp.concatenate([s[:nk - FOX_QB], tail], axis=0)
        return s

    def absorb(c, s, vt0):
        e = chains[c][1]
        m_old = m_s[c]
        m_new = jnp.maximum(m_old, jnp.max(s, axis=0, keepdims=True))
        alpha = jnp.exp(m_old - m_new)
        p = jnp.exp(s - m_new).astype(BF16)
        vts = [vt_ref[e, vt0 + j] for j in range(s.shape[0] // FOX_KB)]
        vt = vts[0] if len(vts) == 1 else jnp.concatenate(vts, axis=1)
        acc_s[c] = alpha * acc_s[c] + _dot(vt, p)
        m_s[c] = m_new

    def sweep(k0, vt0, nks, masked):
        for c in range(nch):
            s_s[c, 0:nks[c], :] = scores(c, k0, nks[c], masked)
        for c in range(nch):
            absorb(c, s_s[c, 0:nks[c], :], vt0)

    def below_diagonal(i, carry):
        sweep(pl.multiple_of(i * FOX_SWEEP, FOX_SWEEP), i * (FOX_SWEEP // FOX_KB), [FOX_SWEEP] * nch, False)
        return carry

    lax.fori_loop(0, qi * (qtile // FOX_SWEEP), below_diagonal, 0)
    sweep(pl.multiple_of(qi * qtile, qtile), qi * (qtile // FOX_KB),
          [(si + 1) * FOX_QB for si, _ in chains], True)

    for si in range(nsub):
        outs = []
        for e in range(2):
            acc = acc_s[chains.index((si, e))]
            outs.append(acc[0:FOX_HEAD_DIM] / acc[FOX_HEAD_DIM:FOX_HEAD_DIM + 1])
        o = jnp.concatenate(outs, axis=0).T
        o_ref[si * FOX_QB:(si + 1) * FOX_QB, :] = o.astype(o_ref.dtype)


def _fox_prompt_call(qt, kp, vt, tq):
    bsz, _, nqb, _, _ = qt.shape
    t = kp.shape[2]
    nsub = _row_tile(nqb, max(FOX_SWEEP // FOX_QB, tq // FOX_QB))
    npair = FOX_HEADS // 2
    nch = 2 * nsub
    return pl.pallas_call(
        functools.partial(_fox_prompt_body, nsub=nsub), grid=(bsz, npair, nqb // nsub),
        in_specs=[pl.BlockSpec((None, 2, nsub, 2 * LANES, FOX_QB), lambda b, p, i: (b, p, i, 0, 0)),
                  pl.BlockSpec((None, None, t, 2 * LANES), lambda b, p, i: (b, p, 0, 0)),
                  pl.BlockSpec((None, 2, t // FOX_KB, FOX_VT_ROWS, FOX_KB), lambda b, p, i: (b, p, 0, 0, 0))],
        out_specs=pl.BlockSpec((None, nsub * FOX_QB, LANES), lambda b, p, i: (b, i, p)),
        out_shape=jax.ShapeDtypeStruct((bsz, t, FOX_WIDTH), BF16),
        scratch_shapes=[pltpu.VMEM((nch, 1, FOX_QB), F32),
                        pltpu.VMEM((nch, FOX_VT_ROWS, FOX_QB), F32),
                        pltpu.VMEM((nch, nsub * FOX_QB, FOX_QB), F32)],
        compiler_params=_params(("parallel", "parallel", "arbitrary"), 40),
        name="fox_attn_prompt",
    )(qt, kp, vt)


def _fox_sample_body(q_ref, kn_ref, vn_ref, lfn_ref, kc_ref, vc_ref, lfp_ref, o_ref, *, tn, past, seg):
    r, c = _tri_mask(seg)
    upper = (r <= c).astype(BF16)
    carry = jnp.zeros((FOX_HEADS, 1), F32)
    cps = []
    for j in range(past // seg):
        cs = carry + _dot_sel_rhs(lfp_ref[:, j * seg:(j + 1) * seg], upper)
        cps.append(cs)
        carry = cs[:, seg - 1:seg]
    cp = jnp.concatenate(cps, axis=1) if len(cps) > 1 else cps[0]
    lfn = lfn_ref[...]
    if tn < LANES:
        lfn = jnp.concatenate([lfn, jnp.zeros((LANES - tn, LANES), F32)], axis=0)
    r, c = _tri_mask(LANES)
    cn = _dot_sel_lhs((r >= c).astype(BF16), lfn)
    cn_t = cn.T

    q = q_ref[...]
    kn = kn_ref[...].astype(BF16)
    vn = vn_ref[...].astype(BF16)
    r, c = _tri_mask(tn)
    outs = []
    for hh in range(FOX_HEADS):
        hs = slice(hh * FOX_HEAD_DIM, (hh + 1) * FOX_HEAD_DIM)
        qh = q[:, hs]
        cq = cn[0:tn, hh:hh + 1]
        s_past = _dot(qh, kc_ref[hh].astype(BF16)) + ((carry[hh:hh + 1, :] + cq) - cp[hh:hh + 1, :])
        s_new = _dot_nt(qh, kn[:, hs]) + (cq - cn_t[hh:hh + 1, 0:tn])
        s_new = jnp.where(r >= c, s_new, -jnp.inf)
        m = jnp.maximum(jnp.max(s_past, axis=-1, keepdims=True), jnp.max(s_new, axis=-1, keepdims=True))
        p_past = jnp.exp(s_past - m)
        p_new = jnp.exp(s_new - m)
        denom = jnp.sum(p_past, axis=-1, keepdims=True) + jnp.sum(p_new, axis=-1, keepdims=True)
        o = _dot_nt(p_past.astype(BF16), vc_ref[hh].astype(BF16)) + _dot(p_new.astype(BF16), vn[:, hs])
        outs.append(o / denom)
    o_ref[...] = jnp.concatenate(outs, axis=1).astype(o_ref.dtype)


def _fox_sample_call(q, kn_all, vn_all, lfn, cache_k, cache_v, lfp_t, layer):
    bsz, tn, _ = q.shape
    past = cache_k.shape[-1]
    seg = _row_tile(past, 512)
    bt = lambda w_: pl.BlockSpec((None, tn, w_), lambda b: (b, 0, 0))
    new = pl.BlockSpec((None, None, tn, FOX_WIDTH), lambda b: (layer, b, 0, 0))
    cache = pl.BlockSpec((None, None, FOX_HEADS, FOX_HEAD_DIM, past), lambda b: (layer, b, 0, 0, 0))
    return pl.pallas_call(
        functools.partial(_fox_sample_body, tn=tn, past=past, seg=seg), grid=(bsz,),
        in_specs=[bt(FOX_WIDTH), new, new, bt(LANES), cache, cache,
                  pl.BlockSpec((None, None, FOX_HEADS, past), lambda b: (layer, b, 0, 0))],
        out_specs=bt(FOX_WIDTH),
        out_shape=jax.ShapeDtypeStruct((bsz, tn, FOX_WIDTH), BF16),
        compiler_params=_params(("parallel",), 48),
        name="fox_attn_sample",
    )(q, kn_all, vn_all, lfn, cache_k, cache_v, lfp_t)


def _pad_cols(a, width):
    return jnp.pad(a, ((0, 0), (0, width - a.shape[1])))


def _layer_params(l, norm_ffn1, w1_gate, w1_up, w1_down, norm_mix, w_in, ssd_conv_w, ssd_conv_b,
                  ssd_dt_bias, ssd_a_log, ssd_d, ssd_norm, gla_w_gate, gla_b_gate, gla_norm,
                  fox_q_norm, fox_k_norm, fox_f_bias, w_out, norm_ffn2, w2_gate, w2_up, w2_down):
    wi = w_in[l]
    row = lambda a: a.reshape(1, -1).astype(F32)
    hk = GLA_HEADS * GLA_DK
    p = dict(
        ffn1=(row(norm_ffn1[l]), w1_gate[l].astype(BF16), w1_up[l].astype(BF16), w1_down[l].astype(BF16)),
        ffn2=(row(norm_ffn2[l]), w2_gate[l].astype(BF16), w2_up[l].astype(BF16), w2_down[l].astype(BF16)),
        norm_mix=row(norm_mix[l]),
        w_out=w_out[l].astype(BF16),
        w_ssd=jnp.concatenate([wi[:, _OFF_Z:_OFF_DT], _pad_cols(wi[:, _OFF_DT:_OFF_GQ], LANES)], axis=1).astype(BF16),
        conv_w=ssd_conv_w[l].astype(F32),
        conv_b=row(ssd_conv_b[l]),
        dt_bias=_pad_cols(row(ssd_dt_bias[l]), LANES),
        a_log=_pad_cols(row(ssd_a_log[l]), LANES),
        d_exp=row(jnp.repeat(ssd_d[l], SSD_HEAD_DIM)),
        ssd_norm=row(ssd_norm[l]),
        w_gla=jnp.concatenate([wi[:, _OFF_GQ:_OFF_GR], _pad_cols(wi[:, _OFF_GR:_OFF_FQ], LANES)], axis=1).astype(BF16),
        w_gate=jnp.pad(gla_w_gate[l], ((0, LANES - GLA_RANK), (0, 0))).astype(BF16),
        b_gate=row(gla_b_gate[l]),
        gla_norm=row(jnp.tile(gla_norm[l], GLA_HEADS)),
        w_fox=jnp.concatenate([wi[:, _OFF_FQ:_OFF_FF], _pad_cols(wi[:, _OFF_FF:], LANES)], axis=1).astype(BF16),
        q_norm=row(jnp.tile(fox_q_norm[l], FOX_HEADS)),
        k_norm=row(jnp.tile(fox_k_norm[l], FOX_HEADS)),
        f_bias=_pad_cols(row(fox_f_bias[l]), LANES),
    )
    return p


def _gla_state_in(s):
    bsz = s.shape[0]
    eye = jnp.eye(GLA_HEADS, dtype=s.dtype)
    full = jnp.einsum('bhkv,hg->bhvgk', s, eye)
    return full.reshape(bsz, GLA_HEADS * GLA_DV, GLA_HEADS * GLA_DK)


def _gla_state_out(st):
    bsz = st.shape[0]
    full = st.reshape(bsz, GLA_HEADS, GLA_DV, GLA_HEADS, GLA_DK)
    idx = jnp.arange(GLA_HEADS)
    diag = full[:, idx, :, idx, :]
    return jnp.transpose(diag, (1, 0, 3, 2))


def _mix(x, p, conv_prev, ssd_h0, gla_s0, fox_cache, layer, depth, kv_bufs, tiles):
    bsz, t, _ = x.shape
    cprev = jnp.pad(conv_prev, ((0, 0), (SUBLANES - (SSD_CONV - 1), 0), (0, 0)))
    y_ssd, cnew, hnew = _ssd_call(x, p['norm_mix'], p['w_ssd'], p['conv_w'], p['conv_b'], p['dt_bias'],
                                  p['a_log'], p['d_exp'], p['ssd_norm'], cprev,
                                  ssd_h0.reshape(bsz, SSD_WIDTH, SSD_STATE), tiles['ssd'], tiles['ssd_rows'])
    o_gla, snew = _gla_call(x, p['norm_mix'], p['w_gla'], p['w_gate'], p['b_gate'], p['gla_norm'],
                            _gla_state_in(gla_s0), tiles['gla'], tiles['gla_rows'])
    proj = functools.partial(_fox_proj_call, x, p['norm_mix'], p['w_fox'], p['q_norm'], p['k_norm'], p['f_bias'],
                             tiles['fox_proj'], tiles['fox_rows'], layer=layer, depth=depth, kv_bufs=kv_bufs)
    if fox_cache is None:
        k_all, v_all, lf_t, qt, kp, vt = proj(attn_layout=True)
        o_fox = _fox_prompt_call(qt, kp, vt, tiles['fox_q'])
        lf_state = jnp.swapaxes(lf_t[:, 0:FOX_HEADS, :], 1, 2)
    else:
        k_all, v_all, lf, qs = proj(attn_layout=False)
        cache_k, cache_v, lfp_t = fox_cache
        o_fox = _fox_sample_call(qs, k_all, v_all, lf, cache_k, cache_v, lfp_t, layer)
        lf_state = lf[:, :, 0:FOX_HEADS]
    state = (cnew[:, SUBLANES - (SSD_CONV - 1):, :],
             hnew.reshape(bsz, SSD_HEADS, SSD_HEAD_DIM, SSD_STATE),
             _gla_state_out(snew),
             lf_state)
    return (y_ssd, o_gla, o_fox), state, (k_all, v_all)


def _trunk_layer(x, p, conv_prev, ssd_h0, gla_s0, fox_cache, layer, depth, kv_bufs, tiles):
    bsz, t, d = x.shape
    x1 = _ffn_call(x.reshape(bsz * t, d), *p['ffn1'], tiles['ffn']).reshape(bsz, t, d)
    (y_ssd, o_gla, o_fox), state, kv_bufs = _mix(x1, p, conv_prev, ssd_h0, gla_s0, fox_cache, layer, depth,
                                                 kv_bufs, tiles)
    flat = lambda a: a.reshape(bsz * t, a.shape[-1])
    x3 = _out_ffn_call(flat(x1), flat(y_ssd), flat(o_gla), flat(o_fox), p['w_out'], *p['ffn2'], tiles['ffn'])
    return x3.reshape(bsz, t, d), state, kv_bufs


PROMPT_TILES = dict(ffn=512, ssd=256, ssd_rows=2, gla=256, gla_rows=4, fox_proj=512, fox_rows=2,
                    fox_q=1024)
SAMPLE_TILES = dict(ffn=512, ssd=64, ssd_rows=4, gla=64, gla_rows=4, fox_proj=64, fox_rows=8, fox_q=64)


def kernel(x_prompt, x_sample, state_ssd_conv, state_ssd, state_gla, cache_fox_k, cache_fox_v, cache_fox_logf, norm_ffn1, w1_gate, w1_up, w1_down, norm_mix, w_in, ssd_conv_w, ssd_conv_b, ssd_dt_bias, ssd_a_log, ssd_d, ssd_norm, gla_w_gate, gla_b_gate, gla_norm, fox_q_norm, fox_k_norm, fox_f_bias, w_out, norm_ffn2, w2_gate, w2_up, w2_down):
    depth = w_in.shape[0]
    bp = x_prompt.shape[0]
    weights = (norm_ffn1, w1_gate, w1_up, w1_down, norm_mix, w_in, ssd_conv_w, ssd_conv_b, ssd_dt_bias,
               ssd_a_log, ssd_d, ssd_norm, gla_w_gate, gla_b_gate, gla_norm, fox_q_norm, fox_k_norm,
               fox_f_bias, w_out, norm_ffn2, w2_gate, w2_up, w2_down)
    cache_k = jnp.transpose(cache_fox_k, (0, 1, 3, 4, 2))
    cache_v = jnp.transpose(cache_fox_v, (0, 1, 3, 4, 2))
    lfp_t = jnp.swapaxes(cache_fox_logf, 2, 3)
    zeros_conv = jnp.zeros((bp, SSD_CONV - 1, SSD_CONV_CH), F32)
    zeros_ssd = jnp.zeros((bp, SSD_HEADS, SSD_HEAD_DIM, SSD_STATE), F32)
    zeros_gla = jnp.zeros((bp, GLA_HEADS, GLA_DK, GLA_DV), F32)

    xp, xs = x_prompt, x_sample
    p_new = [[] for _ in range(4)]
    s_new = [[] for _ in range(4)]
    kv_p = kv_s = None
    for l in range(depth):
        p = _layer_params(l, *weights)
        xp, st_p, kv_p = _trunk_layer(xp, p, zeros_conv, zeros_ssd, zeros_gla, None, l, depth, kv_p, PROMPT_TILES)
        xs, st_s, kv_s = _trunk_layer(xs, p, state_ssd_conv[l], state_ssd[l], state_gla[l],
                                      (cache_k, cache_v, lfp_t), l, depth, kv_s, SAMPLE_TILES)
        for i in range(4):
            p_new[i].append(st_p[i])
            s_new[i].append(st_s[i])

    def leaves(small, kv):
        conv, ssd, gla, logf = [jnp.stack(a) for a in small]
        heads = lambda a: a.reshape(a.shape[:3] + (FOX_HEADS, FOX_HEAD_DIM))
        return conv, ssd, gla, heads(kv[0]), heads(kv[1]), logf

    return (xp, xs, *leaves(p_new, kv_p), *leaves(s_new, kv_s))
```

```python
import functools

import numpy as np
import jax
import jax.numpy as jnp
from jax import lax
from jax.experimental import pallas as pl
from jax.experimental.pallas import tpu as pltpu

F32 = jnp.float32
BF16 = jnp.bfloat16

EPS = 1e-6
D_MODEL = 1024
D_FF = 2816
SSD_HEADS = 8
SSD_HEAD_DIM = 64
SSD_WIDTH = 512
SSD_GROUPS = 2
SSD_STATE = 128
SSD_CONV = 4
SSD_CONV_CH = 1024
GLA_HEADS = 4
GLA_DK = 32
GLA_DV = 64
GLA_WIDTH = 256
GLA_RANK = 16
GLA_TAU = 16.0
GLA_CHUNK = 64
FOX_HEADS = 4
FOX_HEAD_DIM = 64
FOX_WIDTH = 256
LANES = 128
SUBLANES = 8
MIB = 1024 * 1024

_OFF_Z, _OFF_XBC, _OFF_DT = 0, 512, 1536
_OFF_GQ, _OFF_GR = 1544, 2312
_OFF_FQ, _OFF_FF = 2328, 3096
SSD_COLS = SSD_WIDTH + SSD_CONV_CH + LANES
GLA_COLS = 2 * GLA_HEADS * GLA_DK + 2 * GLA_WIDTH + LANES
FOX_COLS = 3 * FOX_WIDTH + LANES
AUX_PARTS = 3
FOX_QB = 256
FOX_KB = 256
FOX_SWEEP = 1024
FOX_VT_ROWS = FOX_HEAD_DIM + 16


def _dot(a, b):
    return jnp.dot(a, b, preferred_element_type=F32)


def _dot_nt(a, b):
    return lax.dot_general(a, b, (((1,), (1,)), ((), ())), preferred_element_type=F32)


def _dot_tn(a, b):
    return lax.dot_general(a, b, (((0,), (0,)), ((), ())), preferred_element_type=F32)


def _split_bf16(x, parts):
    out = []
    r = x
    for i in range(parts):
        p = r.astype(BF16)
        out.append(p)
        if i + 1 < parts:
            r = r - p.astype(F32)
    return out


def _dot_sel_lhs(sel, x, parts=3):
    n = x.shape[1]
    t = _dot(sel, jnp.concatenate(_split_bf16(x, parts), axis=1))
    acc = t[:, 0:n]
    for i in range(1, parts):
        acc = acc + t[:, i * n:(i + 1) * n]
    return acc


def _dot_sel_rhs(x, sel, parts=3):
    m = x.shape[0]
    pieces = _split_bf16(x, parts)
    if m % (2 * SUBLANES):
        acc = _dot(pieces[0], sel)
        for p in pieces[1:]:
            acc = acc + _dot(p, sel)
        return acc
    t = _dot(jnp.concatenate(pieces, axis=0), sel)
    acc = t[0:m]
    for i in range(1, parts):
        acc = acc + t[i * m:(i + 1) * m]
    return acc


def _rms(x, w):
    ms = jnp.mean(x * x, axis=-1, keepdims=True)
    return x * lax.rsqrt(ms + EPS) * w


def _silu(x):
    return x * jax.nn.sigmoid(x)


def _softplus(x):
    return jnp.maximum(x, 0.0) + jnp.log1p(jnp.exp(-jnp.abs(x)))


def _log_sigmoid(x):
    return -_softplus(-x)


def _tri_mask(n, m=None):
    m = n if m is None else m
    r = lax.broadcasted_iota(jnp.int32, (n, m), 0)
    c = lax.broadcasted_iota(jnp.int32, (n, m), 1)
    return r, c


def _lane_pair_select(a_even, a_odd):
    lane = lax.broadcasted_iota(jnp.int32, a_even.shape, 1)
    return jnp.where(lane < 64, a_even, a_odd)


def _expand_heads(v, heads, rows):
    pieces = []
    for i in range(0, len(heads), 2):
        a = jnp.broadcast_to(v[:, heads[i]:heads[i] + 1], (rows, LANES))
        b = jnp.broadcast_to(v[:, heads[i + 1]:heads[i + 1] + 1], (rows, LANES))
        pieces.append(_lane_pair_select(a, b))
    return pieces[0] if len(pieces) == 1 else jnp.concatenate(pieces, axis=1)


def _group_mean_matrix(width, group):
    r, c = _tri_mask(width)
    return jnp.where((r // group) == (c // group), 1.0 / group, 0.0).astype(BF16)


def _project_rows(h, w_ref, tt, rows):
    if rows == 1 or tt < 2 * LANES:
        return _dot(h, w_ref[...])
    return jnp.concatenate([_dot(h[i * tt:(i + 1) * tt], w_ref[...]) for i in range(rows)], axis=0)


def _params(sem, vmem_mib):
    return pltpu.CompilerParams(dimension_semantics=sem, vmem_limit_bytes=vmem_mib * MIB)


def _const_spec(shape, single=False):
    nd = len(shape)
    if single:
        return pl.BlockSpec(shape, lambda *_: (0,) * nd, pipeline_mode=pl.Buffered(1))
    return pl.BlockSpec(shape, lambda *_: (0,) * nd)


def _swiglu_half(x, g_ref, wg_ref, wu_ref, wd_ref):
    h = _rms(x, g_ref[...]).astype(BF16)
    gate = _dot(h, wg_ref[...])
    up = _dot(h, wu_ref[...])
    a = (_silu(gate) * up).astype(BF16)
    return x + 0.5 * _dot(a, wd_ref[...])


def _ffn_body(x_ref, g_ref, wg_ref, wu_ref, wd_ref, o_ref):
    o_ref[...] = _swiglu_half(x_ref[...], g_ref, wg_ref, wu_ref, wd_ref)


def _out_ffn_body(x_ref, ys_ref, og_ref, of_ref, wo_ref, g_ref, wg_ref, wu_ref, wd_ref, o_ref):
    x = x_ref[...]
    x = x + (_dot(ys_ref[...], wo_ref[0:SSD_WIDTH, :])
             + _dot(og_ref[...], wo_ref[SSD_WIDTH:SSD_WIDTH + GLA_WIDTH, :])
             + _dot(of_ref[...], wo_ref[SSD_WIDTH + GLA_WIDTH:, :]))
    o_ref[...] = _swiglu_half(x, g_ref, wg_ref, wu_ref, wd_ref)


def _ffn_weight_specs():
    return [_const_spec((1, D_MODEL)),
            _const_spec((D_MODEL, D_FF), single=True),
            _const_spec((D_MODEL, D_FF), single=True),
            _const_spec((D_FF, D_MODEL), single=True)]


def _row_tile(n, want):
    t = min(want, n)
    while n % t:
        t //= 2
    return t


def _ffn_call(x, g, wg, wu, wd, tm):
    n = x.shape[0]
    tm = _row_tile(n, tm)
    row = pl.BlockSpec((tm, D_MODEL), lambda i: (i, 0))
    return pl.pallas_call(
        _ffn_body, grid=(n // tm,),
        in_specs=[row] + _ffn_weight_specs(),
        out_specs=row,
        out_shape=jax.ShapeDtypeStruct((n, D_MODEL), F32),
        compiler_params=_params(("parallel",), 56),
        name="ffn",
    )(x, g, wg, wu, wd)


def _out_ffn_call(x, ys, og, of, wo, g, wg, wu, wd, tm):
    n = x.shape[0]
    tm = _row_tile(n, tm)
    row = lambda w: pl.BlockSpec((tm, w), lambda i: (i, 0))
    return pl.pallas_call(
        _out_ffn_body, grid=(n // tm,),
        in_specs=[row(D_MODEL), row(SSD_WIDTH), row(GLA_WIDTH), row(FOX_WIDTH),
                  _const_spec((D_MODEL, D_MODEL), single=True)] + _ffn_weight_specs(),
        out_specs=row(D_MODEL),
        out_shape=jax.ShapeDtypeStruct((n, D_MODEL), F32),
        compiler_params=_params(("parallel",), 56),
        name="out_ffn",
    )(x, ys, og, of, wo, g, wg, wu, wd)


def _ssd_body(x_ref, nm_ref, w_ref, cw_ref, cb_ref, dtb_ref, alog_ref, dexp_ref, nw_ref,
              cprev_ref, h0_ref, y_ref, cnew_ref, hnew_ref, xbuf, hst, *, tt, rows):
    t = pl.program_id(1)

    @pl.when(t == 0)
    def _():
        xbuf[...] = cprev_ref[...]
        hst[...] = h0_ref[...]

    x = x_ref[...].reshape(rows * tt, D_MODEL)
    h = _rms(x, nm_ref[...]).astype(BF16)
    u = _project_rows(h, w_ref, tt, rows)
    z = u[:, 0:SSD_WIDTH]
    xbc = u[:, SSD_WIDTH:SSD_WIDTH + SSD_CONV_CH]
    dt_raw = u[:, SSD_WIDTH + SSD_CONV_CH:]

    convs = []
    row8 = lax.broadcasted_iota(jnp.int32, (SUBLANES, SSD_CONV_CH), 0)
    for i in range(rows):
        xi = xbc[i * tt:(i + 1) * tt]
        prev = xbuf[i]
        conv = None
        for j in range(SSD_CONV - 1, 0, -1):
            rolled = pltpu.roll(xi, j, 0)
            head = jnp.where(row8 < j, pltpu.roll(prev, j, 0), rolled[0:SUBLANES])
            term = jnp.concatenate([head, rolled[SUBLANES:]], axis=0) * cw_ref[SSD_CONV - 1 - j:SSD_CONV - j, :]
            conv = term if conv is None else conv + term
        convs.append(conv + xi * cw_ref[SSD_CONV - 1:SSD_CONV, :])
        xbuf[i] = xi[tt - SUBLANES:, :]
    xa = _silu(jnp.concatenate(convs, axis=0) + cb_ref[...])

    xs = xa[:, 0:SSD_WIDTH]
    bm = xa[:, SSD_WIDTH:SSD_WIDTH + SSD_GROUPS * SSD_STATE].astype(BF16)
    cm = xa[:, SSD_WIDTH + SSD_GROUPS * SSD_STATE:].astype(BF16)
    dt = _softplus(dt_raw + dtb_ref[...])
    a = dt * (-jnp.exp(alog_ref[...]))

    r, c = _tri_mask(tt)
    causal = r >= c
    tri = causal.astype(BF16)
    acum = jnp.concatenate(
        [_dot_sel_lhs(tri, a[i * tt:(i + 1) * tt]) for i in range(rows)], axis=0)
    shift_t = (acum - jnp.log(dt)).T
    alast = jnp.concatenate(
        [jnp.broadcast_to(acum[(i + 1) * tt - 1:(i + 1) * tt, :], (tt, LANES)) for i in range(rows)], axis=0)
    ea = jnp.exp(acum)
    wend = jnp.exp(alast - acum) * dt

    heads_per_group = SSD_HEADS // SSD_GROUPS
    gw = heads_per_group * SSD_HEAD_DIM
    heads_of = [list(range(g * heads_per_group, (g + 1) * heads_per_group)) for g in range(SSD_GROUPS)]
    nrow = rows * tt
    xb = xs.astype(BF16)
    xw = jnp.concatenate([xs[:, g * gw:(g + 1) * gw] * _expand_heads(wend, heads_of[g], nrow)
                          for g in range(SSD_GROUPS)], axis=1).astype(BF16)
    ea_x = jnp.concatenate([_expand_heads(ea, heads_of[g], nrow) for g in range(SSD_GROUPS)], axis=1)

    units = [(i, g) for g in range(SSD_GROUPS) for i in range(rows)]
    rs = {i: slice(i * tt, (i + 1) * tt) for i in range(rows)}
    gs = {g: slice(g * SSD_STATE, (g + 1) * SSD_STATE) for g in range(SSD_GROUPS)}
    cbs = {(i, g): _dot_nt(cm[rs[i], gs[g]], bm[rs[i], gs[g]]) for i, g in units}
    upds = {(i, g): _dot_tn(xw[rs[i], g * gw:(g + 1) * gw], bm[rs[i], gs[g]]) for i, g in units}
    y_unit = {}
    for i, g in units:
        heads = heads_of[g]
        intra = []
        for pi in range(heads_per_group // 2):
            ys = []
            for e in range(2):
                hh = heads[2 * pi + e]
                seg = acum[rs[i], hh:hh + 1] - shift_t[hh:hh + 1, rs[i]]
                m = (cbs[(i, g)] * jnp.where(causal, jnp.exp(seg), 0.0)).astype(BF16)
                col = g * gw + pi * LANES
                ys.append(_dot(m, xb[rs[i], col:col + LANES]))
            intra.append(_lane_pair_select(ys[0], ys[1]))
        hg = hst[i, g * gw:(g + 1) * gw, :]
        y_inter = _dot_nt(cm[rs[i], gs[g]], hg.astype(BF16)) * ea_x[rs[i], g * gw:(g + 1) * gw]
        y_unit[(i, g)] = jnp.concatenate(intra, axis=1) + y_inter
        cd = jnp.concatenate(
            [jnp.broadcast_to(jnp.exp(acum[(i + 1) * tt - 1:(i + 1) * tt, hh:hh + 1]), (SSD_HEAD_DIM, SSD_STATE))
             for hh in heads], axis=0)
        hst[i, g * gw:(g + 1) * gw, :] = hg * cd + upds[(i, g)]

    y = jnp.concatenate(
        [jnp.concatenate([y_unit[(i, g)] for g in range(SSD_GROUPS)], axis=1) for i in range(rows)], axis=0)
    y = (y + dexp_ref[...] * xs) * _silu(z)
    nw = nw_ref[...]
    outs = [_rms(y[:, g * gw:(g + 1) * gw], nw[:, g * gw:(g + 1) * gw]) for g in range(SSD_GROUPS)]
    y_ref[...] = jnp.concatenate(outs, axis=1).astype(y_ref.dtype).reshape(rows, tt, SSD_WIDTH)

    @pl.when(t == pl.num_programs(1) - 1)
    def _():
        cnew_ref[...] = xbuf[...]
        hnew_ref[...] = hst[...]


def _ssd_call(x, nm, w, cw, cb, dtb, alog, dexp, nw, cprev, h0, tt, rows):
    bsz, t, _ = x.shape
    tt = _row_tile(t, tt)
    rows = _row_tile(bsz, rows)
    bt = lambda w_: pl.BlockSpec((rows, tt, w_), lambda b, i: (b, i, 0))
    per_b = lambda r_, w_: pl.BlockSpec((rows, r_, w_), lambda b, i: (b, 0, 0))
    return pl.pallas_call(
        functools.partial(_ssd_body, tt=tt, rows=rows), grid=(bsz // rows, t // tt),
        in_specs=[bt(D_MODEL), _const_spec((1, D_MODEL)), _const_spec((D_MODEL, SSD_COLS), single=True),
                  _const_spec((SSD_CONV, SSD_CONV_CH)), _const_spec((1, SSD_CONV_CH)),
                  _const_spec((1, LANES)), _const_spec((1, LANES)),
                  _const_spec((1, SSD_WIDTH)), _const_spec((1, SSD_WIDTH)),
                  per_b(SUBLANES, SSD_CONV_CH), per_b(SSD_WIDTH, SSD_STATE)],
        out_specs=[bt(SSD_WIDTH), per_b(SUBLANES, SSD_CONV_CH), per_b(SSD_WIDTH, SSD_STATE)],
        out_shape=[jax.ShapeDtypeStruct((bsz, t, SSD_WIDTH), BF16),
                   jax.ShapeDtypeStruct((bsz, SUBLANES, SSD_CONV_CH), F32),
                   jax.ShapeDtypeStruct((bsz, SSD_WIDTH, SSD_STATE), F32)],
        scratch_shapes=[pltpu.VMEM((rows, SUBLANES, SSD_CONV_CH), F32),
                        pltpu.VMEM((rows, SSD_WIDTH, SSD_STATE), F32)],
        compiler_params=_params(("parallel", "arbitrary"), 48),
        name="ssd",
    )(x, nm, w, cw, cb, dtb, alog, dexp, nw, cprev, h0)


def _gla_body(x_ref, nm_ref, w_ref, wgate_ref, bgate_ref, gn_ref, s0_ref, o_ref, snew_ref, st, *, tt, rows):
    t = pl.program_id(1)

    @pl.when(t == 0)
    def _():
        st[...] = s0_ref[...]

    hk = GLA_HEADS * GLA_DK
    nchunk = tt // GLA_CHUNK
    x = x_ref[...].reshape(rows * tt, D_MODEL)
    h = _rms(x, nm_ref[...]).astype(BF16)
    u = _project_rows(h, w_ref, tt, rows)
    q = u[:, 0:hk] * (GLA_DK ** -0.5)
    k = u[:, hk:2 * hk]
    v = u[:, 2 * hk:2 * hk + GLA_WIDTH].astype(BF16)
    gg = u[:, 2 * hk + GLA_WIDTH:2 * hk + 2 * GLA_WIDTH]
    gr = u[:, 2 * hk + 2 * GLA_WIDTH:].astype(BF16)
    la = _log_sigmoid(_dot(gr, wgate_ref[...]) + bgate_ref[...]) / GLA_TAU

    r, c = _tri_mask(tt)
    sel = (((r // GLA_CHUNK) == (c // GLA_CHUNK)) & (r >= c)).astype(BF16)
    bcum = jnp.concatenate(
        [_dot_sel_lhs(sel, la[i * tt:(i + 1) * tt]) for i in range(rows)], axis=0)
    blast = jnp.concatenate(
        [jnp.broadcast_to(bcum[(ci + 1) * GLA_CHUNK - 1:(ci + 1) * GLA_CHUNK, :], (GLA_CHUNK, hk))
         for ci in range(rows * nchunk)], axis=0)
    qd = (q * jnp.exp(bcum)).astype(BF16)
    kd = (k * jnp.exp(-bcum)).astype(BF16)
    kend = (k * jnp.exp(blast - bcum)).astype(BF16)
    eblast = jnp.exp(blast)

    lane_k = lax.broadcasted_iota(jnp.int32, (GLA_CHUNK, hk), 1) // GLA_DK
    lane_v = lax.broadcasted_iota(jnp.int32, (GLA_CHUNK, GLA_WIDTH), 1) // GLA_DV
    ar, ac = _tri_mask(GLA_HEADS * GLA_CHUNK, GLA_CHUNK)
    att_causal = (ar % GLA_CHUNK) >= ac
    sr, sc = _tri_mask(GLA_WIDTH, hk)
    diag = (sr // GLA_DV) == (sc // GLA_DK)

    units = [(i, ci) for ci in range(nchunk) for i in range(rows)]
    sl = {(i, ci): slice(i * tt + ci * GLA_CHUNK, i * tt + (ci + 1) * GLA_CHUNK) for i, ci in units}
    att, upd, o_intra = {}, {}, {}
    for un in units:
        qd_c = qd[sl[un]]
        lhs = jnp.concatenate([jnp.where(lane_k == hh, qd_c, jnp.zeros_like(qd_c))
                               for hh in range(GLA_HEADS)], axis=0)
        att[un] = jnp.where(att_causal, _dot_nt(lhs, kd[sl[un]]), 0.0).astype(BF16)
        upd[un] = jnp.where(diag, _dot_tn(v[sl[un]], kend[sl[un]]), 0.0)
    for un in units:
        res = _dot(att[un], v[sl[un]])
        acc = jnp.zeros((GLA_CHUNK, GLA_WIDTH), F32)
        for hh in range(GLA_HEADS):
            acc = jnp.where(lane_v == hh, res[hh * GLA_CHUNK:(hh + 1) * GLA_CHUNK], acc)
        o_intra[un] = acc
    outs = {}
    for un in units:
        i = un[0]
        s_prev = st[i]
        outs[un] = o_intra[un] + _dot_nt(qd[sl[un]], s_prev.astype(BF16))
        st[i] = s_prev * eblast[sl[un].start:sl[un].start + 1, :] + upd[un]

    o = jnp.concatenate([outs[(i, ci)] for i in range(rows) for ci in range(nchunk)], axis=0)
    ms = _dot_sel_rhs(o * o, _group_mean_matrix(GLA_WIDTH, GLA_DV), parts=2)
    o = o * lax.rsqrt(ms + EPS) * gn_ref[...]
    o_ref[...] = (o * _silu(gg)).astype(o_ref.dtype).reshape(rows, tt, GLA_WIDTH)

    @pl.when(t == pl.num_programs(1) - 1)
    def _():
        snew_ref[...] = st[...]


def _gla_call(x, nm, w, wgate, bgate, gn, s0, tt, rows):
    bsz, t, _ = x.shape
    tt = _row_tile(t, tt)
    rows = _row_tile(bsz, rows)
    hk = GLA_HEADS * GLA_DK
    bt = lambda w_: pl.BlockSpec((rows, tt, w_), lambda b, i: (b, i, 0))
    per_b = pl.BlockSpec((rows, GLA_WIDTH, hk), lambda b, i: (b, 0, 0))
    return pl.pallas_call(
        functools.partial(_gla_body, tt=tt, rows=rows), grid=(bsz // rows, t // tt),
        in_specs=[bt(D_MODEL), _const_spec((1, D_MODEL)), _const_spec((D_MODEL, GLA_COLS), single=True),
                  _const_spec((LANES, hk)), _const_spec((1, hk)), _const_spec((1, GLA_WIDTH)), per_b],
        out_specs=[bt(GLA_WIDTH), per_b],
        out_shape=[jax.ShapeDtypeStruct((bsz, t, GLA_WIDTH), BF16),
                   jax.ShapeDtypeStruct((bsz, GLA_WIDTH, hk), F32)],
        scratch_shapes=[pltpu.VMEM((rows, GLA_WIDTH, hk), F32)],
        compiler_params=_params(("parallel", "arbitrary"), 32),
        name="gla",
    )(x, nm, w, wgate, bgate, gn, s0)


def _fox_proj_body(x_ref, nm_ref, w_ref, qn_ref, kn_ref, fb_ref, pa_ref, arow_ref, *rest,
                   tt, rows, attn_layout, n_alias):
    k_ref, v_ref, lf_ref, *rest = rest[n_alias:]
    if attn_layout:
        qt_ref, kp_ref, vt_ref, carry = rest
    else:
        qs_ref, carry = rest
    t = pl.program_id(1)

    @pl.when(t == 0)
    def _():
        carry[...] = jnp.zeros_like(carry)

    nrow = rows * tt
    h = _rms(x_ref[...].reshape(nrow, D_MODEL), nm_ref[...]).astype(BF16)
    u = _project_rows(h, w_ref, tt, rows)
    fq = u[:, 0:FOX_WIDTH]
    fk = u[:, FOX_WIDTH:2 * FOX_WIDTH]
    fv = u[:, 2 * FOX_WIDTH:3 * FOX_WIDTH]
    ff = u[:, 3 * FOX_WIDTH:]

    gmat = _group_mean_matrix(FOX_WIDTH, FOX_HEAD_DIM)
    qn = fq * lax.rsqrt(_dot_sel_rhs(fq * fq, gmat, parts=2) + EPS) * qn_ref[...]
    kn = fk * lax.rsqrt(_dot_sel_rhs(fk * fk, gmat, parts=2) + EPS) * kn_ref[...]
    lf = _log_sigmoid(ff + fb_ref[...])
    k_ref[...] = kn.reshape(rows, tt, FOX_WIDTH)
    v_ref[...] = fv.reshape(rows, tt, FOX_WIDTH)
    qs = (qn * (FOX_HEAD_DIM ** -0.5)).astype(BF16)

    if not attn_layout:
        lf_ref[...] = lf.reshape(rows, tt, LANES)
        qs_ref[...] = qs.reshape(rows, tt, FOX_WIDTH)
        return

    r, c = _tri_mask(tt)
    tri = (r >= c).astype(BF16)
    cums = []
    for i in range(rows):
        lf_i = lf[i * tt:(i + 1) * tt]
        lf_ref[i] = lf_i.T[0:SUBLANES]
        cum_i = carry[i] + _dot_sel_lhs(tri, lf_i)
        carry[i] = cum_i[tt - 1:tt, :]
        cums.append(cum_i)
    cum = jnp.concatenate(cums, axis=0) if rows > 1 else cums[0]
    c3 = jnp.concatenate(_split_bf16(cum, AUX_PARTS), axis=1)
    aux = _dot(c3, pa_ref[...])
    aux_k = (aux[:, 0:LANES] + arow_ref[0:1, :]).astype(BF16)
    aux_q_all = aux[:, LANES:]
    knb = kn.astype(BF16)
    fv_t = fv.T
    ones = jnp.ones((FOX_VT_ROWS - FOX_HEAD_DIM, nrow), F32)
    lane = lax.broadcasted_iota(jnp.int32, (nrow, LANES), 1)
    for p in range(FOX_HEADS // 2):
        sl = slice(p * LANES, (p + 1) * LANES)
        for i in range(rows):
            kp_ref[i, p, :, 0:LANES] = knb[i * tt:(i + 1) * tt, sl]
            kp_ref[i, p, :, LANES:] = aux_k[i * tt:(i + 1) * tt]
        for e in range(2):
            hh = 2 * p + e
            vt = jnp.concatenate([fv_t[hh * FOX_HEAD_DIM:(hh + 1) * FOX_HEAD_DIM, :], ones], axis=0).astype(BF16)
            qmask = jnp.where((lane // FOX_HEAD_DIM) == e, qs[:, sl], jnp.zeros_like(qs[:, sl])).astype(F32)
            aux_q = jnp.where((lane // AUX_PARTS) == hh, aux_q_all, 0.0) + arow_ref[1 + hh:2 + hh, :]
            q_t = jnp.concatenate([qmask.T, aux_q.T], axis=0).astype(BF16)
            for i in range(rows):
                for j in range(tt // FOX_KB):
                    vt_ref[i, hh, j] = vt[:, i * tt + j * FOX_KB:i * tt + (j + 1) * FOX_KB]
                for j in range(tt // FOX_QB):
                    qt_ref[i, hh, j] = q_t[:, i * tt + j * FOX_QB:i * tt + (j + 1) * FOX_QB]


def _fox_aux_constants():
    nslot = FOX_HEADS * AUX_PARTS
    pa = np.zeros((AUX_PARTS * LANES, 2 * LANES), np.float32)
    arow = np.zeros((SUBLANES, LANES), np.float32)
    for hh in range(FOX_HEADS):
        for j in range(AUX_PARTS):
            pa[j * LANES + hh, LANES + AUX_PARTS * hh + j] = 1.0
            pa[j * LANES + hh, nslot + AUX_PARTS * hh + j] = -1.0
            arow[0, AUX_PARTS * hh + j] = 1.0
            arow[1 + hh, nslot + AUX_PARTS * hh + j] = 1.0
    return jnp.asarray(pa, BF16), jnp.asarray(arow, F32)


def _fox_proj_call(x, nm, w, qn, kn, fb, tt, rows, attn_layout, layer, depth, kv_bufs):
    bsz, t, _ = x.shape
    tt = _row_tile(t, tt)
    rows = _row_tile(bsz, rows)
    pa, arow = _fox_aux_constants()
    bt = lambda w_: pl.BlockSpec((rows, tt, w_), lambda b, i: (b, i, 0))
    bht = lambda n_, w_: pl.BlockSpec((rows, n_, tt, w_), lambda b, i: (b, 0, i, 0))
    kv_spec = pl.BlockSpec((None, rows, tt, FOX_WIDTH), lambda b, i: (layer, b, i, 0))
    out_specs = [kv_spec, kv_spec]
    out_shape = [jax.ShapeDtypeStruct((depth, bsz, t, FOX_WIDTH), F32),
                 jax.ShapeDtypeStruct((depth, bsz, t, FOX_WIDTH), F32)]
    n_fixed = 8
    if kv_bufs is None:
        kv_bufs = tuple(jnp.zeros(s.shape, s.dtype) for s in out_shape)
    aliased = list(kv_bufs)
    aliases = {n_fixed + j: j for j in range(len(aliased))}
    if attn_layout:
        out_specs += [pl.BlockSpec((rows, SUBLANES, tt), lambda b, i: (b, 0, i))]
        out_shape += [jax.ShapeDtypeStruct((bsz, SUBLANES, t), F32)]
    else:
        out_specs += [bt(LANES)]
        out_shape += [jax.ShapeDtypeStruct((bsz, t, LANES), F32)]
    if attn_layout:
        tiled = lambda n_, r_, w_: pl.BlockSpec((rows, FOX_HEADS, tt // n_, r_, w_), lambda b, i: (b, 0, i, 0, 0))
        out_specs += [tiled(FOX_QB, 2 * LANES, FOX_QB), bht(FOX_HEADS // 2, 2 * LANES),
                      tiled(FOX_KB, FOX_VT_ROWS, FOX_KB)]
        out_shape += [jax.ShapeDtypeStruct((bsz, FOX_HEADS, t // FOX_QB, 2 * LANES, FOX_QB), BF16),
                      jax.ShapeDtypeStruct((bsz, FOX_HEADS // 2, t, 2 * LANES), BF16),
                      jax.ShapeDtypeStruct((bsz, FOX_HEADS, t // FOX_KB, FOX_VT_ROWS, FOX_KB), BF16)]
    else:
        out_specs += [bt(FOX_WIDTH)]
        out_shape += [jax.ShapeDtypeStruct((bsz, t, FOX_WIDTH), BF16)]
    return pl.pallas_call(
        functools.partial(_fox_proj_body, tt=tt, rows=rows, attn_layout=attn_layout, n_alias=len(aliased)),
        grid=(bsz // rows, t // tt),
        in_specs=[bt(D_MODEL), _const_spec((1, D_MODEL)), _const_spec((D_MODEL, FOX_COLS), single=True),
                  _const_spec((1, FOX_WIDTH)), _const_spec((1, FOX_WIDTH)), _const_spec((1, LANES)),
                  _const_spec(pa.shape), _const_spec(arow.shape)]
                 + [pl.BlockSpec(memory_space=pl.ANY)] * len(aliased),
        out_specs=out_specs, out_shape=out_shape,
        input_output_aliases=aliases,
        scratch_shapes=[pltpu.VMEM((rows, 1, LANES), F32)],
        compiler_params=_params(("parallel", "arbitrary"), 40),
        name="fox_proj",
    )(x, nm, w, qn, kn, fb, pa, arow, *aliased)


def _fox_prompt_body(qt_ref, k_ref, vt_ref, o_ref, m_s, acc_s, s_s, *, nsub):
    qi = pl.program_id(2)
    qtile = nsub * FOX_QB
    chains = [(si, e) for si in range(nsub) for e in range(2)]
    m_s[...] = jnp.full_like(m_s, -jnp.inf)
    acc_s[...] = jnp.zeros_like(acc_s)

    nch = len(chains)

    def scores(c, k0, nk, masked):
        si, e = chains[c]
        s = _dot(k_ref[pl.ds(k0, nk), :], qt_ref[e, si])
        if masked:
            r, col = _tri_mask(FOX_QB)
            tail = jnp.where(r <= col, s[nk - FOX_QB:], -jnp.inf)
            s = tail if nk == FOX_QB else jnp.concatenate([s[:nk - FOX_QB], tail], axis=0)
        return s

    def absorb(c, s, vt0):
        e = chains[c][1]
        m_old = m_s[c]
        m_new = jnp.maximum(m_old, jnp.max(s, axis=0, keepdims=True))
        alpha = jnp.exp(m_old - m_new)
        p = jnp.exp(s - m_new).astype(BF16)
        vts = [vt_ref[e, vt0 + j] for j in range(s.shape[0] // FOX_KB)]
        vt = vts[0] if len(vts) == 1 else jnp.concatenate(vts, axis=1)
        acc_s[c] = alpha * acc_s[c] + _dot(vt, p)
        m_s[c] = m_new

    def sweep(k0, vt0, nks, masked):
        for c in range(nch):
            s_s[c, 0:nks[c], :] = scores(c, k0, nks[c], masked)
        for c in range(nch):
            absorb(c, s_s[c, 0:nks[c], :], vt0)

    def below_diagonal(i, carry):
        sweep(pl.multiple_of(i * FOX_SWEEP, FOX_SWEEP), i * (FOX_SWEEP // FOX_KB), [FOX_SWEEP] * nch, False)
        return carry

    lax.fori_loop(0, qi * (qtile // FOX_SWEEP), below_diagonal, 0)
    sweep(pl.multiple_of(qi * qtile, qtile), qi * (qtile // FOX_KB),
          [(si + 1) * FOX_QB for si, _ in chains], True)

    for si in range(nsub):
        outs = []
        for e in range(2):
            acc = acc_s[chains.index((si, e))]
            outs.append(acc[0:FOX_HEAD_DIM] / acc[FOX_HEAD_DIM:FOX_HEAD_DIM + 1])
        o = jnp.concatenate(outs, axis=0).T
        o_ref[si * FOX_QB:(si + 1) * FOX_QB, :] = o.astype(o_ref.dtype)


def _fox_prompt_call(qt, kp, vt, tq):
    bsz, _, nqb, _, _ = qt.shape
    t = kp.shape[2]
    nsub = _row_tile(nqb, max(FOX_SWEEP // FOX_QB, tq // FOX_QB))
    npair = FOX_HEADS // 2
    nch = 2 * nsub
    return pl.pallas_call(
        functools.partial(_fox_prompt_body, nsub=nsub), grid=(bsz, npair, nqb // nsub),
        in_specs=[pl.BlockSpec((None, 2, nsub, 2 * LANES, FOX_QB), lambda b, p, i: (b, p, i, 0, 0)),
                  pl.BlockSpec((None, None, t, 2 * LANES), lambda b, p, i: (b, p, 0, 0)),
                  pl.BlockSpec((None, 2, t // FOX_KB, FOX_VT_ROWS, FOX_KB), lambda b, p, i: (b, p, 0, 0, 0))],
        out_specs=pl.BlockSpec((None, nsub * FOX_QB, LANES), lambda b, p, i: (b, i, p)),
        out_shape=jax.ShapeDtypeStruct((bsz, t, FOX_WIDTH), BF16),
        scratch_shapes=[pltpu.VMEM((nch, 1, FOX_QB), F32),
                        pltpu.VMEM((nch, FOX_VT_ROWS, FOX_QB), F32),
                        pltpu.VMEM((nch, nsub * FOX_QB, FOX_QB), F32)],
        compiler_params=_params(("parallel", "parallel", "arbitrary"), 40),
        name="fox_attn_prompt",
    )(qt, kp, vt)


def _fox_sample_body(q_ref, kn_ref, vn_ref, lfn_ref, kc_ref, vc_ref, lfp_ref, o_ref, *, tn, past, seg):
    r, c = _tri_mask(seg)
    upper = (r <= c).astype(BF16)
    carry = jnp.zeros((FOX_HEADS, 1), F32)
    cps = []
    for j in range(past // seg):
        cs = carry + _dot_sel_rhs(lfp_ref[:, j * seg:(j + 1) * seg], upper)
        cps.append(cs)
        carry = cs[:, seg - 1:seg]
    cp = jnp.concatenate(cps, axis=1) if len(cps) > 1 else cps[0]
    lfn = lfn_ref[...]
    if tn < LANES:
        lfn = jnp.concatenate([lfn, jnp.zeros((LANES - tn, LANES), F32)], axis=0)
    r, c = _tri_mask(LANES)
    cn = _dot_sel_lhs((r >= c).astype(BF16), lfn)
    cn_t = cn.T

    q = q_ref[...]
    kn = kn_ref[...].astype(BF16)
    vn = vn_ref[...].astype(BF16)
    r, c = _tri_mask(tn)
    outs = []
    for hh in range(FOX_HEADS):
        hs = slice(hh * FOX_HEAD_DIM, (hh + 1) * FOX_HEAD_DIM)
        qh = q[:, hs]
        cq = cn[0:tn, hh:hh + 1]
        s_past = _dot(qh, kc_ref[hh].astype(BF16)) + ((carry[hh:hh + 1, :] + cq) - cp[hh:hh + 1, :])
        s_new = _dot_nt(qh, kn[:, hs]) + (cq - cn_t[hh:hh + 1, 0:tn])
        s_new = jnp.where(r >= c, s_new, -jnp.inf)
        m = jnp.maximum(jnp.max(s_past, axis=-1, keepdims=True), jnp.max(s_new, axis=-1, keepdims=True))
        p_past = jnp.exp(s_past - m)
        p_new = jnp.exp(s_new - m)
        denom = jnp.sum(p_past, axis=-1, keepdims=True) + jnp.sum(p_new, axis=-1, keepdims=True)
        o = _dot_nt(p_past.astype(BF16), vc_ref[hh].astype(BF16)) + _dot(p_new.astype(BF16), vn[:, hs])
        outs.append(o / denom)
    o_ref[...] = jnp.concatenate(outs, axis=1).astype(o_ref.dtype)


def _fox_sample_call(q, kn_all, vn_all, lfn, cache_k, cache_v, lfp_t, layer):
    bsz, tn, _ = q.shape
    past = cache_k.shape[-1]
    seg = _row_tile(past, 512)
    bt = lambda w_: pl.BlockSpec((None, tn, w_), lambda b: (b, 0, 0))
    new = pl.BlockSpec((None, None, tn, FOX_WIDTH), lambda b: (layer, b, 0, 0))
    cache = pl.BlockSpec((None, None, FOX_HEADS, FOX_HEAD_DIM, past), lambda b: (layer, b, 0, 0, 0))
    return pl.pallas_call(
        functools.partial(_fox_sample_body, tn=tn, past=past, seg=seg), grid=(bsz,),
        in_specs=[bt(FOX_WIDTH), new, new, bt(LANES), cache, cache,
                  pl.BlockSpec((None, None, FOX_HEADS, past), lambda b: (layer, b, 0, 0))],
        out_specs=bt(FOX_WIDTH),
        out_shape=jax.ShapeDtypeStruct((bsz, tn, FOX_WIDTH), BF16),
        compiler_params=_params(("parallel",), 48),
        name="fox_attn_sample",
    )(q, kn_all, vn_all, lfn, cache_k, cache_v, lfp_t)


def _pad_cols(a, width):
    return jnp.pad(a, ((0, 0), (0, width - a.shape[1])))


def _layer_params(l, norm_ffn1, w1_gate, w1_up, w1_down, norm_mix, w_in, ssd_conv_w, ssd_conv_b,
                  ssd_dt_bias, ssd_a_log, ssd_d, ssd_norm, gla_w_gate, gla_b_gate, gla_norm,
                  fox_q_norm, fox_k_norm, fox_f_bias, w_out, norm_ffn2, w2_gate, w2_up, w2_down):
    wi = w_in[l]
    row = lambda a: a.reshape(1, -1).astype(F32)
    hk = GLA_HEADS * GLA_DK
    p = dict(
        ffn1=(row(norm_ffn1[l]), w1_gate[l].astype(BF16), w1_up[l].astype(BF16), w1_down[l].astype(BF16)),
        ffn2=(row(norm_ffn2[l]), w2_gate[l].astype(BF16), w2_up[l].astype(BF16), w2_down[l].astype(BF16)),
        norm_mix=row(norm_mix[l]),
        w_out=w_out[l].astype(BF16),
        w_ssd=jnp.concatenate([wi[:, _OFF_Z:_OFF_DT], _pad_cols(wi[:, _OFF_DT:_OFF_GQ], LANES)], axis=1).astype(BF16),
        conv_w=ssd_conv_w[l].astype(F32),
        conv_b=row(ssd_conv_b[l]),
        dt_bias=_pad_cols(row(ssd_dt_bias[l]), LANES),
        a_log=_pad_cols(row(ssd_a_log[l]), LANES),
        d_exp=row(jnp.repeat(ssd_d[l], SSD_HEAD_DIM)),
        ssd_norm=row(ssd_norm[l]),
        w_gla=jnp.concatenate([wi[:, _OFF_GQ:_OFF_GR], _pad_cols(wi[:, _OFF_GR:_OFF_FQ], LANES)], axis=1).astype(BF16),
        w_gate=jnp.pad(gla_w_gate[l], ((0, LANES - GLA_RANK), (0, 0))).astype(BF16),
        b_gate=row(gla_b_gate[l]),
        gla_norm=row(jnp.tile(gla_norm[l], GLA_HEADS)),
        w_fox=jnp.concatenate([wi[:, _OFF_FQ:_OFF_FF], _pad_cols(wi[:, _OFF_FF:], LANES)], axis=1).astype(BF16),
        q_norm=row(jnp.tile(fox_q_norm[l], FOX_HEADS)),
        k_norm=row(jnp.tile(fox_k_norm[l], FOX_HEADS)),
        f_bias=_pad_cols(row(fox_f_bias[l]), LANES),
    )
    return p


def _gla_state_in(s):
    bsz = s.shape[0]
    eye = jnp.eye(GLA_HEADS, dtype=s.dtype)
    full = jnp.einsum('bhkv,hg->bhvgk', s, eye)
    return full.reshape(bsz, GLA_HEADS * GLA_DV, GLA_HEADS * GLA_DK)


def _gla_state_out(st):
    bsz = st.shape[0]
    full = st.reshape(bsz, GLA_HEADS, GLA_DV, GLA_HEADS, GLA_DK)
    idx = jnp.arange(GLA_HEADS)
    diag = full[:, idx, :, idx, :]
    return jnp.transpose(diag, (1, 0, 3, 2))


def _mix(x, p, conv_prev, ssd_h0, gla_s0, fox_cache, layer, depth, kv_bufs, tiles):
    bsz, t, _ = x.shape
    cprev = jnp.pad(conv_prev, ((0, 0), (SUBLANES - (SSD_CONV - 1), 0), (0, 0)))
    y_ssd, cnew, hnew = _ssd_call(x, p['norm_mix'], p['w_ssd'], p['conv_w'], p['conv_b'], p['dt_bias'],
                                  p['a_log'], p['d_exp'], p['ssd_norm'], cprev,
                                  ssd_h0.reshape(bsz, SSD_WIDTH, SSD_STATE), tiles['ssd'], tiles['ssd_rows'])
    o_gla, snew = _gla_call(x, p['norm_mix'], p['w_gla'], p['w_gate'], p['b_gate'], p['gla_norm'],
                            _gla_state_in(gla_s0), tiles['gla'], tiles['gla_rows'])
    proj = functools.partial(_fox_proj_call, x, p['norm_mix'], p['w_fox'], p['q_norm'], p['k_norm'], p['f_bias'],
                             tiles['fox_proj'], tiles['fox_rows'], layer=layer, depth=depth, kv_bufs=kv_bufs)
    if fox_cache is None:
        k_all, v_all, lf_t, qt, kp, vt = proj(attn_layout=True)
        o_fox = _fox_prompt_call(qt, kp, vt, tiles['fox_q'])
        lf_state = jnp.swapaxes(lf_t[:, 0:FOX_HEADS, :], 1, 2)
    else:
        k_all, v_all, lf, qs = proj(attn_layout=False)
        cache_k, cache_v, lfp_t = fox_cache
        o_fox = _fox_sample_call(qs, k_all, v_all, lf, cache_k, cache_v, lfp_t, layer)
        lf_state = lf[:, :, 0:FOX_HEADS]
    state = (cnew[:, SUBLANES - (SSD_CONV - 1):, :],
             hnew.reshape(bsz, SSD_HEADS, SSD_HEAD_DIM, SSD_STATE),
             _gla_state_out(snew),
             lf_state)
    return (y_ssd, o_gla, o_fox), state, (k_all, v_all)


def _trunk_layer(x, p, conv_prev, ssd_h0, gla_s0, fox_cache, layer, depth, kv_bufs, tiles):
    bsz, t, d = x.shape
    x1 = _ffn_call(x.reshape(bsz * t, d), *p['ffn1'], tiles['ffn']).reshape(bsz, t, d)
    (y_ssd, o_gla, o_fox), state, kv_bufs = _mix(x1, p, conv_prev, ssd_h0, gla_s0, fox_cache, layer, depth,
                                                 kv_bufs, tiles)
    flat = lambda a: a.reshape(bsz * t, a.shape[-1])
    x3 = _out_ffn_call(flat(x1), flat(y_ssd), flat(o_gla), flat(o_fox), p['w_out'], *p['ffn2'], tiles['ffn'])
    return x3.reshape(bsz, t, d), state, kv_bufs


PROMPT_TILES = dict(ffn=512, ssd=256, ssd_rows=4, gla=256, gla_rows=4, fox_proj=512, fox_rows=2,
                    fox_q=1024)
SAMPLE_TILES = dict(ffn=512, ssd=64, ssd_rows=4, gla=64, gla_rows=4, fox_proj=64, fox_rows=8, fox_q=64)


def kernel(x_prompt, x_sample, state_ssd_conv, state_ssd, state_gla, cache_fox_k, cache_fox_v, cache_fox_logf, norm_ffn1, w1_gate, w1_up, w1_down, norm_mix, w_in, ssd_conv_w, ssd_conv_b, ssd_dt_bias, ssd_a_log, ssd_d, ssd_norm, gla_w_gate, gla_b_gate, gla_norm, fox_q_norm, fox_k_norm, fox_f_bias, w_out, norm_ffn2, w2_gate, w2_up, w2_down):
    depth = w_in.shape[0]
    bp = x_prompt.shape[0]
    weights = (norm_ffn1, w1_gate, w1_up, w1_down, norm_mix, w_in, ssd_conv_w, ssd_conv_b, ssd_dt_bias,
               ssd_a_log, ssd_d, ssd_norm, gla_w_gate, gla_b_gate, gla_norm, fox_q_norm, fox_k_norm,
               fox_f_bias, w_out, norm_ffn2, w2_gate, w2_up, w2_down)
    cache_k = jnp.transpose(cache_fox_k, (0, 1, 3, 4, 2))
    cache_v = jnp.transpose(cache_fox_v, (0, 1, 3, 4, 2))
    lfp_t = jnp.swapaxes(cache_fox_logf, 2, 3)
    zeros_conv = jnp.zeros((bp, SSD_CONV - 1, SSD_CONV_CH), F32)
    zeros_ssd = jnp.zeros((bp, SSD_HEADS, SSD_HEAD_DIM, SSD_STATE), F32)
    zeros_gla = jnp.zeros((bp, GLA_HEADS, GLA_DK, GLA_DV), F32)

    xp, xs = x_prompt, x_sample
    p_new = [[] for _ in range(4)]
    s_new = [[] for _ in range(4)]
    kv_p = kv_s = None
    for l in range(depth):
        p = _layer_params(l, *weights)
        xp, st_p, kv_p = _trunk_layer(xp, p, zeros_conv, zeros_ssd, zeros_gla, None, l, depth, kv_p, PROMPT_TILES)
        xs, st_s, kv_s = _trunk_layer(xs, p, state_ssd_conv[l], state_ssd[l], state_gla[l],
                                      (cache_k, cache_v, lfp_t), l, depth, kv_s, SAMPLE_TILES)
        for i in range(4):
            p_new[i].append(st_p[i])
            s_new[i].append(st_s[i])

    def leaves(small, kv):
        conv, ssd, gla, logf = [jnp.stack(a) for a in small]
        heads = lambda a: a.reshape(a.shape[:3] + (FOX_HEADS, FOX_HEAD_DIM))
        return conv, ssd, gla, heads(kv[0]), heads(kv[1]), logf

    return (xp, xs, *leaves(p_new, kv_p), *leaves(s_new, kv_s))
```

```python
import functools

import numpy as np
import jax
import jax.numpy as jnp
from jax import lax
from jax.experimental import pallas as pl
from jax.experimental.pallas import tpu as pltpu

F32 = jnp.float32
BF16 = jnp.bfloat16

EPS = 1e-6
D_MODEL = 1024
D_FF = 2816
SSD_HEADS = 8
SSD_HEAD_DIM = 64
SSD_WIDTH = 512
SSD_GROUPS = 2
SSD_STATE = 128
SSD_CONV = 4
SSD_CONV_CH = 1024
GLA_HEADS = 4
GLA_DK = 32
GLA_DV = 64
GLA_WIDTH = 256
GLA_RANK = 16
GLA_TAU = 16.0
GLA_CHUNK = 64
FOX_HEADS = 4
FOX_HEAD_DIM = 64
FOX_WIDTH = 256
LANES = 128
SUBLANES = 8
MIB = 1024 * 1024

_OFF_Z, _OFF_XBC, _OFF_DT = 0, 512, 1536
_OFF_GQ, _OFF_GR = 1544, 2312
_OFF_FQ, _OFF_FF = 2328, 3096
SSD_COLS = SSD_WIDTH + SSD_CONV_CH + LANES
GLA_COLS = 2 * GLA_HEADS * GLA_DK + 2 * GLA_WIDTH + LANES
FOX_COLS = 3 * FOX_WIDTH + LANES
AUX_PARTS = 3
FOX_QB = 256
FOX_KB = 256
FOX_SWEEP = 1024
FOX_VT_ROWS = FOX_HEAD_DIM + 16


def _dot(a, b):
    return jnp.dot(a, b, preferred_element_type=F32)


def _dot_nt(a, b):
    return lax.dot_general(a, b, (((1,), (1,)), ((), ())), preferred_element_type=F32)


def _dot_tn(a, b):
    return lax.dot_general(a, b, (((0,), (0,)), ((), ())), preferred_element_type=F32)


def _split_bf16(x, parts):
    out = []
    r = x
    for i in range(parts):
        p = r.astype(BF16)
        out.append(p)
        if i + 1 < parts:
            r = r - p.astype(F32)
    return out


def _dot_sel_lhs(sel, x, parts=3):
    n = x.shape[1]
    t = _dot(sel, jnp.concatenate(_split_bf16(x, parts), axis=1))
    acc = t[:, 0:n]
    for i in range(1, parts):
        acc = acc + t[:, i * n:(i + 1) * n]
    return acc


def _dot_sel_rhs(x, sel, parts=3):
    m = x.shape[0]
    pieces = _split_bf16(x, parts)
    if m % (2 * SUBLANES):
        acc = _dot(pieces[0], sel)
        for p in pieces[1:]:
            acc = acc + _dot(p, sel)
        return acc
    t = _dot(jnp.concatenate(pieces, axis=0), sel)
    acc = t[0:m]
    for i in range(1, parts):
        acc = acc + t[i * m:(i + 1) * m]
    return acc


def _rms(x, w):
    ms = jnp.mean(x * x, axis=-1, keepdims=True)
    return x * lax.rsqrt(ms + EPS) * w


def _silu(x):
    return x * jax.nn.sigmoid(x)


def _softplus(x):
    return jnp.maximum(x, 0.0) + jnp.log1p(jnp.exp(-jnp.abs(x)))


def _log_sigmoid(x):
    return -_softplus(-x)


def _tri_mask(n, m=None):
    m = n if m is None else m
    r = lax.broadcasted_iota(jnp.int32, (n, m), 0)
    c = lax.broadcasted_iota(jnp.int32, (n, m), 1)
    return r, c


def _lane_pair_select(a_even, a_odd):
    lane = lax.broadcasted_iota(jnp.int32, a_even.shape, 1)
    return jnp.where(lane < 64, a_even, a_odd)


def _expand_heads(v, heads, rows):
    pieces = []
    for i in range(0, len(heads), 2):
        a = jnp.broadcast_to(v[:, heads[i]:heads[i] + 1], (rows, LANES))
        b = jnp.broadcast_to(v[:, heads[i + 1]:heads[i + 1] + 1], (rows, LANES))
        pieces.append(_lane_pair_select(a, b))
    return pieces[0] if len(pieces) == 1 else jnp.concatenate(pieces, axis=1)


def _group_mean_matrix(width, group):
    r, c = _tri_mask(width)
    return jnp.where((r // group) == (c // group), 1.0 / group, 0.0).astype(BF16)


def _project_rows(h, w_ref, tt, rows):
    if rows == 1 or tt < 2 * LANES:
        return _dot(h, w_ref[...])
    return jnp.concatenate([_dot(h[i * tt:(i + 1) * tt], w_ref[...]) for i in range(rows)], axis=0)


VMEM_LIMIT_MIB = dict(ffn=56, out_ffn=56, ssd=48, gla=32, fox_proj=40, fox_attn_prompt=40, fox_attn_sample=48)


def _params(sem, call):
    return pltpu.CompilerParams(dimension_semantics=sem, vmem_limit_bytes=VMEM_LIMIT_MIB[call] * MIB)


def _const_spec(shape, single=False):
    nd = len(shape)
    if single:
        return pl.BlockSpec(shape, lambda *_: (0,) * nd, pipeline_mode=pl.Buffered(1))
    return pl.BlockSpec(shape, lambda *_: (0,) * nd)


def _swiglu_half(x, g_ref, wg_ref, wu_ref, wd_ref):
    h = _rms(x, g_ref[...]).astype(BF16)
    gate = _dot(h, wg_ref[...])
    up = _dot(h, wu_ref[...])
    a = (_silu(gate) * up).astype(BF16)
    return x + 0.5 * _dot(a, wd_ref[...])


def _ffn_body(x_ref, g_ref, wg_ref, wu_ref, wd_ref, o_ref):
    o_ref[...] = _swiglu_half(x_ref[...], g_ref, wg_ref, wu_ref, wd_ref)


def _out_ffn_body(x_ref, ys_ref, og_ref, of_ref, wo_ref, g_ref, wg_ref, wu_ref, wd_ref, o_ref):
    x = x_ref[...]
    x = x + (_dot(ys_ref[...], wo_ref[0:SSD_WIDTH, :])
             + _dot(og_ref[...], wo_ref[SSD_WIDTH:SSD_WIDTH + GLA_WIDTH, :])
             + _dot(of_ref[...], wo_ref[SSD_WIDTH + GLA_WIDTH:, :]))
    o_ref[...] = _swiglu_half(x, g_ref, wg_ref, wu_ref, wd_ref)


def _ffn_weight_specs():
    return [_const_spec((1, D_MODEL)),
            _const_spec((D_MODEL, D_FF), single=True),
            _const_spec((D_MODEL, D_FF), single=True),
            _const_spec((D_FF, D_MODEL), single=True)]


def _row_tile(n, want):
    t = min(want, n)
    while n % t:
        t //= 2
    return t


def _ffn_call(x, g, wg, wu, wd, tm):
    n = x.shape[0]
    tm = _row_tile(n, tm)
    row = pl.BlockSpec((tm, D_MODEL), lambda i: (i, 0))
    return pl.pallas_call(
        _ffn_body, grid=(n // tm,),
        in_specs=[row] + _ffn_weight_specs(),
        out_specs=row,
        out_shape=jax.ShapeDtypeStruct((n, D_MODEL), F32),
        compiler_params=_params(("parallel",), "ffn"),
        name="ffn",
    )(x, g, wg, wu, wd)


def _out_ffn_call(x, ys, og, of, wo, g, wg, wu, wd, tm):
    n = x.shape[0]
    tm = _row_tile(n, tm)
    row = lambda w: pl.BlockSpec((tm, w), lambda i: (i, 0))
    return pl.pallas_call(
        _out_ffn_body, grid=(n // tm,),
        in_specs=[row(D_MODEL), row(SSD_WIDTH), row(GLA_WIDTH), row(FOX_WIDTH),
                  _const_spec((D_MODEL, D_MODEL), single=True)] + _ffn_weight_specs(),
        out_specs=row(D_MODEL),
        out_shape=jax.ShapeDtypeStruct((n, D_MODEL), F32),
        compiler_params=_params(("parallel",), "out_ffn"),
        name="out_ffn",
    )(x, ys, og, of, wo, g, wg, wu, wd)


def _ssd_body(x_ref, nm_ref, w_ref, cw_ref, cb_ref, dtb_ref, alog_ref, dexp_ref, nw_ref,
              cprev_ref, h0_ref, y_ref, cnew_ref, hnew_ref, xbuf, hst, *, tt, rows):
    t = pl.program_id(1)

    @pl.when(t == 0)
    def _():
        xbuf[...] = cprev_ref[...]
        hst[...] = h0_ref[...]

    x = x_ref[...].reshape(rows * tt, D_MODEL)
    h = _rms(x, nm_ref[...]).astype(BF16)
    u = _project_rows(h, w_ref, tt, rows)
    z = u[:, 0:SSD_WIDTH]
    xbc = u[:, SSD_WIDTH:SSD_WIDTH + SSD_CONV_CH]
    dt_raw = u[:, SSD_WIDTH + SSD_CONV_CH:]

    convs = []
    row8 = lax.broadcasted_iota(jnp.int32, (SUBLANES, SSD_CONV_CH), 0)
    for i in range(rows):
        xi = xbc[i * tt:(i + 1) * tt]
        prev = xbuf[i]
        conv = None
        for j in range(SSD_CONV - 1, 0, -1):
            rolled = pltpu.roll(xi, j, 0)
            head = jnp.where(row8 < j, pltpu.roll(prev, j, 0), rolled[0:SUBLANES])
            term = jnp.concatenate([head, rolled[SUBLANES:]], axis=0) * cw_ref[SSD_CONV - 1 - j:SSD_CONV - j, :]
            conv = term if conv is None else conv + term
        convs.append(conv + xi * cw_ref[SSD_CONV - 1:SSD_CONV, :])
        xbuf[i] = xi[tt - SUBLANES:, :]
    xa = _silu(jnp.concatenate(convs, axis=0) + cb_ref[...])

    xs = xa[:, 0:SSD_WIDTH]
    bm = xa[:, SSD_WIDTH:SSD_WIDTH + SSD_GROUPS * SSD_STATE].astype(BF16)
    cm = xa[:, SSD_WIDTH + SSD_GROUPS * SSD_STATE:].astype(BF16)
    dt = _softplus(dt_raw + dtb_ref[...])
    a = dt * (-jnp.exp(alog_ref[...]))

    r, c = _tri_mask(tt)
    causal = r >= c
    tri = causal.astype(BF16)
    acum = jnp.concatenate(
        [_dot_sel_lhs(tri, a[i * tt:(i + 1) * tt]) for i in range(rows)], axis=0)
    shift_t = (acum - jnp.log(dt)).T
    alast = jnp.concatenate(
        [jnp.broadcast_to(acum[(i + 1) * tt - 1:(i + 1) * tt, :], (tt, LANES)) for i in range(rows)], axis=0)
    ea = jnp.exp(acum)
    wend = jnp.exp(alast - acum) * dt

    heads_per_group = SSD_HEADS // SSD_GROUPS
    gw = heads_per_group * SSD_HEAD_DIM
    heads_of = [list(range(g * heads_per_group, (g + 1) * heads_per_group)) for g in range(SSD_GROUPS)]
    nrow = rows * tt
    xb = xs.astype(BF16)
    xw = jnp.concatenate([xs[:, g * gw:(g + 1) * gw] * _expand_heads(wend, heads_of[g], nrow)
                          for g in range(SSD_GROUPS)], axis=1).astype(BF16)
    ea_x = jnp.concatenate([_expand_heads(ea, heads_of[g], nrow) for g in range(SSD_GROUPS)], axis=1)

    units = [(i, g) for g in range(SSD_GROUPS) for i in range(rows)]
    rs = {i: slice(i * tt, (i + 1) * tt) for i in range(rows)}
    gs = {g: slice(g * SSD_STATE, (g + 1) * SSD_STATE) for g in range(SSD_GROUPS)}
    cbs = {(i, g): _dot_nt(cm[rs[i], gs[g]], bm[rs[i], gs[g]]) for i, g in units}
    upds = {(i, g): _dot_tn(xw[rs[i], g * gw:(g + 1) * gw], bm[rs[i], gs[g]]) for i, g in units}
    y_unit = {}
    for i, g in units:
        heads = heads_of[g]
        intra = []
        for pi in range(heads_per_group // 2):
            ys = []
            for e in range(2):
                hh = heads[2 * pi + e]
                seg = acum[rs[i], hh:hh + 1] - shift_t[hh:hh + 1, rs[i]]
                m = (cbs[(i, g)] * jnp.where(causal, jnp.exp(seg), 0.0)).astype(BF16)
                col = g * gw + pi * LANES
                ys.append(_dot(m, xb[rs[i], col:col + LANES]))
            intra.append(_lane_pair_select(ys[0], ys[1]))
        hg = hst[i, g * gw:(g + 1) * gw, :]
        y_inter = _dot_nt(cm[rs[i], gs[g]], hg.astype(BF16)) * ea_x[rs[i], g * gw:(g + 1) * gw]
        y_unit[(i, g)] = jnp.concatenate(intra, axis=1) + y_inter
        cd = jnp.concatenate(
            [jnp.broadcast_to(jnp.exp(acum[(i + 1) * tt - 1:(i + 1) * tt, hh:hh + 1]), (SSD_HEAD_DIM, SSD_STATE))
             for hh in heads], axis=0)
        hst[i, g * gw:(g + 1) * gw, :] = hg * cd + upds[(i, g)]

    y = jnp.concatenate(
        [jnp.concatenate([y_unit[(i, g)] for g in range(SSD_GROUPS)], axis=1) for i in range(rows)], axis=0)
    y = (y + dexp_ref[...] * xs) * _silu(z)
    nw = nw_ref[...]
    outs = [_rms(y[:, g * gw:(g + 1) * gw], nw[:, g * gw:(g + 1) * gw]) for g in range(SSD_GROUPS)]
    y_ref[...] = jnp.concatenate(outs, axis=1).astype(y_ref.dtype).reshape(rows, tt, SSD_WIDTH)

    @pl.when(t == pl.num_programs(1) - 1)
    def _():
        cnew_ref[...] = xbuf[...]
        hnew_ref[...] = hst[...]


def _ssd_call(x, nm, w, cw, cb, dtb, alog, dexp, nw, cprev, h0, tt, rows):
    bsz, t, _ = x.shape
    tt = _row_tile(t, tt)
    rows = _row_tile(bsz, rows)
    bt = lambda w_: pl.BlockSpec((rows, tt, w_), lambda b, i: (b, i, 0))
    per_b = lambda r_, w_: pl.BlockSpec((rows, r_, w_), lambda b, i: (b, 0, 0))
    return pl.pallas_call(
        functools.partial(_ssd_body, tt=tt, rows=rows), grid=(bsz // rows, t // tt),
        in_specs=[bt(D_MODEL), _const_spec((1, D_MODEL)), _const_spec((D_MODEL, SSD_COLS), single=True),
                  _const_spec((SSD_CONV, SSD_CONV_CH)), _const_spec((1, SSD_CONV_CH)),
                  _const_spec((1, LANES)), _const_spec((1, LANES)),
                  _const_spec((1, SSD_WIDTH)), _const_spec((1, SSD_WIDTH)),
                  per_b(SUBLANES, SSD_CONV_CH), per_b(SSD_WIDTH, SSD_STATE)],
        out_specs=[bt(SSD_WIDTH), per_b(SUBLANES, SSD_CONV_CH), per_b(SSD_WIDTH, SSD_STATE)],
        out_shape=[jax.ShapeDtypeStruct((bsz, t, SSD_WIDTH), BF16),
                   jax.ShapeDtypeStruct((bsz, SUBLANES, SSD_CONV_CH), F32),
                   jax.ShapeDtypeStruct((bsz, SSD_WIDTH, SSD_STATE), F32)],
        scratch_shapes=[pltpu.VMEM((rows, SUBLANES, SSD_CONV_CH), F32),
                        pltpu.VMEM((rows, SSD_WIDTH, SSD_STATE), F32)],
        compiler_params=_params(("parallel", "arbitrary"), "ssd"),
        name="ssd",
    )(x, nm, w, cw, cb, dtb, alog, dexp, nw, cprev, h0)


def _gla_body(x_ref, nm_ref, w_ref, wgate_ref, bgate_ref, gn_ref, s0_ref, o_ref, snew_ref, st, *, tt, rows):
    t = pl.program_id(1)

    @pl.when(t == 0)
    def _():
        st[...] = s0_ref[...]

    hk = GLA_HEADS * GLA_DK
    nchunk = tt // GLA_CHUNK
    x = x_ref[...].reshape(rows * tt, D_MODEL)
    h = _rms(x, nm_ref[...]).astype(BF16)
    u = _project_rows(h, w_ref, tt, rows)
    q = u[:, 0:hk] * (GLA_DK ** -0.5)
    k = u[:, hk:2 * hk]
    v = u[:, 2 * hk:2 * hk + GLA_WIDTH].astype(BF16)
    gg = u[:, 2 * hk + GLA_WIDTH:2 * hk + 2 * GLA_WIDTH]
    gr = u[:, 2 * hk + 2 * GLA_WIDTH:].astype(BF16)
    la = _log_sigmoid(_dot(gr, wgate_ref[...]) + bgate_ref[...]) / GLA_TAU

    r, c = _tri_mask(tt)
    sel = (((r // GLA_CHUNK) == (c // GLA_CHUNK)) & (r >= c)).astype(BF16)
    bcum = jnp.concatenate(
        [_dot_sel_lhs(sel, la[i * tt:(i + 1) * tt]) for i in range(rows)], axis=0)
    blast = jnp.concatenate(
        [jnp.broadcast_to(bcum[(ci + 1) * GLA_CHUNK - 1:(ci + 1) * GLA_CHUNK, :], (GLA_CHUNK, hk))
         for ci in range(rows * nchunk)], axis=0)
    qd = (q * jnp.exp(bcum)).astype(BF16)
    kd = (k * jnp.exp(-bcum)).astype(BF16)
    kend = (k * jnp.exp(blast - bcum)).astype(BF16)
    eblast = jnp.exp(blast)

    lane_k = lax.broadcasted_iota(jnp.int32, (GLA_CHUNK, hk), 1) // GLA_DK
    lane_v = lax.broadcasted_iota(jnp.int32, (GLA_CHUNK, GLA_WIDTH), 1) // GLA_DV
    ar, ac = _tri_mask(GLA_HEADS * GLA_CHUNK, GLA_CHUNK)
    att_causal = (ar % GLA_CHUNK) >= ac
    sr, sc = _tri_mask(GLA_WIDTH, hk)
    diag = (sr // GLA_DV) == (sc // GLA_DK)

    units = [(i, ci) for ci in range(nchunk) for i in range(rows)]
    sl = {(i, ci): slice(i * tt + ci * GLA_CHUNK, i * tt + (ci + 1) * GLA_CHUNK) for i, ci in units}
    att, upd, o_intra = {}, {}, {}
    for un in units:
        qd_c = qd[sl[un]]
        lhs = jnp.concatenate([jnp.where(lane_k == hh, qd_c, jnp.zeros_like(qd_c))
                               for hh in range(GLA_HEADS)], axis=0)
        att[un] = jnp.where(att_causal, _dot_nt(lhs, kd[sl[un]]), 0.0).astype(BF16)
        upd[un] = jnp.where(diag, _dot_tn(v[sl[un]], kend[sl[un]]), 0.0)
    for un in units:
        res = _dot(att[un], v[sl[un]])
        acc = jnp.zeros((GLA_CHUNK, GLA_WIDTH), F32)
        for hh in range(GLA_HEADS):
            acc = jnp.where(lane_v == hh, res[hh * GLA_CHUNK:(hh + 1) * GLA_CHUNK], acc)
        o_intra[un] = acc
    outs = {}
    for un in units:
        i = un[0]
        s_prev = st[i]
        outs[un] = o_intra[un] + _dot_nt(qd[sl[un]], s_prev.astype(BF16))
        st[i] = s_prev * eblast[sl[un].start:sl[un].start + 1, :] + upd[un]

    o = jnp.concatenate([outs[(i, ci)] for i in range(rows) for ci in range(nchunk)], axis=0)
    ms = _dot_sel_rhs(o * o, _group_mean_matrix(GLA_WIDTH, GLA_DV), parts=2)
    o = o * lax.rsqrt(ms + EPS) * gn_ref[...]
    o_ref[...] = (o * _silu(gg)).astype(o_ref.dtype).reshape(rows, tt, GLA_WIDTH)

    @pl.when(t == pl.num_programs(1) - 1)
    def _():
        snew_ref[...] = st[...]


def _gla_call(x, nm, w, wgate, bgate, gn, s0, tt, rows):
    bsz, t, _ = x.shape
    tt = _row_tile(t, tt)
    rows = _row_tile(bsz, rows)
    hk = GLA_HEADS * GLA_DK
    bt = lambda w_: pl.BlockSpec((rows, tt, w_), lambda b, i: (b, i, 0))
    per_b = pl.BlockSpec((rows, GLA_WIDTH, hk), lambda b, i: (b, 0, 0))
    return pl.pallas_call(
        functools.partial(_gla_body, tt=tt, rows=rows), grid=(bsz // rows, t // tt),
        in_specs=[bt(D_MODEL), _const_spec((1, D_MODEL)), _const_spec((D_MODEL, GLA_COLS), single=True),
                  _const_spec((LANES, hk)), _const_spec((1, hk)), _const_spec((1, GLA_WIDTH)), per_b],
        out_specs=[bt(GLA_WIDTH), per_b],
        out_shape=[jax.ShapeDtypeStruct((bsz, t, GLA_WIDTH), BF16),
                   jax.ShapeDtypeStruct((bsz, GLA_WIDTH, hk), F32)],
        scratch_shapes=[pltpu.VMEM((rows, GLA_WIDTH, hk), F32)],
        compiler_params=_params(("parallel", "arbitrary"), "gla"),
        name="gla",
    )(x, nm, w, wgate, bgate, gn, s0)


def _fox_proj_body(x_ref, nm_ref, w_ref, qn_ref, kn_ref, fb_ref, pa_ref, arow_ref, *rest,
                   tt, rows, attn_layout, n_alias):
    k_ref, v_ref, lf_ref, *rest = rest[n_alias:]
    if attn_layout:
        qt_ref, kp_ref, vt_ref, carry = rest
    else:
        qs_ref, carry = rest
    t = pl.program_id(1)

    @pl.when(t == 0)
    def _():
        carry[...] = jnp.zeros_like(carry)

    nrow = rows * tt
    h = _rms(x_ref[...].reshape(nrow, D_MODEL), nm_ref[...]).astype(BF16)
    u = _project_rows(h, w_ref, tt, rows)
    fq = u[:, 0:FOX_WIDTH]
    fk = u[:, FOX_WIDTH:2 * FOX_WIDTH]
    fv = u[:, 2 * FOX_WIDTH:3 * FOX_WIDTH]
    ff = u[:, 3 * FOX_WIDTH:]

    gmat = _group_mean_matrix(FOX_WIDTH, FOX_HEAD_DIM)
    qn = fq * lax.rsqrt(_dot_sel_rhs(fq * fq, gmat, parts=2) + EPS) * qn_ref[...]
    kn = fk * lax.rsqrt(_dot_sel_rhs(fk * fk, gmat, parts=2) + EPS) * kn_ref[...]
    lf = _log_sigmoid(ff + fb_ref[...])
    k_ref[...] = kn.reshape(rows, tt, FOX_WIDTH)
    v_ref[...] = fv.reshape(rows, tt, FOX_WIDTH)
    qs = (qn * (FOX_HEAD_DIM ** -0.5)).astype(BF16)

    if not attn_layout:
        lf_ref[...] = lf.reshape(rows, tt, LANES)
        qs_ref[...] = qs.reshape(rows, tt, FOX_WIDTH)
        return

    r, c = _tri_mask(tt)
    tri = (r >= c).astype(BF16)
    cums = []
    for i in range(rows):
        lf_i = lf[i * tt:(i + 1) * tt]
        lf_ref[i] = lf_i.T[0:SUBLANES]
        cum_i = carry[i] + _dot_sel_lhs(tri, lf_i)
        carry[i] = cum_i[tt - 1:tt, :]
        cums.append(cum_i)
    cum = jnp.concatenate(cums, axis=0) if rows > 1 else cums[0]
    c3 = jnp.concatenate(_split_bf16(cum, AUX_PARTS), axis=1)
    aux = _dot(c3, pa_ref[...])
    aux_k = (aux[:, 0:LANES] + arow_ref[0:1, :]).astype(BF16)
    aux_q_all = aux[:, LANES:]
    knb = kn.astype(BF16)
    fv_t = fv.T
    ones = jnp.ones((FOX_VT_ROWS - FOX_HEAD_DIM, nrow), F32)
    lane = lax.broadcasted_iota(jnp.int32, (nrow, LANES), 1)
    for p in range(FOX_HEADS // 2):
        sl = slice(p * LANES, (p + 1) * LANES)
        for i in range(rows):
            kp_ref[i, p, :, 0:LANES] = knb[i * tt:(i + 1) * tt, sl]
            kp_ref[i, p, :, LANES:] = aux_k[i * tt:(i + 1) * tt]
        for e in range(2):
            hh = 2 * p + e
            vt = jnp.concatenate([fv_t[hh * FOX_HEAD_DIM:(hh + 1) * FOX_HEAD_DIM, :], ones], axis=0).astype(BF16)
            qmask = jnp.where((lane // FOX_HEAD_DIM) == e, qs[:, sl], jnp.zeros_like(qs[:, sl])).astype(F32)
            aux_q = jnp.where((lane // AUX_PARTS) == hh, aux_q_all, 0.0) + arow_ref[1 + hh:2 + hh, :]
            q_t = jnp.concatenate([qmask.T, aux_q.T], axis=0).astype(BF16)
            for i in range(rows):
                for j in range(tt // FOX_KB):
                    vt_ref[i, hh, j] = vt[:, i * tt + j * FOX_KB:i * tt + (j + 1) * FOX_KB]
                for j in range(tt // FOX_QB):
                    qt_ref[i, hh, j] = q_t[:, i * tt + j * FOX_QB:i * tt + (j + 1) * FOX_QB]


def _fox_aux_constants():
    nslot = FOX_HEADS * AUX_PARTS
    pa = np.zeros((AUX_PARTS * LANES, 2 * LANES), np.float32)
    arow = np.zeros((SUBLANES, LANES), np.float32)
    for hh in range(FOX_HEADS):
        for j in range(AUX_PARTS):
            pa[j * LANES + hh, LANES + AUX_PARTS * hh + j] = 1.0
            pa[j * LANES + hh, nslot + AUX_PARTS * hh + j] = -1.0
            arow[0, AUX_PARTS * hh + j] = 1.0
            arow[1 + hh, nslot + AUX_PARTS * hh + j] = 1.0
    return jnp.asarray(pa, BF16), jnp.asarray(arow, F32)


def _fox_proj_call(x, nm, w, qn, kn, fb, tt, rows, attn_layout, layer, depth, kv_bufs):
    bsz, t, _ = x.shape
    tt = _row_tile(t, tt)
    rows = _row_tile(bsz, rows)
    pa, arow = _fox_aux_constants()
    bt = lambda w_: pl.BlockSpec((rows, tt, w_), lambda b, i: (b, i, 0))
    bht = lambda n_, w_: pl.BlockSpec((rows, n_, tt, w_), lambda b, i: (b, 0, i, 0))
    kv_spec = pl.BlockSpec((None, rows, tt, FOX_WIDTH), lambda b, i: (layer, b, i, 0))
    out_specs = [kv_spec, kv_spec]
    out_shape = [jax.ShapeDtypeStruct((depth, bsz, t, FOX_WIDTH), F32),
                 jax.ShapeDtypeStruct((depth, bsz, t, FOX_WIDTH), F32)]
    n_fixed = 8
    if kv_bufs is None:
        kv_bufs = tuple(jnp.zeros(s.shape, s.dtype) for s in out_shape)
    aliased = list(kv_bufs)
    aliases = {n_fixed + j: j for j in range(len(aliased))}
    if attn_layout:
        out_specs += [pl.BlockSpec((rows, SUBLANES, tt), lambda b, i: (b, 0, i))]
        out_shape += [jax.ShapeDtypeStruct((bsz, SUBLANES, t), F32)]
    else:
        out_specs += [bt(LANES)]
        out_shape += [jax.ShapeDtypeStruct((bsz, t, LANES), F32)]
    if attn_layout:
        tiled = lambda n_, r_, w_: pl.BlockSpec((rows, FOX_HEADS, tt // n_, r_, w_), lambda b, i: (b, 0, i, 0, 0))
        out_specs += [tiled(FOX_QB, 2 * LANES, FOX_QB), bht(FOX_HEADS // 2, 2 * LANES),
                      tiled(FOX_KB, FOX_VT_ROWS, FOX_KB)]
        out_shape += [jax.ShapeDtypeStruct((bsz, FOX_HEADS, t // FOX_QB, 2 * LANES, FOX_QB), BF16),
                      jax.ShapeDtypeStruct((bsz, FOX_HEADS // 2, t, 2 * LANES), BF16),
                      jax.ShapeDtypeStruct((bsz, FOX_HEADS, t // FOX_KB, FOX_VT_ROWS, FOX_KB), BF16)]
    else:
        out_specs += [bt(FOX_WIDTH)]
        out_shape += [jax.ShapeDtypeStruct((bsz, t, FOX_WIDTH), BF16)]
    return pl.pallas_call(
        functools.partial(_fox_proj_body, tt=tt, rows=rows, attn_layout=attn_layout, n_alias=len(aliased)),
        grid=(bsz // rows, t // tt),
        in_specs=[bt(D_MODEL), _const_spec((1, D_MODEL)), _const_spec((D_MODEL, FOX_COLS), single=True),
                  _const_spec((1, FOX_WIDTH)), _const_spec((1, FOX_WIDTH)), _const_spec((1, LANES)),
                  _const_spec(pa.shape), _const_spec(arow.shape)]
                 + [pl.BlockSpec(memory_space=pl.ANY)] * len(aliased),
        out_specs=out_specs, out_shape=out_shape,
        input_output_aliases=aliases,
        scratch_shapes=[pltpu.VMEM((rows, 1, LANES), F32)],
        compiler_params=_params(("parallel", "arbitrary"), "fox_proj"),
        name="fox_proj",
    )(x, nm, w, qn, kn, fb, pa, arow, *aliased)


def _fox_prompt_body(qt_ref, k_ref, vt_ref, o_ref, m_s, acc_s, s_s, *, nsub):
    qi = pl.program_id(2)
    qtile = nsub * FOX_QB
    chains = [(si, e) for si in range(nsub) for e in range(2)]
    m_s[...] = jnp.full_like(m_s, -jnp.inf)
    acc_s[...] = jnp.zeros_like(acc_s)

    nch = len(chains)

    def scores(c, k0, nk, masked):
        si, e = chains[c]
        s = _dot(k_ref[pl.ds(k0, nk), :], qt_ref[e, si])
        if masked:
            r, col = _tri_mask(FOX_QB)
            tail = jnp.where(r <= col, s[nk - FOX_QB:], -jnp.inf)
            s = tail if nk == FOX_QB else jnp.concatenate([s[:nk - FOX_QB], tail], axis=0)
        return s

    def absorb(c, s, vt0):
        e = chains[c][1]
        m_old = m_s[c]
        m_new = jnp.maximum(m_old, jnp.max(s, axis=0, keepdims=True))
        alpha = jnp.exp(m_old - m_new)
        p = jnp.exp(s - m_new).astype(BF16)
        vts = [vt_ref[e, vt0 + j] for j in range(s.shape[0] // FOX_KB)]
        vt = vts[0] if len(vts) == 1 else jnp.concatenate(vts, axis=1)
        acc_s[c] = alpha * acc_s[c] + _dot(vt, p)
        m_s[c] = m_new

    def sweep(k0, vt0, nks, masked):
        lead = nch // 2
        for c in range(lead):
            s_s[c, 0:nks[c], :] = scores(c, k0, nks[c], masked)
        for c in range(nch):
            if c + lead < nch:
                s_s[c + lead, 0:nks[c + lead], :] = scores(c + lead, k0, nks[c + lead], masked)
            absorb(c, s_s[c, 0:nks[c], :], vt0)

    def below_diagonal(i, carry):
        sweep(pl.multiple_of(i * FOX_SWEEP, FOX_SWEEP), i * (FOX_SWEEP // FOX_KB), [FOX_SWEEP] * nch, False)
        return carry

    lax.fori_loop(0, qi * (qtile // FOX_SWEEP), below_diagonal, 0)
    sweep(pl.multiple_of(qi * qtile, qtile), qi * (qtile // FOX_KB),
          [(si + 1) * FOX_QB for si, _ in chains], True)

    for si in range(nsub):
        outs = []
        for e in range(2):
            acc = acc_s[chains.index((si, e))]
            outs.append(acc[0:FOX_HEAD_DIM] / acc[FOX_HEAD_DIM:FOX_HEAD_DIM + 1])
        o = jnp.concatenate(outs, axis=0).T
        o_ref[si * FOX_QB:(si + 1) * FOX_QB, :] = o.astype(o_ref.dtype)


def _fox_prompt_call(qt, kp, vt, tq):
    bsz, _, nqb, _, _ = qt.shape
    t = kp.shape[2]
    nsub = _row_tile(nqb, max(FOX_SWEEP // FOX_QB, tq // FOX_QB))
    npair = FOX_HEADS // 2
    nch = 2 * nsub
    return pl.pallas_call(
        functools.partial(_fox_prompt_body, nsub=nsub), grid=(bsz, npair, nqb // nsub),
        in_specs=[pl.BlockSpec((None, 2, nsub, 2 * LANES, FOX_QB), lambda b, p, i: (b, p, i, 0, 0)),
                  pl.BlockSpec((None, None, t, 2 * LANES), lambda b, p, i: (b, p, 0, 0)),
                  pl.BlockSpec((None, 2, t // FOX_KB, FOX_VT_ROWS, FOX_KB), lambda b, p, i: (b, p, 0, 0, 0))],
        out_specs=pl.BlockSpec((None, nsub * FOX_QB, LANES), lambda b, p, i: (b, i, p)),
        out_shape=jax.ShapeDtypeStruct((bsz, t, FOX_WIDTH), BF16),
        scratch_shapes=[pltpu.VMEM((nch, 1, FOX_QB), F32),
                        pltpu.VMEM((nch, FOX_VT_ROWS, FOX_QB), F32),
                        pltpu.VMEM((nch, nsub * FOX_QB, FOX_QB), F32)],
        compiler_params=_params(("parallel", "parallel", "arbitrary"), "fox_attn_prompt"),
        name="fox_attn_prompt",
    )(qt, kp, vt)


def _fox_sample_body(q_ref, kn_ref, vn_ref, lfn_ref, kc_ref, vc_ref, lfp_ref, o_ref, *, tn, past, seg):
    r, c = _tri_mask(seg)
    upper = (r <= c).astype(BF16)
    carry = jnp.zeros((FOX_HEADS, 1), F32)
    cps = []
    for j in range(past // seg):
        cs = carry + _dot_sel_rhs(lfp_ref[:, j * seg:(j + 1) * seg], upper)
        cps.append(cs)
        carry = cs[:, seg - 1:seg]
    cp = jnp.concatenate(cps, axis=1) if len(cps) > 1 else cps[0]
    lfn = lfn_ref[...]
    if tn < LANES:
        lfn = jnp.concatenate([lfn, jnp.zeros((LANES - tn, LANES), F32)], axis=0)
    r, c = _tri_mask(LANES)
    cn = _dot_sel_lhs((r >= c).astype(BF16), lfn)
    cn_t = cn.T

    q = q_ref[...]
    kn = kn_ref[...].astype(BF16)
    vn = vn_ref[...].astype(BF16)
    r, c = _tri_mask(tn)
    outs = []
    for hh in range(FOX_HEADS):
        hs = slice(hh * FOX_HEAD_DIM, (hh + 1) * FOX_HEAD_DIM)
        qh = q[:, hs]
        cq = cn[0:tn, hh:hh + 1]
        s_past = _dot(qh, kc_ref[hh].astype(BF16)) + ((carry[hh:hh + 1, :] + cq) - cp[hh:hh + 1, :])
        s_new = _dot_nt(qh, kn[:, hs]) + (cq - cn_t[hh:hh + 1, 0:tn])
        s_new = jnp.where(r >= c, s_new, -jnp.inf)
        m = jnp.maximum(jnp.max(s_past, axis=-1, keepdims=True), jnp.max(s_new, axis=-1, keepdims=True))
        p_past = jnp.exp(s_past - m)
        p_new = jnp.exp(s_new - m)
        denom = jnp.sum(p_past, axis=-1, keepdims=True) + jnp.sum(p_new, axis=-1, keepdims=True)
        o = _dot_nt(p_past.astype(BF16), vc_ref[hh].astype(BF16)) + _dot(p_new.astype(BF16), vn[:, hs])
        outs.append(o / denom)
    o_ref[...] = jnp.concatenate(outs, axis=1).astype(o_ref.dtype)


def _fox_sample_call(q, kn_all, vn_all, lfn, cache_k, cache_v, lfp_t, layer):
    bsz, tn, _ = q.shape
    past = cache_k.shape[-1]
    seg = _row_tile(past, 512)
    bt = lambda w_: pl.BlockSpec((None, tn, w_), lambda b: (b, 0, 0))
    new = pl.BlockSpec((None, None, tn, FOX_WIDTH), lambda b: (layer, b, 0, 0))
    cache = pl.BlockSpec((None, None, FOX_HEADS, FOX_HEAD_DIM, past), lambda b: (layer, b, 0, 0, 0))
    return pl.pallas_call(
        functools.partial(_fox_sample_body, tn=tn, past=past, seg=seg), grid=(bsz,),
        in_specs=[bt(FOX_WIDTH), new, new, bt(LANES), cache, cache,
                  pl.BlockSpec((None, None, FOX_HEADS, past), lambda b: (layer, b, 0, 0))],
        out_specs=bt(FOX_WIDTH),
        out_shape=jax.ShapeDtypeStruct((bsz, tn, FOX_WIDTH), BF16),
        compiler_params=_params(("parallel",), "fox_attn_sample"),
        name="fox_attn_sample",
    )(q, kn_all, vn_all, lfn, cache_k, cache_v, lfp_t)


def _pad_cols(a, width):
    return jnp.pad(a, ((0, 0), (0, width - a.shape[1])))


def _layer_params(l, norm_ffn1, w1_gate, w1_up, w1_down, norm_mix, w_in, ssd_conv_w, ssd_conv_b,
                  ssd_dt_bias, ssd_a_log, ssd_d, ssd_norm, gla_w_gate, gla_b_gate, gla_norm,
                  fox_q_norm, fox_k_norm, fox_f_bias, w_out, norm_ffn2, w2_gate, w2_up, w2_down):
    wi = w_in[l]
    row = lambda a: a.reshape(1, -1).astype(F32)
    hk = GLA_HEADS * GLA_DK
    p = dict(
        ffn1=(row(norm_ffn1[l]), w1_gate[l].astype(BF16), w1_up[l].astype(BF16), w1_down[l].astype(BF16)),
        ffn2=(row(norm_ffn2[l]), w2_gate[l].astype(BF16), w2_up[l].astype(BF16), w2_down[l].astype(BF16)),
        norm_mix=row(norm_mix[l]),
        w_out=w_out[l].astype(BF16),
        w_ssd=jnp.concatenate([wi[:, _OFF_Z:_OFF_DT], _pad_cols(wi[:, _OFF_DT:_OFF_GQ], LANES)], axis=1).astype(BF16),
        conv_w=ssd_conv_w[l].astype(F32),
        conv_b=row(ssd_conv_b[l]),
        dt_bias=_pad_cols(row(ssd_dt_bias[l]), LANES),
        a_log=_pad_cols(row(ssd_a_log[l]), LANES),
        d_exp=row(jnp.repeat(ssd_d[l], SSD_HEAD_DIM)),
        ssd_norm=row(ssd_norm[l]),
        w_gla=jnp.concatenate([wi[:, _OFF_GQ:_OFF_GR], _pad_cols(wi[:, _OFF_GR:_OFF_FQ], LANES)], axis=1).astype(BF16),
        w_gate=jnp.pad(gla_w_gate[l], ((0, LANES - GLA_RANK), (0, 0))).astype(BF16),
        b_gate=row(gla_b_gate[l]),
        gla_norm=row(jnp.tile(gla_norm[l], GLA_HEADS)),
        w_fox=jnp.concatenate([wi[:, _OFF_FQ:_OFF_FF], _pad_cols(wi[:, _OFF_FF:], LANES)], axis=1).astype(BF16),
        q_norm=row(jnp.tile(fox_q_norm[l], FOX_HEADS)),
        k_norm=row(jnp.tile(fox_k_norm[l], FOX_HEADS)),
        f_bias=_pad_cols(row(fox_f_bias[l]), LANES),
    )
    return p


def _gla_state_in(s):
    bsz = s.shape[0]
    eye = jnp.eye(GLA_HEADS, dtype=s.dtype)
    full = jnp.einsum('bhkv,hg->bhvgk', s, eye)
    return full.reshape(bsz, GLA_HEADS * GLA_DV, GLA_HEADS * GLA_DK)


def _gla_state_out(st):
    bsz = st.shape[0]
    full = st.reshape(bsz, GLA_HEADS, GLA_DV, GLA_HEADS, GLA_DK)
    idx = jnp.arange(GLA_HEADS)
    diag = full[:, idx, :, idx, :]
    return jnp.transpose(diag, (1, 0, 3, 2))


def _mix(x, p, conv_prev, ssd_h0, gla_s0, fox_cache, layer, depth, kv_bufs, tiles):
    bsz, t, _ = x.shape
    cprev = jnp.pad(conv_prev, ((0, 0), (SUBLANES - (SSD_CONV - 1), 0), (0, 0)))
    y_ssd, cnew, hnew = _ssd_call(x, p['norm_mix'], p['w_ssd'], p['conv_w'], p['conv_b'], p['dt_bias'],
                                  p['a_log'], p['d_exp'], p['ssd_norm'], cprev,
                                  ssd_h0.reshape(bsz, SSD_WIDTH, SSD_STATE), tiles['ssd'], tiles['ssd_rows'])
    o_gla, snew = _gla_call(x, p['norm_mix'], p['w_gla'], p['w_gate'], p['b_gate'], p['gla_norm'],
                            _gla_state_in(gla_s0), tiles['gla'], tiles['gla_rows'])
    proj = functools.partial(_fox_proj_call, x, p['norm_mix'], p['w_fox'], p['q_norm'], p['k_norm'], p['f_bias'],
                             tiles['fox_proj'], tiles['fox_rows'], layer=layer, depth=depth, kv_bufs=kv_bufs)
    if fox_cache is None:
        k_all, v_all, lf_t, qt, kp, vt = proj(attn_layout=True)
        o_fox = _fox_prompt_call(qt, kp, vt, tiles['fox_q'])
        lf_state = jnp.swapaxes(lf_t[:, 0:FOX_HEADS, :], 1, 2)
    else:
        k_all, v_all, lf, qs = proj(attn_layout=False)
        cache_k, cache_v, lfp_t = fox_cache
        o_fox = _fox_sample_call(qs, k_all, v_all, lf, cache_k, cache_v, lfp_t, layer)
        lf_state = lf[:, :, 0:FOX_HEADS]
    state = (cnew[:, SUBLANES - (SSD_CONV - 1):, :],
             hnew.reshape(bsz, SSD_HEADS, SSD_HEAD_DIM, SSD_STATE),
             _gla_state_out(snew),
             lf_state)
    return (y_ssd, o_gla, o_fox), state, (k_all, v_all)


def _trunk_layer(x, p, conv_prev, ssd_h0, gla_s0, fox_cache, layer, depth, kv_bufs, tiles):
    bsz, t, d = x.shape
    x1 = _ffn_call(x.reshape(bsz * t, d), *p['ffn1'], tiles['ffn']).reshape(bsz, t, d)
    (y_ssd, o_gla, o_fox), state, kv_bufs = _mix(x1, p, conv_prev, ssd_h0, gla_s0, fox_cache, layer, depth,
                                                 kv_bufs, tiles)
    flat = lambda a: a.reshape(bsz * t, a.shape[-1])
    x3 = _out_ffn_call(flat(x1), flat(y_ssd), flat(o_gla), flat(o_fox), p['w_out'], *p['ffn2'], tiles['ffn'])
    return x3.reshape(bsz, t, d), state, kv_bufs


PROMPT_TILES = dict(ffn=512, ssd=256, ssd_rows=4, gla=256, gla_rows=4, fox_proj=512, fox_rows=2,
                    fox_q=1024)
SAMPLE_TILES = dict(ffn=512, ssd=64, ssd_rows=4, gla=64, gla_rows=4, fox_proj=64, fox_rows=8)


def kernel(x_prompt, x_sample, state_ssd_conv, state_ssd, state_gla, cache_fox_k, cache_fox_v, cache_fox_logf, norm_ffn1, w1_gate, w1_up, w1_down, norm_mix, w_in, ssd_conv_w, ssd_conv_b, ssd_dt_bias, ssd_a_log, ssd_d, ssd_norm, gla_w_gate, gla_b_gate, gla_norm, fox_q_norm, fox_k_norm, fox_f_bias, w_out, norm_ffn2, w2_gate, w2_up, w2_down):
    depth = w_in.shape[0]
    bp = x_prompt.shape[0]
    weights = (norm_ffn1, w1_gate, w1_up, w1_down, norm_mix, w_in, ssd_conv_w, ssd_conv_b, ssd_dt_bias,
               ssd_a_log, ssd_d, ssd_norm, gla_w_gate, gla_b_gate, gla_norm, fox_q_norm, fox_k_norm,
               fox_f_bias, w_out, norm_ffn2, w2_gate, w2_up, w2_down)
    cache_k = jnp.transpose(cache_fox_k, (0, 1, 3, 4, 2))
    cache_v = jnp.transpose(cache_fox_v, (0, 1, 3, 4, 2))
    lfp_t = jnp.swapaxes(cache_fox_logf, 2, 3)
    zeros_conv = jnp.zeros((bp, SSD_CONV - 1, SSD_CONV_CH), F32)
    zeros_ssd = jnp.zeros((bp, SSD_HEADS, SSD_HEAD_DIM, SSD_STATE), F32)
    zeros_gla = jnp.zeros((bp, GLA_HEADS, GLA_DK, GLA_DV), F32)

    xp, xs = x_prompt, x_sample
    p_new = [[] for _ in range(4)]
    s_new = [[] for _ in range(4)]
    kv_p = kv_s = None
    for l in range(depth):
        p = _layer_params(l, *weights)
        xp, st_p, kv_p = _trunk_layer(xp, p, zeros_conv, zeros_ssd, zeros_gla, None, l, depth, kv_p, PROMPT_TILES)
        xs, st_s, kv_s = _trunk_layer(xs, p, state_ssd_conv[l], state_ssd[l], state_gla[l],
                                      (cache_k, cache_v, lfp_t), l, depth, kv_s, SAMPLE_TILES)
        for i in range(4):
            p_new[i].append(st_p[i])
            s_new[i].append(st_s[i])

    def leaves(small, kv):
        conv, ssd, gla, logf = [jnp.stack(a) for a in small]
        heads = lambda a: a.reshape(a.shape[:3] + (FOX_HEADS, FOX_HEAD_DIM))
        return conv, ssd, gla, heads(kv[0]), heads(kv[1]), logf

    return (xp, xs, *leaves(p_new, kv_p), *leaves(s_new, kv_s))
```

```python
import functools

import numpy as np
import jax
import jax.numpy as jnp
from jax import lax
from jax.experimental import pallas as pl
from jax.experimental.pallas import tpu as pltpu

F32 = jnp.float32
BF16 = jnp.bfloat16

EPS = 1e-6
D_MODEL = 1024
D_FF = 2816
SSD_HEADS = 8
SSD_HEAD_DIM = 64
SSD_WIDTH = 512
SSD_GROUPS = 2
SSD_STATE = 128
SSD_CONV = 4
SSD_CONV_CH = 1024
GLA_HEADS = 4
GLA_DK = 32
GLA_DV = 64
GLA_WIDTH = 256
GLA_RANK = 16
GLA_TAU = 16.0
GLA_CHUNK = 64
FOX_HEADS = 4
FOX_HEAD_DIM = 64
FOX_WIDTH = 256
LANES = 128
SUBLANES = 8
MIB = 1024 * 1024

_OFF_Z, _OFF_XBC, _OFF_DT = 0, 512, 1536
_OFF_GQ, _OFF_GR = 1544, 2312
_OFF_FQ, _OFF_FF = 2328, 3096
SSD_COLS = SSD_WIDTH + SSD_CONV_CH + LANES
GLA_COLS = 2 * GLA_HEADS * GLA_DK + 2 * GLA_WIDTH + LANES
FOX_COLS = 3 * FOX_WIDTH + LANES
AUX_PARTS = 3
FOX_QB = 256
FOX_KB = 256
FOX_SWEEP = 1024
FOX_VT_ROWS = FOX_HEAD_DIM + 16


def _dot(a, b):
    return jnp.dot(a, b, preferred_element_type=F32)


def _dot_nt(a, b):
    return lax.dot_general(a, b, (((1,), (1,)), ((), ())), preferred_element_type=F32)


def _dot_tn(a, b):
    return lax.dot_general(a, b, (((0,), (0,)), ((), ())), preferred_element_type=F32)


def _split_bf16(x, parts):
    out = []
    r = x
    for i in range(parts):
        p = r.astype(BF16)
        out.append(p)
        if i + 1 < parts:
            r = r - p.astype(F32)
    return out


def _dot_sel_lhs(sel, x, parts=3):
    n = x.shape[1]
    t = _dot(sel, jnp.concatenate(_split_bf16(x, parts), axis=1))
    acc = t[:, 0:n]
    for i in range(1, parts):
        acc = acc + t[:, i * n:(i + 1) * n]
    return acc


def _dot_sel_rhs(x, sel, parts=3):
    m = x.shape[0]
    pieces = _split_bf16(x, parts)
    if m % (2 * SUBLANES):
        acc = _dot(pieces[0], sel)
        for p in pieces[1:]:
            acc = acc + _dot(p, sel)
        return acc
    t = _dot(jnp.concatenate(pieces, axis=0), sel)
    acc = t[0:m]
    for i in range(1, parts):
        acc = acc + t[i * m:(i + 1) * m]
    return acc


def _rms(x, w):
    ms = jnp.mean(x * x, axis=-1, keepdims=True)
    return x * lax.rsqrt(ms + EPS) * w


def _silu(x):
    return x * jax.nn.sigmoid(x)


def _softplus(x):
    return jnp.maximum(x, 0.0) + jnp.log1p(jnp.exp(-jnp.abs(x)))


def _log_sigmoid(x):
    return -_softplus(-x)


def _tri_mask(n, m=None):
    m = n if m is None else m
    r = lax.broadcasted_iota(jnp.int32, (n, m), 0)
    c = lax.broadcasted_iota(jnp.int32, (n, m), 1)
    return r, c


def _lane_pair_select(a_even, a_odd):
    lane = lax.broadcasted_iota(jnp.int32, a_even.shape, 1)
    return jnp.where(lane < 64, a_even, a_odd)


def _expand_heads(v, heads, rows):
    pieces = []
    for i in range(0, len(heads), 2):
        a = jnp.broadcast_to(v[:, heads[i]:heads[i] + 1], (rows, LANES))
        b = jnp.broadcast_to(v[:, heads[i + 1]:heads[i + 1] + 1], (rows, LANES))
        pieces.append(_lane_pair_select(a, b))
    return pieces[0] if len(pieces) == 1 else jnp.concatenate(pieces, axis=1)


def _group_mean_matrix(width, group):
    r, c = _tri_mask(width)
    return jnp.where((r // group) == (c // group), 1.0 / group, 0.0).astype(BF16)


def _project_rows(h, w_ref, tt, rows):
    if rows == 1 or tt < 2 * LANES:
        return _dot(h, w_ref[...])
    return jnp.concatenate([_dot(h[i * tt:(i + 1) * tt], w_ref[...]) for i in range(rows)], axis=0)


VMEM_LIMIT_MIB = dict(ffn=56, out_ffn=56, ssd=48, gla=32, fox_proj=40, fox_attn_prompt=40, fox_attn_sample=48)


def _params(sem, call):
    return pltpu.CompilerParams(dimension_semantics=sem, vmem_limit_bytes=VMEM_LIMIT_MIB[call] * MIB)


def _const_spec(shape, single=False):
    nd = len(shape)
    if single:
        return pl.BlockSpec(shape, lambda *_: (0,) * nd, pipeline_mode=pl.Buffered(1))
    return pl.BlockSpec(shape, lambda *_: (0,) * nd)


def _swiglu_half(x, g_ref, wg_ref, wu_ref, wd_ref):
    h = _rms(x, g_ref[...]).astype(BF16)
    gate = _dot(h, wg_ref[...])
    up = _dot(h, wu_ref[...])
    a = (_silu(gate) * up).astype(BF16)
    return x + 0.5 * _dot(a, wd_ref[...])


def _ffn_body(x_ref, g_ref, wg_ref, wu_ref, wd_ref, o_ref):
    o_ref[...] = _swiglu_half(x_ref[...], g_ref, wg_ref, wu_ref, wd_ref)


def _out_ffn_body(x_ref, ys_ref, og_ref, of_ref, wo_ref, g_ref, wg_ref, wu_ref, wd_ref, o_ref):
    x = x_ref[...]
    x = x + (_dot(ys_ref[...], wo_ref[0:SSD_WIDTH, :])
             + _dot(og_ref[...], wo_ref[SSD_WIDTH:SSD_WIDTH + GLA_WIDTH, :])
             + _dot(of_ref[...], wo_ref[SSD_WIDTH + GLA_WIDTH:, :]))
    o_ref[...] = _swiglu_half(x, g_ref, wg_ref, wu_ref, wd_ref)


def _ffn_weight_specs():
    return [_const_spec((1, D_MODEL)),
            _const_spec((D_MODEL, D_FF), single=True),
            _const_spec((D_MODEL, D_FF), single=True),
            _const_spec((D_FF, D_MODEL), single=True)]


def _row_tile(n, want):
    t = min(want, n)
    while n % t:
        t //= 2
    return t


def _ffn_call(x, g, wg, wu, wd, tm):
    n = x.shape[0]
    tm = _row_tile(n, tm)
    row = pl.BlockSpec((tm, D_MODEL), lambda i: (i, 0))
    return pl.pallas_call(
        _ffn_body, grid=(n // tm,),
        in_specs=[row] + _ffn_weight_specs(),
        out_specs=row,
        out_shape=jax.ShapeDtypeStruct((n, D_MODEL), F32),
        compiler_params=_params(("parallel",), "ffn"),
        name="ffn",
    )(x, g, wg, wu, wd)


def _out_ffn_call(x, ys, og, of, wo, g, wg, wu, wd, tm):
    n = x.shape[0]
    tm = _row_tile(n, tm)
    row = lambda w: pl.BlockSpec((tm, w), lambda i: (i, 0))
    return pl.pallas_call(
        _out_ffn_body, grid=(n // tm,),
        in_specs=[row(D_MODEL), row(SSD_WIDTH), row(GLA_WIDTH), row(FOX_WIDTH),
                  _const_spec((D_MODEL, D_MODEL), single=True)] + _ffn_weight_specs(),
        out_specs=row(D_MODEL),
        out_shape=jax.ShapeDtypeStruct((n, D_MODEL), F32),
        compiler_params=_params(("parallel",), "out_ffn"),
        name="out_ffn",
    )(x, ys, og, of, wo, g, wg, wu, wd)


def _ssd_body(x_ref, nm_ref, w_ref, cw_ref, cb_ref, dtb_ref, alog_ref, dexp_ref, nw_ref,
              cprev_ref, h0_ref, y_ref, cnew_ref, hnew_ref, xbuf, hst, *, tt, rows):
    t = pl.program_id(1)

    @pl.when(t == 0)
    def _():
        xbuf[...] = cprev_ref[...]
        hst[...] = h0_ref[...]

    x = x_ref[...].reshape(rows * tt, D_MODEL)
    h = _rms(x, nm_ref[...]).astype(BF16)
    u = _project_rows(h, w_ref, tt, rows)
    z = u[:, 0:SSD_WIDTH]
    xbc = u[:, SSD_WIDTH:SSD_WIDTH + SSD_CONV_CH]
    dt_raw = u[:, SSD_WIDTH + SSD_CONV_CH:]

    convs = []
    row8 = lax.broadcasted_iota(jnp.int32, (SUBLANES, SSD_CONV_CH), 0)
    for i in range(rows):
        xi = xbc[i * tt:(i + 1) * tt]
        prev = xbuf[i]
        conv = None
        for j in range(SSD_CONV - 1, 0, -1):
            rolled = pltpu.roll(xi, j, 0)
            head = jnp.where(row8 < j, pltpu.roll(prev, j, 0), rolled[0:SUBLANES])
            term = jnp.concatenate([head, rolled[SUBLANES:]], axis=0) * cw_ref[SSD_CONV - 1 - j:SSD_CONV - j, :]
            conv = term if conv is None else conv + term
        convs.append(conv + xi * cw_ref[SSD_CONV - 1:SSD_CONV, :])
        xbuf[i] = xi[tt - SUBLANES:, :]
    xa = _silu(jnp.concatenate(convs, axis=0) + cb_ref[...])

    xs = xa[:, 0:SSD_WIDTH]
    bm = xa[:, SSD_WIDTH:SSD_WIDTH + SSD_GROUPS * SSD_STATE].astype(BF16)
    cm = xa[:, SSD_WIDTH + SSD_GROUPS * SSD_STATE:].astype(BF16)
    dt = _softplus(dt_raw + dtb_ref[...])
    a = dt * (-jnp.exp(alog_ref[...]))

    r, c = _tri_mask(tt)
    causal = r >= c
    tri = causal.astype(BF16)
    acum = jnp.concatenate(
        [_dot_sel_lhs(tri, a[i * tt:(i + 1) * tt]) for i in range(rows)], axis=0)
    shift_t = (acum - jnp.log(dt)).T
    alast = jnp.concatenate(
        [jnp.broadcast_to(acum[(i + 1) * tt - 1:(i + 1) * tt, :], (tt, LANES)) for i in range(rows)], axis=0)
    ea = jnp.exp(acum)
    wend = jnp.exp(alast - acum) * dt

    heads_per_group = SSD_HEADS // SSD_GROUPS
    gw = heads_per_group * SSD_HEAD_DIM
    heads_of = [list(range(g * heads_per_group, (g + 1) * heads_per_group)) for g in range(SSD_GROUPS)]
    nrow = rows * tt
    xb = xs.astype(BF16)
    xw = jnp.concatenate([xs[:, g * gw:(g + 1) * gw] * _expand_heads(wend, heads_of[g], nrow)
                          for g in range(SSD_GROUPS)], axis=1).astype(BF16)
    ea_x = jnp.concatenate([_expand_heads(ea, heads_of[g], nrow) for g in range(SSD_GROUPS)], axis=1)

    units = [(i, g) for g in range(SSD_GROUPS) for i in range(rows)]
    rs = {i: slice(i * tt, (i + 1) * tt) for i in range(rows)}
    gs = {g: slice(g * SSD_STATE, (g + 1) * SSD_STATE) for g in range(SSD_GROUPS)}
    cbs = {(i, g): _dot_nt(cm[rs[i], gs[g]], bm[rs[i], gs[g]]) for i, g in units}
    upds = {(i, g): _dot_tn(xw[rs[i], g * gw:(g + 1) * gw], bm[rs[i], gs[g]]) for i, g in units}
    y_unit = {}
    for i, g in units:
        heads = heads_of[g]
        intra = []
        for pi in range(heads_per_group // 2):
            ys = []
            for e in range(2):
                hh = heads[2 * pi + e]
                seg = acum[rs[i], hh:hh + 1] - shift_t[hh:hh + 1, rs[i]]
                m = (cbs[(i, g)] * jnp.where(causal, jnp.exp(seg), 0.0)).astype(BF16)
                col = g * gw + pi * LANES
                ys.append(_dot(m, xb[rs[i], col:col + LANES]))
            intra.append(_lane_pair_select(ys[0], ys[1]))
        hg = hst[i, g * gw:(g + 1) * gw, :]
        y_inter = _dot_nt(cm[rs[i], gs[g]], hg.astype(BF16)) * ea_x[rs[i], g * gw:(g + 1) * gw]
        y_unit[(i, g)] = jnp.concatenate(intra, axis=1) + y_inter
        cd = jnp.concatenate(
            [jnp.broadcast_to(jnp.exp(acum[(i + 1) * tt - 1:(i + 1) * tt, hh:hh + 1]), (SSD_HEAD_DIM, SSD_STATE))
             for hh in heads], axis=0)
        hst[i, g * gw:(g + 1) * gw, :] = hg * cd + upds[(i, g)]

    y = jnp.concatenate(
        [jnp.concatenate([y_unit[(i, g)] for g in range(SSD_GROUPS)], axis=1) for i in range(rows)], axis=0)
    y = (y + dexp_ref[...] * xs) * _silu(z)
    nw = nw_ref[...]
    outs = [_rms(y[:, g * gw:(g + 1) * gw], nw[:, g * gw:(g + 1) * gw]) for g in range(SSD_GROUPS)]
    y_ref[...] = jnp.concatenate(outs, axis=1).astype(y_ref.dtype).reshape(rows, tt, SSD_WIDTH)

    @pl.when(t == pl.num_programs(1) - 1)
    def _():
        cnew_ref[...] = xbuf[...]
        hnew_ref[...] = hst[...]


def _ssd_call(x, nm, w, cw, cb, dtb, alog, dexp, nw, cprev, h0, tt, rows):
    bsz, t, _ = x.shape
    tt = _row_tile(t, tt)
    rows = _row_tile(bsz, rows)
    bt = lambda w_: pl.BlockSpec((rows, tt, w_), lambda b, i: (b, i, 0))
    per_b = lambda r_, w_: pl.BlockSpec((rows, r_, w_), lambda b, i: (b, 0, 0))
    return pl.pallas_call(
        functools.partial(_ssd_body, tt=tt, rows=rows), grid=(bsz // rows, t // tt),
        in_specs=[bt(D_MODEL), _const_spec((1, D_MODEL)), _const_spec((D_MODEL, SSD_COLS), single=True),
                  _const_spec((SSD_CONV, SSD_CONV_CH)), _const_spec((1, SSD_CONV_CH)),
                  _const_spec((1, LANES)), _const_spec((1, LANES)),
                  _const_spec((1, SSD_WIDTH)), _const_spec((1, SSD_WIDTH)),
                  per_b(SUBLANES, SSD_CONV_CH), per_b(SSD_WIDTH, SSD_STATE)],
        out_specs=[bt(SSD_WIDTH), per_b(SUBLANES, SSD_CONV_CH), per_b(SSD_WIDTH, SSD_STATE)],
        out_shape=[jax.ShapeDtypeStruct((bsz, t, SSD_WIDTH), BF16),
                   jax.ShapeDtypeStruct((bsz, SUBLANES, SSD_CONV_CH), F32),
                   jax.ShapeDtypeStruct((bsz, SSD_WIDTH, SSD_STATE), F32)],
        scratch_shapes=[pltpu.VMEM((rows, SUBLANES, SSD_CONV_CH), F32),
                        pltpu.VMEM((rows, SSD_WIDTH, SSD_STATE), F32)],
        compiler_params=_params(("parallel", "arbitrary"), "ssd"),
        name="ssd",
    )(x, nm, w, cw, cb, dtb, alog, dexp, nw, cprev, h0)


def _gla_body(x_ref, nm_ref, w_ref, wgate_ref, bgate_ref, gn_ref, s0_ref, o_ref, snew_ref, st, *, tt, rows):
    t = pl.program_id(1)

    @pl.when(t == 0)
    def _():
        st[...] = s0_ref[...]

    hk = GLA_HEADS * GLA_DK
    nchunk = tt // GLA_CHUNK
    x = x_ref[...].reshape(rows * tt, D_MODEL)
    h = _rms(x, nm_ref[...]).astype(BF16)
    u = _project_rows(h, w_ref, tt, rows)
    q = u[:, 0:hk] * (GLA_DK ** -0.5)
    k = u[:, hk:2 * hk]
    v = u[:, 2 * hk:2 * hk + GLA_WIDTH].astype(BF16)
    gg = u[:, 2 * hk + GLA_WIDTH:2 * hk + 2 * GLA_WIDTH]
    gr = u[:, 2 * hk + 2 * GLA_WIDTH:].astype(BF16)
    la = _log_sigmoid(_dot(gr, wgate_ref[...]) + bgate_ref[...]) / GLA_TAU

    r, c = _tri_mask(tt)
    sel = (((r // GLA_CHUNK) == (c // GLA_CHUNK)) & (r >= c)).astype(BF16)
    bcum = jnp.concatenate(
        [_dot_sel_lhs(sel, la[i * tt:(i + 1) * tt]) for i in range(rows)], axis=0)
    blast = jnp.concatenate(
        [jnp.broadcast_to(bcum[(ci + 1) * GLA_CHUNK - 1:(ci + 1) * GLA_CHUNK, :], (GLA_CHUNK, hk))
         for ci in range(rows * nchunk)], axis=0)
    qd = (q * jnp.exp(bcum)).astype(BF16)
    kd = (k * jnp.exp(-bcum)).astype(BF16)
    kend = (k * jnp.exp(blast - bcum)).astype(BF16)
    eblast = jnp.exp(blast)

    lane_k = lax.broadcasted_iota(jnp.int32, (GLA_CHUNK, hk), 1) // GLA_DK
    lane_v = lax.broadcasted_iota(jnp.int32, (GLA_CHUNK, GLA_WIDTH), 1) // GLA_DV
    ar, ac = _tri_mask(GLA_HEADS * GLA_CHUNK, GLA_CHUNK)
    att_causal = (ar % GLA_CHUNK) >= ac
    sr, sc = _tri_mask(GLA_WIDTH, hk)
    diag = (sr // GLA_DV) == (sc // GLA_DK)

    units = [(i, ci) for ci in range(nchunk) for i in range(rows)]
    sl = {(i, ci): slice(i * tt + ci * GLA_CHUNK, i * tt + (ci + 1) * GLA_CHUNK) for i, ci in units}
    att, upd, o_intra = {}, {}, {}
    for un in units:
        qd_c = qd[sl[un]]
        lhs = jnp.concatenate([jnp.where(lane_k == hh, qd_c, jnp.zeros_like(qd_c))
                               for hh in range(GLA_HEADS)], axis=0)
        att[un] = jnp.where(att_causal, _dot_nt(lhs, kd[sl[un]]), 0.0).astype(BF16)
        upd[un] = jnp.where(diag, _dot_tn(v[sl[un]], kend[sl[un]]), 0.0)
    for un in units:
        res = _dot(att[un], v[sl[un]])
        acc = jnp.zeros((GLA_CHUNK, GLA_WIDTH), F32)
        for hh in range(GLA_HEADS):
            acc = jnp.where(lane_v == hh, res[hh * GLA_CHUNK:(hh + 1) * GLA_CHUNK], acc)
        o_intra[un] = acc
    outs = {}
    for un in units:
        i = un[0]
        s_prev = st[i]
        outs[un] = o_intra[un] + _dot_nt(qd[sl[un]], s_prev.astype(BF16))
        st[i] = s_prev * eblast[sl[un].start:sl[un].start + 1, :] + upd[un]

    o = jnp.concatenate([outs[(i, ci)] for i in range(rows) for ci in range(nchunk)], axis=0)
    ms = _dot_sel_rhs(o * o, _group_mean_matrix(GLA_WIDTH, GLA_DV), parts=2)
    o = o * lax.rsqrt(ms + EPS) * gn_ref[...]
    o_ref[...] = (o * _silu(gg)).astype(o_ref.dtype).reshape(rows, tt, GLA_WIDTH)

    @pl.when(t == pl.num_programs(1) - 1)
    def _():
        snew_ref[...] = st[...]


def _gla_call(x, nm, w, wgate, bgate, gn, s0, tt, rows):
    bsz, t, _ = x.shape
    tt = _row_tile(t, tt)
    rows = _row_tile(bsz, rows)
    hk = GLA_HEADS * GLA_DK
    bt = lambda w_: pl.BlockSpec((rows, tt, w_), lambda b, i: (b, i, 0))
    per_b = pl.BlockSpec((rows, GLA_WIDTH, hk), lambda b, i: (b, 0, 0))
    return pl.pallas_call(
        functools.partial(_gla_body, tt=tt, rows=rows), grid=(bsz // rows, t // tt),
        in_specs=[bt(D_MODEL), _const_spec((1, D_MODEL)), _const_spec((D_MODEL, GLA_COLS), single=True),
                  _const_spec((LANES, hk)), _const_spec((1, hk)), _const_spec((1, GLA_WIDTH)), per_b],
        out_specs=[bt(GLA_WIDTH), per_b],
        out_shape=[jax.ShapeDtypeStruct((bsz, t, GLA_WIDTH), BF16),
                   jax.ShapeDtypeStruct((bsz, GLA_WIDTH, hk), F32)],
        scratch_shapes=[pltpu.VMEM((rows, GLA_WIDTH, hk), F32)],
        compiler_params=_params(("parallel", "arbitrary"), "gla"),
        name="gla",
    )(x, nm, w, wgate, bgate, gn, s0)


def _fox_proj_body(x_ref, nm_ref, w_ref, qn_ref, kn_ref, fb_ref, pa_ref, arow_ref, *rest,
                   tt, rows, attn_layout, n_alias):
    k_ref, v_ref, lf_ref, *rest = rest[n_alias:]
    if attn_layout:
        qt_ref, kp_ref, vt_ref, carry = rest
    else:
        qs_ref, carry = rest
    t = pl.program_id(1)

    @pl.when(t == 0)
    def _():
        carry[...] = jnp.zeros_like(carry)

    nrow = rows * tt
    h = _rms(x_ref[...].reshape(nrow, D_MODEL), nm_ref[...]).astype(BF16)
    u = _project_rows(h, w_ref, tt, rows)
    fq = u[:, 0:FOX_WIDTH]
    fk = u[:, FOX_WIDTH:2 * FOX_WIDTH]
    fv = u[:, 2 * FOX_WIDTH:3 * FOX_WIDTH]
    ff = u[:, 3 * FOX_WIDTH:]

    gmat = _group_mean_matrix(FOX_WIDTH, FOX_HEAD_DIM)
    qn = fq * lax.rsqrt(_dot_sel_rhs(fq * fq, gmat, parts=2) + EPS) * qn_ref[...]
    kn = fk * lax.rsqrt(_dot_sel_rhs(fk * fk, gmat, parts=2) + EPS) * kn_ref[...]
    lf = _log_sigmoid(ff + fb_ref[...])
    k_ref[...] = kn.reshape(rows, tt, FOX_WIDTH)
    v_ref[...] = fv.reshape(rows, tt, FOX_WIDTH)
    qs = (qn * (FOX_HEAD_DIM ** -0.5)).astype(BF16)

    if not attn_layout:
        lf_ref[...] = lf.reshape(rows, tt, LANES)
        qs_ref[...] = qs.reshape(rows, tt, FOX_WIDTH)
        return

    r, c = _tri_mask(tt)
    tri = (r >= c).astype(BF16)
    cums = []
    for i in range(rows):
        lf_i = lf[i * tt:(i + 1) * tt]
        lf_ref[i] = lf_i.T[0:SUBLANES]
        cum_i = carry[i] + _dot_sel_lhs(tri, lf_i)
        carry[i] = cum_i[tt - 1:tt, :]
        cums.append(cum_i)
    cum = jnp.concatenate(cums, axis=0) if rows > 1 else cums[0]
    c3 = jnp.concatenate(_split_bf16(cum, AUX_PARTS), axis=1)
    aux = _dot(c3, pa_ref[...])
    aux_k = (aux[:, 0:LANES] + arow_ref[0:1, :]).astype(BF16)
    aux_q_all = aux[:, LANES:]
    knb = kn.astype(BF16)
    fv_t = fv.T
    ones = jnp.ones((FOX_VT_ROWS - FOX_HEAD_DIM, nrow), F32)
    lane = lax.broadcasted_iota(jnp.int32, (nrow, LANES), 1)
    for p in range(FOX_HEADS // 2):
        sl = slice(p * LANES, (p + 1) * LANES)
        for i in range(rows):
            kp_ref[i, p, :, 0:LANES] = knb[i * tt:(i + 1) * tt, sl]
            kp_ref[i, p, :, LANES:] = aux_k[i * tt:(i + 1) * tt]
        for e in range(2):
            hh = 2 * p + e
            vt = jnp.concatenate([fv_t[hh * FOX_HEAD_DIM:(hh + 1) * FOX_HEAD_DIM, :], ones], axis=0).astype(BF16)
            qmask = jnp.where((lane // FOX_HEAD_DIM) == e, qs[:, sl], jnp.zeros_like(qs[:, sl])).astype(F32)
            aux_q = jnp.where((lane // AUX_PARTS) == hh, aux_q_all, 0.0) + arow_ref[1 + hh:2 + hh, :]
            q_t = jnp.concatenate([qmask.T, aux_q.T], axis=0).astype(BF16)
            for i in range(rows):
                for j in range(tt // FOX_KB):
                    vt_ref[i, hh, j] = vt[:, i * tt + j * FOX_KB:i * tt + (j + 1) * FOX_KB]
                for j in range(tt // FOX_QB):
                    qt_ref[i, hh, j] = q_t[:, i * tt + j * FOX_QB:i * tt + (j + 1) * FOX_QB]


def _fox_aux_constants():
    nslot = FOX_HEADS * AUX_PARTS
    pa = np.zeros((AUX_PARTS * LANES, 2 * LANES), np.float32)
    arow = np.zeros((SUBLANES, LANES), np.float32)
    for hh in range(FOX_HEADS):
        for j in range(AUX_PARTS):
            pa[j * LANES + hh, LANES + AUX_PARTS * hh + j] = 1.0
            pa[j * LANES + hh, nslot + AUX_PARTS * hh + j] = -1.0
            arow[0, AUX_PARTS * hh + j] = 1.0
            arow[1 + hh, nslot + AUX_PARTS * hh + j] = 1.0
    return jnp.asarray(pa, BF16), jnp.asarray(arow, F32)


def _fox_proj_call(x, nm, w, qn, kn, fb, tt, rows, attn_layout, layer, depth, kv_bufs):
    bsz, t, _ = x.shape
    tt = _row_tile(t, tt)
    rows = _row_tile(bsz, rows)
    pa, arow = _fox_aux_constants()
    bt = lambda w_: pl.BlockSpec((rows, tt, w_), lambda b, i: (b, i, 0))
    bht = lambda n_, w_: pl.BlockSpec((rows, n_, tt, w_), lambda b, i: (b, 0, i, 0))
    kv_spec = pl.BlockSpec((None, rows, tt, FOX_WIDTH), lambda b, i: (layer, b, i, 0))
    out_specs = [kv_spec, kv_spec]
    out_shape = [jax.ShapeDtypeStruct((depth, bsz, t, FOX_WIDTH), F32),
                 jax.ShapeDtypeStruct((depth, bsz, t, FOX_WIDTH), F32)]
    n_fixed = 8
    if kv_bufs is None:
        kv_bufs = tuple(jnp.zeros(s.shape, s.dtype) for s in out_shape)
    aliased = list(kv_bufs)
    aliases = {n_fixed + j: j for j in range(len(aliased))}
    if attn_layout:
        out_specs += [pl.BlockSpec((rows, SUBLANES, tt), lambda b, i: (b, 0, i))]
        out_shape += [jax.ShapeDtypeStruct((bsz, SUBLANES, t), F32)]
    else:
        out_specs += [bt(LANES)]
        out_shape += [jax.ShapeDtypeStruct((bsz, t, LANES), F32)]
    if attn_layout:
        tiled = lambda n_, r_, w_: pl.BlockSpec((rows, FOX_HEADS, tt // n_, r_, w_), lambda b, i: (b, 0, i, 0, 0))
        out_specs += [tiled(FOX_QB, 2 * LANES, FOX_QB), bht(FOX_HEADS // 2, 2 * LANES),
                      tiled(FOX_KB, FOX_VT_ROWS, FOX_KB)]
        out_shape += [jax.ShapeDtypeStruct((bsz, FOX_HEADS, t // FOX_QB, 2 * LANES, FOX_QB), BF16),
                      jax.ShapeDtypeStruct((bsz, FOX_HEADS // 2, t, 2 * LANES), BF16),
                      jax.ShapeDtypeStruct((bsz, FOX_HEADS, t // FOX_KB, FOX_VT_ROWS, FOX_KB), BF16)]
    else:
        out_specs += [bt(FOX_WIDTH)]
        out_shape += [jax.ShapeDtypeStruct((bsz, t, FOX_WIDTH), BF16)]
    return pl.pallas_call(
        functools.partial(_fox_proj_body, tt=tt, rows=rows, attn_layout=attn_layout, n_alias=len(aliased)),
        grid=(bsz // rows, t // tt),
        in_specs=[bt(D_MODEL), _const_spec((1, D_MODEL)), _const_spec((D_MODEL, FOX_COLS), single=True),
                  _const_spec((1, FOX_WIDTH)), _const_spec((1, FOX_WIDTH)), _const_spec((1, LANES)),
                  _const_spec(pa.shape), _const_spec(arow.shape)]
                 + [pl.BlockSpec(memory_space=pl.ANY)] * len(aliased),
        out_specs=out_specs, out_shape=out_shape,
        input_output_aliases=aliases,
        scratch_shapes=[pltpu.VMEM((rows, 1, LANES), F32)],
        compiler_params=_params(("parallel", "arbitrary"), "fox_proj"),
        name="fox_proj",
    )(x, nm, w, qn, kn, fb, pa, arow, *aliased)


def _fox_prompt_body(qt_ref, k_ref, vt_ref, o_ref, m_s, acc_s, s_s, *, nsub):
    qi = pl.program_id(2)
    qtile = nsub * FOX_QB
    chains = [(si, e) for si in range(nsub) for e in range(2)]
    m_s[...] = jnp.full_like(m_s, -jnp.inf)
    acc_s[...] = jnp.zeros_like(acc_s)

    nch = len(chains)

    def scores(c, k0, nk, masked):
        si, e = chains[c]
        s = _dot(k_ref[pl.ds(k0, nk), :], qt_ref[e, si])
        if masked:
            r, col = _tri_mask(FOX_QB)
            tail = jnp.where(r <= col, s[nk - FOX_QB:], -jnp.inf)
            s = tail if nk == FOX_QB else jnp.concatenate([s[:nk - FOX_QB], tail], axis=0)
        return s

    def absorb(c, s, vt0):
        e = chains[c][1]
        m_old = m_s[c]
        m_new = jnp.maximum(m_old, jnp.max(s, axis=0, keepdims=True))
        alpha = jnp.exp(m_old - m_new)
        p = jnp.exp(s - m_new).astype(BF16)
        vts = [vt_ref[e, vt0 + j] for j in range(s.shape[0] // FOX_KB)]
        vt = vts[0] if len(vts) == 1 else jnp.concatenate(vts, axis=1)
        acc_s[c] = alpha * acc_s[c] + _dot(vt, p)
        m_s[c] = m_new

    def sweep(k0, vt0, nks, masked):
        lead = nch // 2
        for c in range(lead):
            s_s[c, 0:nks[c], :] = scores(c, k0, nks[c], masked)
        for c in range(nch):
            if c + lead < nch:
                s_s[c + lead, 0:nks[c + lead], :] = scores(c + lead, k0, nks[c + lead], masked)
            absorb(c, s_s[c, 0:nks[c], :], vt0)

    def below_diagonal(i, carry):
        sweep(pl.multiple_of(i * FOX_SWEEP, FOX_SWEEP), i * (FOX_SWEEP // FOX_KB), [FOX_SWEEP] * nch, False)
        return carry

    lax.fori_loop(0, qi * (qtile // FOX_SWEEP), below_diagonal, 0)
    sweep(pl.multiple_of(qi * qtile, qtile), qi * (qtile // FOX_KB),
          [(si + 1) * FOX_QB for si, _ in chains], True)

    for si in range(nsub):
        outs = []
        for e in range(2):
            acc = acc_s[chains.index((si, e))]
            outs.append(acc[0:FOX_HEAD_DIM] / acc[FOX_HEAD_DIM:FOX_HEAD_DIM + 1])
        o = jnp.concatenate(outs, axis=0).T
        o_ref[si * FOX_QB:(si + 1) * FOX_QB, :] = o.astype(o_ref.dtype)


def _fox_prompt_call(qt, kp, vt, tq):
    bsz, _, nqb, _, _ = qt.shape
    t = kp.shape[2]
    nsub = _row_tile(nqb, max(FOX_SWEEP // FOX_QB, tq // FOX_QB))
    npair = FOX_HEADS // 2
    nch = 2 * nsub
    return pl.pallas_call(
        functools.partial(_fox_prompt_body, nsub=nsub), grid=(bsz, npair, nqb // nsub),
        in_specs=[pl.BlockSpec((None, 2, nsub, 2 * LANES, FOX_QB), lambda b, p, i: (b, p, i, 0, 0)),
                  pl.BlockSpec((None, None, t, 2 * LANES), lambda b, p, i: (b, p, 0, 0)),
                  pl.BlockSpec((None, 2, t // FOX_KB, FOX_VT_ROWS, FOX_KB), lambda b, p, i: (b, p, 0, 0, 0))],
        out_specs=pl.BlockSpec((None, nsub * FOX_QB, LANES), lambda b, p, i: (b, i, p)),
        out_shape=jax.ShapeDtypeStruct((bsz, t, FOX_WIDTH), BF16),
        scratch_shapes=[pltpu.VMEM((nch, 1, FOX_QB), F32),
                        pltpu.VMEM((nch, FOX_VT_ROWS, FOX_QB), F32),
                        pltpu.VMEM((nch, nsub * FOX_QB, FOX_QB), F32)],
        compiler_params=_params(("parallel", "parallel", "arbitrary"), "fox_attn_prompt"),
        name="fox_attn_prompt",
    )(qt, kp, vt)


def _fox_sample_body(q_ref, kn_ref, vn_ref, lfn_ref, kc_ref, vc_ref, lfp_ref, o_ref, *, tn, past, seg):
    r, c = _tri_mask(seg)
    upper = (r <= c).astype(BF16)
    carry = jnp.zeros((FOX_HEADS, 1), F32)
    cps = []
    for j in range(past // seg):
        cs = carry + _dot_sel_rhs(lfp_ref[:, j * seg:(j + 1) * seg], upper)
        cps.append(cs)
        carry = cs[:, seg - 1:seg]
    cp = jnp.concatenate(cps, axis=1) if len(cps) > 1 else cps[0]
    lfn = lfn_ref[...]
    if tn < LANES:
        lfn = jnp.concatenate([lfn, jnp.zeros((LANES - tn, LANES), F32)], axis=0)
    r, c = _tri_mask(LANES)
    cn = _dot_sel_lhs((r >= c).astype(BF16), lfn)
    cn_t = cn.T

    q = q_ref[...]
    kn = kn_ref[...].astype(BF16)
    vn = vn_ref[...].astype(BF16)
    r, c = _tri_mask(tn)
    outs = []
    for hh in range(FOX_HEADS):
        hs = slice(hh * FOX_HEAD_DIM, (hh + 1) * FOX_HEAD_DIM)
        qh = q[:, hs]
        cq = cn[0:tn, hh:hh + 1]
        s_past = _dot(qh, kc_ref[hh].astype(BF16)) + ((carry[hh:hh + 1, :] + cq) - cp[hh:hh + 1, :])
        s_new = _dot_nt(qh, kn[:, hs]) + (cq - cn_t[hh:hh + 1, 0:tn])
        s_new = jnp.where(r >= c, s_new, -jnp.inf)
        m = jnp.maximum(jnp.max(s_past, axis=-1, keepdims=True), jnp.max(s_new, axis=-1, keepdims=True))
        p_past = jnp.exp(s_past - m)
        p_new = jnp.exp(s_new - m)
        denom = jnp.sum(p_past, axis=-1, keepdims=True) + jnp.sum(p_new, axis=-1, keepdims=True)
        o = _dot_nt(p_past.astype(BF16), vc_ref[hh].astype(BF16)) + _dot(p_new.astype(BF16), vn[:, hs])
        outs.append(o / denom)
    o_ref[...] = jnp.concatenate(outs, axis=1).astype(o_ref.dtype)


def _fox_sample_call(q, kn_all, vn_all, lfn, cache_k, cache_v, lfp_t, layer):
    bsz, tn, _ = q.shape
    past = cache_k.shape[-1]
    seg = _row_tile(past, 512)
    bt = lambda w_: pl.BlockSpec((None, tn, w_), lambda b: (b, 0, 0))
    new = pl.BlockSpec((None, None, tn, FOX_WIDTH), lambda b: (layer, b, 0, 0))
    cache = pl.BlockSpec((None, None, FOX_HEADS, FOX_HEAD_DIM, past), lambda b: (layer, b, 0, 0, 0))
    return pl.pallas_call(
        functools.partial(_fox_sample_body, tn=tn, past=past, seg=seg), grid=(bsz,),
        in_specs=[bt(FOX_WIDTH), new, new, bt(LANES), cache, cache,
                  pl.BlockSpec((None, None, FOX_HEADS, past), lambda b: (layer, b, 0, 0))],
        out_specs=bt(FOX_WIDTH),
        out_shape=jax.ShapeDtypeStruct((bsz, tn, FOX_WIDTH), BF16),
        compiler_params=_params(("parallel",), "fox_attn_sample"),
        name="fox_attn_sample",
    )(q, kn_all, vn_all, lfn, cache_k, cache_v, lfp_t)


def _pad_cols(a, width):
    return jnp.pad(a, ((0, 0), (0, width - a.shape[1])))


def _layer_params(l, norm_ffn1, w1_gate, w1_up, w1_down, norm_mix, w_in, ssd_conv_w, ssd_conv_b,
                  ssd_dt_bias, ssd_a_log, ssd_d, ssd_norm, gla_w_gate, gla_b_gate, gla_norm,
                  fox_q_norm, fox_k_norm, fox_f_bias, w_out, norm_ffn2, w2_gate, w2_up, w2_down):
    wi = w_in[l]
    row = lambda a: a.reshape(1, -1).astype(F32)
    hk = GLA_HEADS * GLA_DK
    p = dict(
        ffn1=(row(norm_ffn1[l]), w1_gate[l].astype(BF16), w1_up[l].astype(BF16), w1_down[l].astype(BF16)),
        ffn2=(row(norm_ffn2[l]), w2_gate[l].astype(BF16), w2_up[l].astype(BF16), w2_down[l].astype(BF16)),
        norm_mix=row(norm_mix[l]),
        w_out=w_out[l].astype(BF16),
        w_ssd=jnp.concatenate([wi[:, _OFF_Z:_OFF_DT], _pad_cols(wi[:, _OFF_DT:_OFF_GQ], LANES)], axis=1).astype(BF16),
        conv_w=ssd_conv_w[l].astype(F32),
        conv_b=row(ssd_conv_b[l]),
        dt_bias=_pad_cols(row(ssd_dt_bias[l]), LANES),
        a_log=_pad_cols(row(ssd_a_log[l]), LANES),
        d_exp=row(jnp.repeat(ssd_d[l], SSD_HEAD_DIM)),
        ssd_norm=row(ssd_norm[l]),
        w_gla=jnp.concatenate([wi[:, _OFF_GQ:_OFF_GR], _pad_cols(wi[:, _OFF_GR:_OFF_FQ], LANES)], axis=1).astype(BF16),
        w_gate=jnp.pad(gla_w_gate[l], ((0, LANES - GLA_RANK), (0, 0))).astype(BF16),
        b_gate=row(gla_b_gate[l]),
        gla_norm=row(jnp.tile(gla_norm[l], GLA_HEADS)),
        w_fox=jnp.concatenate([wi[:, _OFF_FQ:_OFF_FF], _pad_cols(wi[:, _OFF_FF:], LANES)], axis=1).astype(BF16),
        q_norm=row(jnp.tile(fox_q_norm[l], FOX_HEADS)),
        k_norm=row(jnp.tile(fox_k_norm[l], FOX_HEADS)),
        f_bias=_pad_cols(row(fox_f_bias[l]), LANES),
    )
    return p


def _gla_state_in(s):
    bsz = s.shape[0]
    eye = jnp.eye(GLA_HEADS, dtype=s.dtype)
    full = jnp.einsum('bhkv,hg->bhvgk', s, eye)
    return full.reshape(bsz, GLA_HEADS * GLA_DV, GLA_HEADS * GLA_DK)


def _gla_state_out(st):
    bsz = st.shape[0]
    full = st.reshape(bsz, GLA_HEADS, GLA_DV, GLA_HEADS, GLA_DK)
    idx = jnp.arange(GLA_HEADS)
    diag = full[:, idx, :, idx, :]
    return jnp.transpose(diag, (1, 0, 3, 2))


def _mix(x, p, conv_prev, ssd_h0, gla_s0, fox_cache, layer, depth, kv_bufs, tiles):
    bsz, t, _ = x.shape
    cprev = jnp.pad(conv_prev, ((0, 0), (SUBLANES - (SSD_CONV - 1), 0), (0, 0)))
    y_ssd, cnew, hnew = _ssd_call(x, p['norm_mix'], p['w_ssd'], p['conv_w'], p['conv_b'], p['dt_bias'],
                                  p['a_log'], p['d_exp'], p['ssd_norm'], cprev,
                                  ssd_h0.reshape(bsz, SSD_WIDTH, SSD_STATE), tiles['ssd'], tiles['ssd_rows'])
    o_gla, snew = _gla_call(x, p['norm_mix'], p['w_gla'], p['w_gate'], p['b_gate'], p['gla_norm'],
                            _gla_state_in(gla_s0), tiles['gla'], tiles['gla_rows'])
    proj = functools.partial(_fox_proj_call, x, p['norm_mix'], p['w_fox'], p['q_norm'], p['k_norm'], p['f_bias'],
                             tiles['fox_proj'], tiles['fox_rows'], layer=layer, depth=depth, kv_bufs=kv_bufs)
    if fox_cache is None:
        k_all, v_all, lf_t, qt, kp, vt = proj(attn_layout=True)
        o_fox = _fox_prompt_call(qt, kp, vt, tiles['fox_q'])
        lf_state = jnp.swapaxes(lf_t[:, 0:FOX_HEADS, :], 1, 2)
    else:
        k_all, v_all, lf, qs = proj(attn_layout=False)
        cache_k, cache_v, lfp_t = fox_cache
        o_fox = _fox_sample_call(qs, k_all, v_all, lf, cache_k, cache_v, lfp_t, layer)
        lf_state = lf[:, :, 0:FOX_HEADS]
    state = (cnew[:, SUBLANES - (SSD_CONV - 1):, :],
             hnew.reshape(bsz, SSD_HEADS, SSD_HEAD_DIM, SSD_STATE),
             _gla_state_out(snew),
             lf_state)
    return (y_ssd, o_gla, o_fox), state, (k_all, v_all)


def _trunk_layer(x, p, conv_prev, ssd_h0, gla_s0, fox_cache, layer, depth, kv_bufs, tiles):
    bsz, t, d = x.shape
    x1 = _ffn_call(x.reshape(bsz * t, d), *p['ffn1'], tiles['ffn']).reshape(bsz, t, d)
    (y_ssd, o_gla, o_fox), state, kv_bufs = _mix(x1, p, conv_prev, ssd_h0, gla_s0, fox_cache, layer, depth,
                                                 kv_bufs, tiles)
    flat = lambda a: a.reshape(bsz * t, a.shape[-1])
    x3 = _out_ffn_call(flat(x1), flat(y_ssd), flat(o_gla), flat(o_fox), p['w_out'], *p['ffn2'], tiles['ffn'])
    return x3.reshape(bsz, t, d), state, kv_bufs


PROMPT_TILES = dict(ffn=1024, ssd=256, ssd_rows=4, gla=256, gla_rows=4, fox_proj=512, fox_rows=2,
                    fox_q=1024)
SAMPLE_TILES = dict(ffn=512, ssd=64, ssd_rows=4, gla=64, gla_rows=4, fox_proj=64, fox_rows=8)


def kernel(x_prompt, x_sample, state_ssd_conv, state_ssd, state_gla, cache_fox_k, cache_fox_v, cache_fox_logf, norm_ffn1, w1_gate, w1_up, w1_down, norm_mix, w_in, ssd_conv_w, ssd_conv_b, ssd_dt_bias, ssd_a_log, ssd_d, ssd_norm, gla_w_gate, gla_b_gate, gla_norm, fox_q_norm, fox_k_norm, fox_f_bias, w_out, norm_ffn2, w2_gate, w2_up, w2_down):
    depth = w_in.shape[0]
    bp = x_prompt.shape[0]
    weights = (norm_ffn1, w1_gate, w1_up, w1_down, norm_mix, w_in, ssd_conv_w, ssd_conv_b, ssd_dt_bias,
               ssd_a_log, ssd_d, ssd_norm, gla_w_gate, gla_b_gate, gla_norm, fox_q_norm, fox_k_norm,
               fox_f_bias, w_out, norm_ffn2, w2_gate, w2_up, w2_down)
    cache_k = jnp.transpose(cache_fox_k, (0, 1, 3, 4, 2))
    cache_v = jnp.transpose(cache_fox_v, (0, 1, 3, 4, 2))
    lfp_t = jnp.swapaxes(cache_fox_logf, 2, 3)
    zeros_conv = jnp.zeros((bp, SSD_CONV - 1, SSD_CONV_CH), F32)
    zeros_ssd = jnp.zeros((bp, SSD_HEADS, SSD_HEAD_DIM, SSD_STATE), F32)
    zeros_gla = jnp.zeros((bp, GLA_HEADS, GLA_DK, GLA_DV), F32)

    xp, xs = x_prompt, x_sample
    p_new = [[] for _ in range(4)]
    s_new = [[] for _ in range(4)]
    kv_p = kv_s = None
    for l in range(depth):
        p = _layer_params(l, *weights)
        xp, st_p, kv_p = _trunk_layer(xp, p, zeros_conv, zeros_ssd, zeros_gla, None, l, depth, kv_p, PROMPT_TILES)
        xs, st_s, kv_s = _trunk_layer(xs, p, state_ssd_conv[l], state_ssd[l], state_gla[l],
                                      (cache_k, cache_v, lfp_t), l, depth, kv_s, SAMPLE_TILES)
        for i in range(4):
            p_new[i].append(st_p[i])
            s_new[i].append(st_s[i])

    def leaves(small, kv):
        conv, ssd, gla, logf = [jnp.stack(a) for a in small]
        heads = lambda a: a.reshape(a.shape[:3] + (FOX_HEADS, FOX_HEAD_DIM))
        return conv, ssd, gla, heads(kv[0]), heads(kv[1]), logf

    return (xp, xs, *leaves(p_new, kv_p), *leaves(s_new, kv_s))
```

```python
import functools

import numpy as np
import jax
import jax.numpy as jnp
from jax import lax
from jax.experimental import pallas as pl
from jax.experimental.pallas import tpu as pltpu

F32 = jnp.float32
BF16 = jnp.bfloat16

EPS = 1e-6
D_MODEL = 1024
D_FF = 2816
SSD_HEADS = 8
SSD_HEAD_DIM = 64
SSD_WIDTH = 512
SSD_GROUPS = 2
SSD_STATE = 128
SSD_CONV = 4
SSD_CONV_CH = 1024
GLA_HEADS = 4
GLA_DK = 32
GLA_DV = 64
GLA_WIDTH = 256
GLA_RANK = 16
GLA_TAU = 16.0
GLA_CHUNK = 64
FOX_HEADS = 4
FOX_HEAD_DIM = 64
FOX_WIDTH = 256
LANES = 128
SUBLANES = 8
MIB = 1024 * 1024

_OFF_Z, _OFF_XBC, _OFF_DT = 0, 512, 1536
_OFF_GQ, _OFF_GR = 1544, 2312
_OFF_FQ, _OFF_FF = 2328, 3096
SSD_COLS = SSD_WIDTH + SSD_CONV_CH + LANES
GLA_COLS = 2 * GLA_HEADS * GLA_DK + 2 * GLA_WIDTH + LANES
FOX_COLS = 3 * FOX_WIDTH + LANES
AUX_PARTS = 3
FOX_QB = 256
FOX_KB = 256
FOX_SWEEP = 1024
FOX_VT_ROWS = FOX_HEAD_DIM + 16


def _dot(a, b):
    return jnp.dot(a, b, preferred_element_type=F32)


def _dot_nt(a, b):
    return lax.dot_general(a, b, (((1,), (1,)), ((), ())), preferred_element_type=F32)


def _dot_tn(a, b):
    return lax.dot_general(a, b, (((0,), (0,)), ((), ())), preferred_element_type=F32)


def _split_bf16(x, parts):
    out = []
    r = x
    for i in range(parts):
        p = r.astype(BF16)
        out.append(p)
        if i + 1 < parts:
            r = r - p.astype(F32)
    return out


def _dot_sel_lhs(sel, x, parts=3):
    n = x.shape[1]
    t = _dot(sel, jnp.concatenate(_split_bf16(x, parts), axis=1))
    acc = t[:, 0:n]
    for i in range(1, parts):
        acc = acc + t[:, i * n:(i + 1) * n]
    return acc


def _dot_sel_rhs(x, sel, parts=3):
    m = x.shape[0]
    pieces = _split_bf16(x, parts)
    if m % (2 * SUBLANES):
        acc = _dot(pieces[0], sel)
        for p in pieces[1:]:
            acc = acc + _dot(p, sel)
        return acc
    t = _dot(jnp.concatenate(pieces, axis=0), sel)
    acc = t[0:m]
    for i in range(1, parts):
        acc = acc + t[i * m:(i + 1) * m]
    return acc


def _rms(x, w):
    ms = jnp.mean(x * x, axis=-1, keepdims=True)
    return x * lax.rsqrt(ms + EPS) * w


def _silu(x):
    return x * jax.nn.sigmoid(x)


def _softplus(x):
    return jnp.maximum(x, 0.0) + jnp.log1p(jnp.exp(-jnp.abs(x)))


def _log_sigmoid(x):
    return -_softplus(-x)


def _tri_mask(n, m=None):
    m = n if m is None else m
    r = lax.broadcasted_iota(jnp.int32, (n, m), 0)
    c = lax.broadcasted_iota(jnp.int32, (n, m), 1)
    return r, c


def _lane_pair_select(a_even, a_odd):
    lane = lax.broadcasted_iota(jnp.int32, a_even.shape, 1)
    return jnp.where(lane < 64, a_even, a_odd)


def _expand_heads(v, heads, rows):
    pieces = []
    for i in range(0, len(heads), 2):
        a = jnp.broadcast_to(v[:, heads[i]:heads[i] + 1], (rows, LANES))
        b = jnp.broadcast_to(v[:, heads[i + 1]:heads[i + 1] + 1], (rows, LANES))
        pieces.append(_lane_pair_select(a, b))
    return pieces[0] if len(pieces) == 1 else jnp.concatenate(pieces, axis=1)


def _group_mean_matrix(width, group):
    r, c = _tri_mask(width)
    return jnp.where((r // group) == (c // group), 1.0 / group, 0.0).astype(BF16)


def _project_rows(h, w_ref, tt, rows):
    if rows == 1 or tt < 2 * LANES:
        return _dot(h, w_ref[...])
    return jnp.concatenate([_dot(h[i * tt:(i + 1) * tt], w_ref[...]) for i in range(rows)], axis=0)


VMEM_LIMIT_MIB = dict(ffn=56, out_ffn=56, ssd=48, gla=32, fox_proj=40, fox_attn_prompt=40, fox_attn_sample=48)


def _params(sem, call):
    return pltpu.CompilerParams(dimension_semantics=sem, vmem_limit_bytes=VMEM_LIMIT_MIB[call] * MIB)


def _const_spec(shape, single=False):
    nd = len(shape)
    if single:
        return pl.BlockSpec(shape, lambda *_: (0,) * nd, pipeline_mode=pl.Buffered(1))
    return pl.BlockSpec(shape, lambda *_: (0,) * nd)


def _swiglu_half(x, g_ref, wg_ref, wu_ref, wd_ref):
    h = _rms(x, g_ref[...]).astype(BF16)
    gate = _dot(h, wg_ref[...])
    up = _dot(h, wu_ref[...])
    a = (_silu(gate) * up).astype(BF16)
    return x + 0.5 * _dot(a, wd_ref[...])


def _ffn_body(x_ref, g_ref, wg_ref, wu_ref, wd_ref, o_ref):
    o_ref[...] = _swiglu_half(x_ref[...], g_ref, wg_ref, wu_ref, wd_ref)


def _out_ffn_body(x_ref, ys_ref, og_ref, of_ref, wo_ref, g_ref, wg_ref, wu_ref, wd_ref, o_ref):
    x = x_ref[...]
    x = x + (_dot(ys_ref[...], wo_ref[0:SSD_WIDTH, :])
             + _dot(og_ref[...], wo_ref[SSD_WIDTH:SSD_WIDTH + GLA_WIDTH, :])
             + _dot(of_ref[...], wo_ref[SSD_WIDTH + GLA_WIDTH:, :]))
    o_ref[...] = _swiglu_half(x, g_ref, wg_ref, wu_ref, wd_ref)


def _ffn_weight_specs():
    return [_const_spec((1, D_MODEL)),
            _const_spec((D_MODEL, D_FF), single=True),
            _const_spec((D_MODEL, D_FF), single=True),
            _const_spec((D_FF, D_MODEL), single=True)]


def _row_tile(n, want):
    t = min(want, n)
    while n % t:
        t //= 2
    return t


def _ffn_call(x, g, wg, wu, wd, tm):
    n = x.shape[0]
    tm = _row_tile(n, tm)
    row = pl.BlockSpec((tm, D_MODEL), lambda i: (i, 0))
    return pl.pallas_call(
        _ffn_body, grid=(n // tm,),
        in_specs=[row] + _ffn_weight_specs(),
        out_specs=row,
        out_shape=jax.ShapeDtypeStruct((n, D_MODEL), F32),
        compiler_params=_params(("parallel",), "ffn"),
        name="ffn",
    )(x, g, wg, wu, wd)


def _out_ffn_call(x, ys, og, of, wo, g, wg, wu, wd, tm):
    n = x.shape[0]
    tm = _row_tile(n, tm)
    row = lambda w: pl.BlockSpec((tm, w), lambda i: (i, 0))
    return pl.pallas_call(
        _out_ffn_body, grid=(n // tm,),
        in_specs=[row(D_MODEL), row(SSD_WIDTH), row(GLA_WIDTH), row(FOX_WIDTH),
                  _const_spec((D_MODEL, D_MODEL), single=True)] + _ffn_weight_specs(),
        out_specs=row(D_MODEL),
        out_shape=jax.ShapeDtypeStruct((n, D_MODEL), F32),
        compiler_params=_params(("parallel",), "out_ffn"),
        name="out_ffn",
    )(x, ys, og, of, wo, g, wg, wu, wd)


def _ssd_body(x_ref, nm_ref, w_ref, cw_ref, cb_ref, dtb_ref, alog_ref, dexp_ref, nw_ref,
              cprev_ref, h0_ref, y_ref, cnew_ref, hnew_ref, xbuf, hst, *, tt, rows):
    t = pl.program_id(1)

    @pl.when(t == 0)
    def _():
        xbuf[...] = cprev_ref[...]
        hst[...] = h0_ref[...]

    x = x_ref[...].reshape(rows * tt, D_MODEL)
    h = _rms(x, nm_ref[...]).astype(BF16)
    u = _project_rows(h, w_ref, tt, rows)
    z = u[:, 0:SSD_WIDTH]
    xbc = u[:, SSD_WIDTH:SSD_WIDTH + SSD_CONV_CH]
    dt_raw = u[:, SSD_WIDTH + SSD_CONV_CH:]

    convs = []
    row8 = lax.broadcasted_iota(jnp.int32, (SUBLANES, SSD_CONV_CH), 0)
    for i in range(rows):
        xi = xbc[i * tt:(i + 1) * tt]
        prev = xbuf[i]
        conv = None
        for j in range(SSD_CONV - 1, 0, -1):
            rolled = pltpu.roll(xi, j, 0)
            head = jnp.where(row8 < j, pltpu.roll(prev, j, 0), rolled[0:SUBLANES])
            term = jnp.concatenate([head, rolled[SUBLANES:]], axis=0) * cw_ref[SSD_CONV - 1 - j:SSD_CONV - j, :]
            conv = term if conv is None else conv + term
        convs.append(conv + xi * cw_ref[SSD_CONV - 1:SSD_CONV, :])
        xbuf[i] = xi[tt - SUBLANES:, :]
    xa = _silu(jnp.concatenate(convs, axis=0) + cb_ref[...])

    xs = xa[:, 0:SSD_WIDTH]
    bm = xa[:, SSD_WIDTH:SSD_WIDTH + SSD_GROUPS * SSD_STATE].astype(BF16)
    cm = xa[:, SSD_WIDTH + SSD_GROUPS * SSD_STATE:].astype(BF16)
    dt = _softplus(dt_raw + dtb_ref[...])
    a = dt * (-jnp.exp(alog_ref[...]))

    r, c = _tri_mask(tt)
    causal = r >= c
    tri = causal.astype(BF16)
    acum = jnp.concatenate(
        [_dot_sel_lhs(tri, a[i * tt:(i + 1) * tt]) for i in range(rows)], axis=0)
    shift_t = (acum - jnp.log(dt)).T
    alast = jnp.concatenate(
        [jnp.broadcast_to(acum[(i + 1) * tt - 1:(i + 1) * tt, :], (tt, LANES)) for i in range(rows)], axis=0)
    ea = jnp.exp(acum)
    wend = jnp.exp(alast - acum) * dt

    heads_per_group = SSD_HEADS // SSD_GROUPS
    gw = heads_per_group * SSD_HEAD_DIM
    heads_of = [list(range(g * heads_per_group, (g + 1) * heads_per_group)) for g in range(SSD_GROUPS)]
    nrow = rows * tt
    xb = xs.astype(BF16)
    xw = jnp.concatenate([xs[:, g * gw:(g + 1) * gw] * _expand_heads(wend, heads_of[g], nrow)
                          for g in range(SSD_GROUPS)], axis=1).astype(BF16)
    ea_x = jnp.concatenate([_expand_heads(ea, heads_of[g], nrow) for g in range(SSD_GROUPS)], axis=1)

    units = [(i, g) for g in range(SSD_GROUPS) for i in range(rows)]
    rs = {i: slice(i * tt, (i + 1) * tt) for i in range(rows)}
    gs = {g: slice(g * SSD_STATE, (g + 1) * SSD_STATE) for g in range(SSD_GROUPS)}
    cbs = {(i, g): _dot_nt(cm[rs[i], gs[g]], bm[rs[i], gs[g]]) for i, g in units}
    upds = {(i, g): _dot_tn(xw[rs[i], g * gw:(g + 1) * gw], bm[rs[i], gs[g]]) for i, g in units}
    y_unit = {}
    for i, g in units:
        heads = heads_of[g]
        intra = []
        for pi in range(heads_per_group // 2):
            ys = []
            for e in range(2):
                hh = heads[2 * pi + e]
                seg = acum[rs[i], hh:hh + 1] - shift_t[hh:hh + 1, rs[i]]
                m = (cbs[(i, g)] * jnp.where(causal, jnp.exp(seg), 0.0)).astype(BF16)
                col = g * gw + pi * LANES
                ys.append(_dot(m, xb[rs[i], col:col + LANES]))
            intra.append(_lane_pair_select(ys[0], ys[1]))
        hg = hst[i, g * gw:(g + 1) * gw, :]
        y_inter = _dot_nt(cm[rs[i], gs[g]], hg.astype(BF16)) * ea_x[rs[i], g * gw:(g + 1) * gw]
        y_unit[(i, g)] = jnp.concatenate(intra, axis=1) + y_inter
        cd = jnp.concatenate(
            [jnp.broadcast_to(jnp.exp(acum[(i + 1) * tt - 1:(i + 1) * tt, hh:hh + 1]), (SSD_HEAD_DIM, SSD_STATE))
             for hh in heads], axis=0)
        hst[i, g * gw:(g + 1) * gw, :] = hg * cd + upds[(i, g)]

    y = jnp.concatenate(
        [jnp.concatenate([y_unit[(i, g)] for g in range(SSD_GROUPS)], axis=1) for i in range(rows)], axis=0)
    y = (y + dexp_ref[...] * xs) * _silu(z)
    nw = nw_ref[...]
    outs = [_rms(y[:, g * gw:(g + 1) * gw], nw[:, g * gw:(g + 1) * gw]) for g in range(SSD_GROUPS)]
    y_ref[...] = jnp.concatenate(outs, axis=1).astype(y_ref.dtype).reshape(rows, tt, SSD_WIDTH)

    @pl.when(t == pl.num_programs(1) - 1)
    def _():
        cnew_ref[...] = xbuf[...]
        hnew_ref[...] = hst[...]


def _ssd_call(x, nm, w, cw, cb, dtb, alog, dexp, nw, cprev, h0, tt, rows):
    bsz, t, _ = x.shape
    tt = _row_tile(t, tt)
    rows = _row_tile(bsz, rows)
    bt = lambda w_: pl.BlockSpec((rows, tt, w_), lambda b, i: (b, i, 0))
    per_b = lambda r_, w_: pl.BlockSpec((rows, r_, w_), lambda b, i: (b, 0, 0))
    return pl.pallas_call(
        functools.partial(_ssd_body, tt=tt, rows=rows), grid=(bsz // rows, t // tt),
        in_specs=[bt(D_MODEL), _const_spec((1, D_MODEL)), _const_spec((D_MODEL, SSD_COLS), single=True),
                  _const_spec((SSD_CONV, SSD_CONV_CH)), _const_spec((1, SSD_CONV_CH)),
                  _const_spec((1, LANES)), _const_spec((1, LANES)),
                  _const_spec((1, SSD_WIDTH)), _const_spec((1, SSD_WIDTH)),
                  per_b(SUBLANES, SSD_CONV_CH), per_b(SSD_WIDTH, SSD_STATE)],
        out_specs=[bt(SSD_WIDTH), per_b(SUBLANES, SSD_CONV_CH), per_b(SSD_WIDTH, SSD_STATE)],
        out_shape=[jax.ShapeDtypeStruct((bsz, t, SSD_WIDTH), BF16),
                   jax.ShapeDtypeStruct((bsz, SUBLANES, SSD_CONV_CH), F32),
                   jax.ShapeDtypeStruct((bsz, SSD_WIDTH, SSD_STATE), F32)],
        scratch_shapes=[pltpu.VMEM((rows, SUBLANES, SSD_CONV_CH), F32),
                        pltpu.VMEM((rows, SSD_WIDTH, SSD_STATE), F32)],
        compiler_params=_params(("parallel", "arbitrary"), "ssd"),
        name="ssd",
    )(x, nm, w, cw, cb, dtb, alog, dexp, nw, cprev, h0)


def _gla_body(x_ref, nm_ref, w_ref, wgate_ref, bgate_ref, gn_ref, s0_ref, o_ref, snew_ref, st, *, tt, rows):
    t = pl.program_id(1)

    @pl.when(t == 0)
    def _():
        st[...] = s0_ref[...]

    hk = GLA_HEADS * GLA_DK
    nchunk = tt // GLA_CHUNK
    x = x_ref[...].reshape(rows * tt, D_MODEL)
    h = _rms(x, nm_ref[...]).astype(BF16)
    u = _project_rows(h, w_ref, tt, rows)
    q = u[:, 0:hk] * (GLA_DK ** -0.5)
    k = u[:, hk:2 * hk]
    v = u[:, 2 * hk:2 * hk + GLA_WIDTH].astype(BF16)
    gg = u[:, 2 * hk + GLA_WIDTH:2 * hk + 2 * GLA_WIDTH]
    gr = u[:, 2 * hk + 2 * GLA_WIDTH:].astype(BF16)
    la = _log_sigmoid(_dot(gr, wgate_ref[...]) + bgate_ref[...]) / GLA_TAU

    r, c = _tri_mask(tt)
    sel = (((r // GLA_CHUNK) == (c // GLA_CHUNK)) & (r >= c)).astype(BF16)
    bcum = jnp.concatenate(
        [_dot_sel_lhs(sel, la[i * tt:(i + 1) * tt]) for i in range(rows)], axis=0)
    blast = jnp.concatenate(
        [jnp.broadcast_to(bcum[(ci + 1) * GLA_CHUNK - 1:(ci + 1) * GLA_CHUNK, :], (GLA_CHUNK, hk))
         for ci in range(rows * nchunk)], axis=0)
    qd = (q * jnp.exp(bcum)).astype(BF16)
    kd = (k * jnp.exp(-bcum)).astype(BF16)
    kend = (k * jnp.exp(blast - bcum)).astype(BF16)
    eblast = jnp.exp(blast)

    lane_k = lax.broadcasted_iota(jnp.int32, (GLA_CHUNK, hk), 1) // GLA_DK
    lane_v = lax.broadcasted_iota(jnp.int32, (GLA_CHUNK, GLA_WIDTH), 1) // GLA_DV
    ar, ac = _tri_mask(GLA_HEADS * GLA_CHUNK, GLA_CHUNK)
    att_causal = (ar % GLA_CHUNK) >= ac
    sr, sc = _tri_mask(GLA_WIDTH, hk)
    diag = (sr // GLA_DV) == (sc // GLA_DK)

    units = [(i, ci) for ci in range(nchunk) for i in range(rows)]
    sl = {(i, ci): slice(i * tt + ci * GLA_CHUNK, i * tt + (ci + 1) * GLA_CHUNK) for i, ci in units}
    att, upd, o_intra = {}, {}, {}
    for un in units:
        qd_c = qd[sl[un]]
        lhs = jnp.concatenate([jnp.where(lane_k == hh, qd_c, jnp.zeros_like(qd_c))
                               for hh in range(GLA_HEADS)], axis=0)
        att[un] = jnp.where(att_causal, _dot_nt(lhs, kd[sl[un]]), 0.0).astype(BF16)
        upd[un] = jnp.where(diag, _dot_tn(v[sl[un]], kend[sl[un]]), 0.0)
    for un in units:
        res = _dot(att[un], v[sl[un]])
        acc = jnp.zeros((GLA_CHUNK, GLA_WIDTH), F32)
        for hh in range(GLA_HEADS):
            acc = jnp.where(lane_v == hh, res[hh * GLA_CHUNK:(hh + 1) * GLA_CHUNK], acc)
        o_intra[un] = acc
    outs = {}
    for un in units:
        i = un[0]
        s_prev = st[i]
        outs[un] = o_intra[un] + _dot_nt(qd[sl[un]], s_prev.astype(BF16))
        st[i] = s_prev * eblast[sl[un].start:sl[un].start + 1, :] + upd[un]

    o = jnp.concatenate([outs[(i, ci)] for i in range(rows) for ci in range(nchunk)], axis=0)
    ms = _dot_sel_rhs(o * o, _group_mean_matrix(GLA_WIDTH, GLA_DV), parts=2)
    o = o * lax.rsqrt(ms + EPS) * gn_ref[...]
    o_ref[...] = (o * _silu(gg)).astype(o_ref.dtype).reshape(rows, tt, GLA_WIDTH)

    @pl.when(t == pl.num_programs(1) - 1)
    def _():
        snew_ref[...] = st[...]


def _gla_call(x, nm, w, wgate, bgate, gn, s0, tt, rows):
    bsz, t, _ = x.shape
    tt = _row_tile(t, tt)
    rows = _row_tile(bsz, rows)
    hk = GLA_HEADS * GLA_DK
    bt = lambda w_: pl.BlockSpec((rows, tt, w_), lambda b, i: (b, i, 0))
    per_b = pl.BlockSpec((rows, GLA_WIDTH, hk), lambda b, i: (b, 0, 0))
    return pl.pallas_call(
        functools.partial(_gla_body, tt=tt, rows=rows), grid=(bsz // rows, t // tt),
        in_specs=[bt(D_MODEL), _const_spec((1, D_MODEL)), _const_spec((D_MODEL, GLA_COLS), single=True),
                  _const_spec((LANES, hk)), _const_spec((1, hk)), _const_spec((1, GLA_WIDTH)), per_b],
        out_specs=[bt(GLA_WIDTH), per_b],
        out_shape=[jax.ShapeDtypeStruct((bsz, t, GLA_WIDTH), BF16),
                   jax.ShapeDtypeStruct((bsz, GLA_WIDTH, hk), F32)],
        scratch_shapes=[pltpu.VMEM((rows, GLA_WIDTH, hk), F32)],
        compiler_params=_params(("parallel", "arbitrary"), "gla"),
        name="gla",
    )(x, nm, w, wgate, bgate, gn, s0)


def _fox_proj_body(x_ref, nm_ref, w_ref, qn_ref, kn_ref, fb_ref, pa_ref, arow_ref, *rest,
                   tt, rows, attn_layout, n_alias):
    k_ref, v_ref, lf_ref, *rest = rest[n_alias:]
    if attn_layout:
        qt_ref, kp_ref, vt_ref, carry = rest
    else:
        qs_ref, carry = rest
    t = pl.program_id(1)

    @pl.when(t == 0)
    def _():
        carry[...] = jnp.zeros_like(carry)

    nrow = rows * tt
    h = _rms(x_ref[...].reshape(nrow, D_MODEL), nm_ref[...]).astype(BF16)
    u = _project_rows(h, w_ref, tt, rows)
    fq = u[:, 0:FOX_WIDTH]
    fk = u[:, FOX_WIDTH:2 * FOX_WIDTH]
    fv = u[:, 2 * FOX_WIDTH:3 * FOX_WIDTH]
    ff = u[:, 3 * FOX_WIDTH:]

    gmat = _group_mean_matrix(FOX_WIDTH, FOX_HEAD_DIM)
    qn = fq * lax.rsqrt(_dot_sel_rhs(fq * fq, gmat, parts=2) + EPS) * qn_ref[...]
    kn = fk * lax.rsqrt(_dot_sel_rhs(fk * fk, gmat, parts=2) + EPS) * kn_ref[...]
    lf = _log_sigmoid(ff + fb_ref[...])
    k_ref[...] = kn.reshape(rows, tt, FOX_WIDTH)
    v_ref[...] = fv.reshape(rows, tt, FOX_WIDTH)
    qs = (qn * (FOX_HEAD_DIM ** -0.5)).astype(BF16)

    if not attn_layout:
        lf_ref[...] = lf.reshape(rows, tt, LANES)
        qs_ref[...] = qs.reshape(rows, tt, FOX_WIDTH)
        return

    r, c = _tri_mask(tt)
    tri = (r >= c).astype(BF16)
    cums = []
    for i in range(rows):
        lf_i = lf[i * tt:(i + 1) * tt]
        lf_ref[i] = lf_i.T[0:SUBLANES]
        cum_i = carry[i] + _dot_sel_lhs(tri, lf_i)
        carry[i] = cum_i[tt - 1:tt, :]
        cums.append(cum_i)
    cum = jnp.concatenate(cums, axis=0) if rows > 1 else cums[0]
    c3 = jnp.concatenate(_split_bf16(cum, AUX_PARTS), axis=1)
    aux = _dot(c3, pa_ref[...])
    aux_k = (aux[:, 0:LANES] + arow_ref[0:1, :]).astype(BF16)
    aux_q_all = aux[:, LANES:]
    knb = kn.astype(BF16)
    fv_t = fv.T
    ones = jnp.ones((FOX_VT_ROWS - FOX_HEAD_DIM, nrow), F32)
    lane = lax.broadcasted_iota(jnp.int32, (nrow, LANES), 1)
    for p in range(FOX_HEADS // 2):
        sl = slice(p * LANES, (p + 1) * LANES)
        for i in range(rows):
            kp_ref[i, p, :, 0:LANES] = knb[i * tt:(i + 1) * tt, sl]
            kp_ref[i, p, :, LANES:] = aux_k[i * tt:(i + 1) * tt]
        for e in range(2):
            hh = 2 * p + e
            vt = jnp.concatenate([fv_t[hh * FOX_HEAD_DIM:(hh + 1) * FOX_HEAD_DIM, :], ones], axis=0).astype(BF16)
            qmask = jnp.where((lane // FOX_HEAD_DIM) == e, qs[:, sl], jnp.zeros_like(qs[:, sl])).astype(F32)
            aux_q = jnp.where((lane // AUX_PARTS) == hh, aux_q_all, 0.0) + arow_ref[1 + hh:2 + hh, :]
            q_t = jnp.concatenate([qmask.T, aux_q.T], axis=0).astype(BF16)
            for i in range(rows):
                for j in range(tt // FOX_KB):
                    vt_ref[i, hh, j] = vt[:, i * tt + j * FOX_KB:i * tt + (j + 1) * FOX_KB]
                for j in range(tt // FOX_QB):
                    qt_ref[i, hh, j] = q_t[:, i * tt + j * FOX_QB:i * tt + (j + 1) * FOX_QB]


def _fox_aux_constants():
    nslot = FOX_HEADS * AUX_PARTS
    pa = np.zeros((AUX_PARTS * LANES, 2 * LANES), np.float32)
    arow = np.zeros((SUBLANES, LANES), np.float32)
    for hh in range(FOX_HEADS):
        for j in range(AUX_PARTS):
            pa[j * LANES + hh, LANES + AUX_PARTS * hh + j] = 1.0
            pa[j * LANES + hh, nslot + AUX_PARTS * hh + j] = -1.0
            arow[0, AUX_PARTS * hh + j] = 1.0
            arow[1 + hh, nslot + AUX_PARTS * hh + j] = 1.0
    return jnp.asarray(pa, BF16), jnp.asarray(arow, F32)


def _fox_proj_call(x, nm, w, qn, kn, fb, tt, rows, attn_layout, layer, depth, kv_bufs):
    bsz, t, _ = x.shape
    tt = _row_tile(t, tt)
    rows = _row_tile(bsz, rows)
    pa, arow = _fox_aux_constants()
    bt = lambda w_: pl.BlockSpec((rows, tt, w_), lambda b, i: (b, i, 0))
    bht = lambda n_, w_: pl.BlockSpec((rows, n_, tt, w_), lambda b, i: (b, 0, i, 0))
    kv_spec = pl.BlockSpec((None, rows, tt, FOX_WIDTH), lambda b, i: (layer, b, i, 0))
    out_specs = [kv_spec, kv_spec]
    out_shape = [jax.ShapeDtypeStruct((depth, bsz, t, FOX_WIDTH), F32),
                 jax.ShapeDtypeStruct((depth, bsz, t, FOX_WIDTH), F32)]
    n_fixed = 8
    if kv_bufs is None:
        kv_bufs = tuple(jnp.zeros(s.shape, s.dtype) for s in out_shape)
    aliased = list(kv_bufs)
    aliases = {n_fixed + j: j for j in range(len(aliased))}
    if attn_layout:
        out_specs += [pl.BlockSpec((rows, SUBLANES, tt), lambda b, i: (b, 0, i))]
        out_shape += [jax.ShapeDtypeStruct((bsz, SUBLANES, t), F32)]
    else:
        out_specs += [bt(LANES)]
        out_shape += [jax.ShapeDtypeStruct((bsz, t, LANES), F32)]
    if attn_layout:
        tiled = lambda n_, r_, w_: pl.BlockSpec((rows, FOX_HEADS, tt // n_, r_, w_), lambda b, i: (b, 0, i, 0, 0))
        out_specs += [tiled(FOX_QB, 2 * LANES, FOX_QB), bht(FOX_HEADS // 2, 2 * LANES),
                      tiled(FOX_KB, FOX_VT_ROWS, FOX_KB)]
        out_shape += [jax.ShapeDtypeStruct((bsz, FOX_HEADS, t // FOX_QB, 2 * LANES, FOX_QB), BF16),
                      jax.ShapeDtypeStruct((bsz, FOX_HEADS // 2, t, 2 * LANES), BF16),
                      jax.ShapeDtypeStruct((bsz, FOX_HEADS, t // FOX_KB, FOX_VT_ROWS, FOX_KB), BF16)]
    else:
        out_specs += [bt(FOX_WIDTH)]
        out_shape += [jax.ShapeDtypeStruct((bsz, t, FOX_WIDTH), BF16)]
    return pl.pallas_call(
        functools.partial(_fox_proj_body, tt=tt, rows=rows, attn_layout=attn_layout, n_alias=len(aliased)),
        grid=(bsz // rows, t // tt),
        in_specs=[bt(D_MODEL), _const_spec((1, D_MODEL)), _const_spec((D_MODEL, FOX_COLS), single=True),
                  _const_spec((1, FOX_WIDTH)), _const_spec((1, FOX_WIDTH)), _const_spec((1, LANES)),
                  _const_spec(pa.shape), _const_spec(arow.shape)]
                 + [pl.BlockSpec(memory_space=pl.ANY)] * len(aliased),
        out_specs=out_specs, out_shape=out_shape,
        input_output_aliases=aliases,
        scratch_shapes=[pltpu.VMEM((rows, 1, LANES), F32)],
        compiler_params=_params(("parallel", "arbitrary"), "fox_proj"),
        name="fox_proj",
    )(x, nm, w, qn, kn, fb, pa, arow, *aliased)


def _fox_prompt_body(qt_ref, k_ref, vt_ref, o_ref, m_s, acc_s, s_s, *, nsub):
    qi = pl.program_id(2)
    qtile = nsub * FOX_QB
    chains = [(si, e) for si in range(nsub) for e in range(2)]
    m_s[...] = jnp.full_like(m_s, -jnp.inf)
    acc_s[...] = jnp.zeros_like(acc_s)

    nch = len(chains)

    def scores(c, k0, nk, masked):
        si, e = chains[c]
        s = _dot(k_ref[pl.ds(k0, nk), :], qt_ref[e, si])
        if masked:
            r, col = _tri_mask(FOX_QB)
            tail = jnp.where(r <= col, s[nk - FOX_QB:], -jnp.inf)
            s = tail if nk == FOX_QB else jnp.concatenate([s[:nk - FOX_QB], tail], axis=0)
        return s

    def absorb(c, s, vt0):
        e = chains[c][1]
        m_old = m_s[c]
        m_new = jnp.maximum(m_old, jnp.max(s, axis=0, keepdims=True))
        alpha = jnp.exp(m_old - m_new)
        p = jnp.exp(s - m_new).astype(BF16)
        vts = [vt_ref[e, vt0 + j] for j in range(s.shape[0] // FOX_KB)]
        vt = vts[0] if len(vts) == 1 else jnp.concatenate(vts, axis=1)
        acc_s[c] = alpha * acc_s[c] + _dot(vt, p)
        m_s[c] = m_new

    def sweep(k0, vt0, nks, masked):
        lead = nch // 2
        for c in range(lead):
            s_s[c, 0:nks[c], :] = scores(c, k0, nks[c], masked)
        for c in range(nch):
            if c + lead < nch:
                s_s[c + lead, 0:nks[c + lead], :] = scores(c + lead, k0, nks[c + lead], masked)
            absorb(c, s_s[c, 0:nks[c], :], vt0)

    def below_diagonal(i, carry):
        sweep(pl.multiple_of(i * FOX_SWEEP, FOX_SWEEP), i * (FOX_SWEEP // FOX_KB), [FOX_SWEEP] * nch, False)
        return carry

    lax.fori_loop(0, qi * (qtile // FOX_SWEEP), below_diagonal, 0)
    sweep(pl.multiple_of(qi * qtile, qtile), qi * (qtile // FOX_KB),
          [(si + 1) * FOX_QB for si, _ in chains], True)

    for si in range(nsub):
        outs = []
        for e in range(2):
            acc = acc_s[chains.index((si, e))]
            outs.append(acc[0:FOX_HEAD_DIM] / acc[FOX_HEAD_DIM:FOX_HEAD_DIM + 1])
        o = jnp.concatenate(outs, axis=0).T
        o_ref[si * FOX_QB:(si + 1) * FOX_QB, :] = o.astype(o_ref.dtype)


def _fox_prompt_call(qt, kp, vt, tq):
    bsz, _, nqb, _, _ = qt.shape
    t = kp.shape[2]
    nsub = _row_tile(nqb, max(FOX_SWEEP // FOX_QB, tq // FOX_QB))
    npair = FOX_HEADS // 2
    nch = 2 * nsub
    return pl.pallas_call(
        functools.partial(_fox_prompt_body, nsub=nsub), grid=(bsz, npair, nqb // nsub),
        in_specs=[pl.BlockSpec((None, 2, nsub, 2 * LANES, FOX_QB), lambda b, p, i: (b, p, i, 0, 0)),
                  pl.BlockSpec((None, None, t, 2 * LANES), lambda b, p, i: (b, p, 0, 0)),
                  pl.BlockSpec((None, 2, t // FOX_KB, FOX_VT_ROWS, FOX_KB), lambda b, p, i: (b, p, 0, 0, 0))],
        out_specs=pl.BlockSpec((None, nsub * FOX_QB, LANES), lambda b, p, i: (b, i, p)),
        out_shape=jax.ShapeDtypeStruct((bsz, t, FOX_WIDTH), BF16),
        scratch_shapes=[pltpu.VMEM((nch, 1, FOX_QB), F32),
                        pltpu.VMEM((nch, FOX_VT_ROWS, FOX_QB), F32),
                        pltpu.VMEM((nch, nsub * FOX_QB, FOX_QB), F32)],
        compiler_params=_params(("parallel", "parallel", "arbitrary"), "fox_attn_prompt"),
        name="fox_attn_prompt",
    )(qt, kp, vt)


def _fox_sample_body(q_ref, kn_ref, vn_ref, lfn_ref, kc_ref, vc_ref, lfp_ref, o_ref, *, tn, past, seg):
    r, c = _tri_mask(seg)
    upper = (r <= c).astype(BF16)
    carry = jnp.zeros((FOX_HEADS, 1), F32)
    cps = []
    for j in range(past // seg):
        cs = carry + _dot_sel_rhs(lfp_ref[:, j * seg:(j + 1) * seg], upper)
        cps.append(cs)
        carry = cs[:, seg - 1:seg]
    cp = jnp.concatenate(cps, axis=1) if len(cps) > 1 else cps[0]
    lfn = lfn_ref[...]
    if tn < LANES:
        lfn = jnp.concatenate([lfn, jnp.zeros((LANES - tn, LANES), F32)], axis=0)
    r, c = _tri_mask(LANES)
    cn = _dot_sel_lhs((r >= c).astype(BF16), lfn)
    cn_t = cn.T

    q = q_ref[...]
    kn = kn_ref[...].astype(BF16)
    vn = vn_ref[...].astype(BF16)
    r, c = _tri_mask(tn)
    outs = []
    for hh in range(FOX_HEADS):
        hs = slice(hh * FOX_HEAD_DIM, (hh + 1) * FOX_HEAD_DIM)
        qh = q[:, hs]
        cq = cn[0:tn, hh:hh + 1]
        s_past = _dot(qh, kc_ref[hh].astype(BF16)) + ((carry[hh:hh + 1, :] + cq) - cp[hh:hh + 1, :])
        s_new = _dot_nt(qh, kn[:, hs]) + (cq - cn_t[hh:hh + 1, 0:tn])
        s_new = jnp.where(r >= c, s_new, -jnp.inf)
        m = jnp.maximum(jnp.max(s_past, axis=-1, keepdims=True), jnp.max(s_new, axis=-1, keepdims=True))
        p_past = jnp.exp(s_past - m)
        p_new = jnp.exp(s_new - m)
        denom = jnp.sum(p_past, axis=-1, keepdims=True) + jnp.sum(p_new, axis=-1, keepdims=True)
        o = _dot_nt(p_past.astype(BF16), vc_ref[hh].astype(BF16)) + _dot(p_new.astype(BF16), vn[:, hs])
        outs.append(o / denom)
    o_ref[...] = jnp.concatenate(outs, axis=1).astype(o_ref.dtype)


def _fox_sample_call(q, kn_all, vn_all, lfn, cache_k, cache_v, lfp_t, layer):
    bsz, tn, _ = q.shape
    past = cache_k.shape[-1]
    seg = _row_tile(past, 512)
    bt = lambda w_: pl.BlockSpec((None, tn, w_), lambda b: (b, 0, 0))
    new = pl.BlockSpec((None, None, tn, FOX_WIDTH), lambda b: (layer, b, 0, 0))
    cache = pl.BlockSpec((None, None, FOX_HEADS, FOX_HEAD_DIM, past), lambda b: (layer, b, 0, 0, 0))
    return pl.pallas_call(
        functools.partial(_fox_sample_body, tn=tn, past=past, seg=seg), grid=(bsz,),
        in_specs=[bt(FOX_WIDTH), new, new, bt(LANES), cache, cache,
                  pl.BlockSpec((None, None, FOX_HEADS, past), lambda b: (layer, b, 0, 0))],
        out_specs=bt(FOX_WIDTH),
        out_shape=jax.ShapeDtypeStruct((bsz, tn, FOX_WIDTH), BF16),
        compiler_params=_params(("parallel",), "fox_attn_sample"),
        name="fox_attn_sample",
    )(q, kn_all, vn_all, lfn, cache_k, cache_v, lfp_t)


def _pad_cols(a, width):
    return jnp.pad(a, ((0, 0), (0, width - a.shape[1])))


def _layer_params(l, norm_ffn1, w1_gate, w1_up, w1_down, norm_mix, w_in, ssd_conv_w, ssd_conv_b,
                  ssd_dt_bias, ssd_a_log, ssd_d, ssd_norm, gla_w_gate, gla_b_gate, gla_norm,
                  fox_q_norm, fox_k_norm, fox_f_bias, w_out, norm_ffn2, w2_gate, w2_up, w2_down):
    wi = w_in[l]
    row = lambda a: a.reshape(1, -1).astype(F32)
    hk = GLA_HEADS * GLA_DK
    p = dict(
        ffn1=(row(norm_ffn1[l]), w1_gate[l].astype(BF16), w1_up[l].astype(BF16), w1_down[l].astype(BF16)),
        ffn2=(row(norm_ffn2[l]), w2_gate[l].astype(BF16), w2_up[l].astype(BF16), w2_down[l].astype(BF16)),
        norm_mix=row(norm_mix[l]),
        w_out=w_out[l].astype(BF16),
        w_ssd=jnp.concatenate([wi[:, _OFF_Z:_OFF_DT], _pad_cols(wi[:, _OFF_DT:_OFF_GQ], LANES)], axis=1).astype(BF16),
        conv_w=ssd_conv_w[l].astype(F32),
        conv_b=row(ssd_conv_b[l]),
        dt_bias=_pad_cols(row(ssd_dt_bias[l]), LANES),
        a_log=_pad_cols(row(ssd_a_log[l]), LANES),
        d_exp=row(jnp.repeat(ssd_d[l], SSD_HEAD_DIM)),
        ssd_norm=row(ssd_norm[l]),
        w_gla=jnp.concatenate([wi[:, _OFF_GQ:_OFF_GR], _pad_cols(wi[:, _OFF_GR:_OFF_FQ], LANES)], axis=1).astype(BF16),
        w_gate=jnp.pad(gla_w_gate[l], ((0, LANES - GLA_RANK), (0, 0))).astype(BF16),
        b_gate=row(gla_b_gate[l]),
        gla_norm=row(jnp.tile(gla_norm[l], GLA_HEADS)),
        w_fox=jnp.concatenate([wi[:, _OFF_FQ:_OFF_FF], _pad_cols(wi[:, _OFF_FF:], LANES)], axis=1).astype(BF16),
        q_norm=row(jnp.tile(fox_q_norm[l], FOX_HEADS)),
        k_norm=row(jnp.tile(fox_k_norm[l], FOX_HEADS)),
        f_bias=_pad_cols(row(fox_f_bias[l]), LANES),
    )
    return p


def _gla_state_in(s):
    bsz = s.shape[0]
    eye = jnp.eye(GLA_HEADS, dtype=s.dtype)
    full = jnp.einsum('bhkv,hg->bhvgk', s, eye)
    return full.reshape(bsz, GLA_HEADS * GLA_DV, GLA_HEADS * GLA_DK)


def _gla_state_out(st):
    bsz = st.shape[0]
    full = st.reshape(bsz, GLA_HEADS, GLA_DV, GLA_HEADS, GLA_DK)
    idx = jnp.arange(GLA_HEADS)
    diag = full[:, idx, :, idx, :]
    return jnp.transpose(diag, (1, 0, 3, 2))


def _mix(x, p, conv_prev, ssd_h0, gla_s0, fox_cache, layer, depth, kv_bufs, tiles):
    bsz, t, _ = x.shape
    cprev = jnp.pad(conv_prev, ((0, 0), (SUBLANES - (SSD_CONV - 1), 0), (0, 0)))
    y_ssd, cnew, hnew = _ssd_call(x, p['norm_mix'], p['w_ssd'], p['conv_w'], p['conv_b'], p['dt_bias'],
                                  p['a_log'], p['d_exp'], p['ssd_norm'], cprev,
                                  ssd_h0.reshape(bsz, SSD_WIDTH, SSD_STATE), tiles['ssd'], tiles['ssd_rows'])
    o_gla, snew = _gla_call(x, p['norm_mix'], p['w_gla'], p['w_gate'], p['b_gate'], p['gla_norm'],
                            _gla_state_in(gla_s0), tiles['gla'], tiles['gla_rows'])
    proj = functools.partial(_fox_proj_call, x, p['norm_mix'], p['w_fox'], p['q_norm'], p['k_norm'], p['f_bias'],
                             tiles['fox_proj'], tiles['fox_rows'], layer=layer, depth=depth, kv_bufs=kv_bufs)
    if fox_cache is None:
        k_all, v_all, lf_t, qt, kp, vt = proj(attn_layout=True)
        o_fox = _fox_prompt_call(qt, kp, vt, tiles['fox_q'])
        lf_state = jnp.swapaxes(lf_t[:, 0:FOX_HEADS, :], 1, 2)
    else:
        k_all, v_all, lf, qs = proj(attn_layout=False)
        cache_k, cache_v, lfp_t = fox_cache
        o_fox = _fox_sample_call(qs, k_all, v_all, lf, cache_k, cache_v, lfp_t, layer)
        lf_state = lf[:, :, 0:FOX_HEADS]
    state = (cnew[:, SUBLANES - (SSD_CONV - 1):, :],
             hnew.reshape(bsz, SSD_HEADS, SSD_HEAD_DIM, SSD_STATE),
             _gla_state_out(snew),
             lf_state)
    return (y_ssd, o_gla, o_fox), state, (k_all, v_all)


def _trunk_layer(x, p, conv_prev, ssd_h0, gla_s0, fox_cache, layer, depth, kv_bufs, tiles):
    bsz, t, d = x.shape
    x1 = _ffn_call(x.reshape(bsz * t, d), *p['ffn1'], tiles['ffn']).reshape(bsz, t, d)
    (y_ssd, o_gla, o_fox), state, kv_bufs = _mix(x1, p, conv_prev, ssd_h0, gla_s0, fox_cache, layer, depth,
                                                 kv_bufs, tiles)
    flat = lambda a: a.reshape(bsz * t, a.shape[-1])
    x3 = _out_ffn_call(flat(x1), flat(y_ssd), flat(o_gla), flat(o_fox), p['w_out'], *p['ffn2'], tiles['ffn'])
    return x3.reshape(bsz, t, d), state, kv_bufs


PROMPT_TILES = dict(ffn=1024, ssd=256, ssd_rows=4, gla=256, gla_rows=4, fox_proj=256, fox_rows=4,
                    fox_q=1024)
SAMPLE_TILES = dict(ffn=512, ssd=64, ssd_rows=4, gla=64, gla_rows=4, fox_proj=64, fox_rows=8)


def kernel(x_prompt, x_sample, state_ssd_conv, state_ssd, state_gla, cache_fox_k, cache_fox_v, cache_fox_logf, norm_ffn1, w1_gate, w1_up, w1_down, norm_mix, w_in, ssd_conv_w, ssd_conv_b, ssd_dt_bias, ssd_a_log, ssd_d, ssd_norm, gla_w_gate, gla_b_gate, gla_norm, fox_q_norm, fox_k_norm, fox_f_bias, w_out, norm_ffn2, w2_gate, w2_up, w2_down):
    depth = w_in.shape[0]
    bp = x_prompt.shape[0]
    weights = (norm_ffn1, w1_gate, w1_up, w1_down, norm_mix, w_in, ssd_conv_w, ssd_conv_b, ssd_dt_bias,
               ssd_a_log, ssd_d, ssd_norm, gla_w_gate, gla_b_gate, gla_norm, fox_q_norm, fox_k_norm,
               fox_f_bias, w_out, norm_ffn2, w2_gate, w2_up, w2_down)
    cache_k = jnp.transpose(cache_fox_k, (0, 1, 3, 4, 2))
    cache_v = jnp.transpose(cache_fox_v, (0, 1, 3, 4, 2))
    lfp_t = jnp.swapaxes(cache_fox_logf, 2, 3)
    zeros_conv = jnp.zeros((bp, SSD_CONV - 1, SSD_CONV_CH), F32)
    zeros_ssd = jnp.zeros((bp, SSD_HEADS, SSD_HEAD_DIM, SSD_STATE), F32)
    zeros_gla = jnp.zeros((bp, GLA_HEADS, GLA_DK, GLA_DV), F32)

    xp, xs = x_prompt, x_sample
    p_new = [[] for _ in range(4)]
    s_new = [[] for _ in range(4)]
    kv_p = kv_s = None
    for l in range(depth):
        p = _layer_params(l, *weights)
        xp, st_p, kv_p = _trunk_layer(xp, p, zeros_conv, zeros_ssd, zeros_gla, None, l, depth, kv_p, PROMPT_TILES)
        xs, st_s, kv_s = _trunk_layer(xs, p, state_ssd_conv[l], state_ssd[l], state_gla[l],
                                      (cache_k, cache_v, lfp_t), l, depth, kv_s, SAMPLE_TILES)
        for i in range(4):
            p_new[i].append(st_p[i])
            s_new[i].append(st_s[i])

    def leaves(small, kv):
        conv, ssd, gla, logf = [jnp.stack(a) for a in small]
        heads = lambda a: a.reshape(a.shape[:3] + (FOX_HEADS, FOX_HEAD_DIM))
        return conv, ssd, gla, heads(kv[0]), heads(kv[1]), logf

    return (xp, xs, *leaves(p_new, kv_p), *leaves(s_new, kv_s))
```
